```python
import math
import jax, jax.numpy as jnp
from jax import lax
import numpy as np

D_MODEL = 1024
BATCH = 8
SEQ = 2048
DEPTH = 2
DEC_BATCH = 32
DEC_SEQ = 2048
PAST_LEN = 128

PE_DIM = 256
EPS = 1e-6
NEG_BIG = -1e30
N_BRANCH = 4
BRANCH_W = 256
GLA_HEADS = 4
GLA_DK = 32
GLA_DV = 64
GLA_RANK = 16
GLA_NORMALIZER = 16.0
GLA_CHUNK = 32
MLSTM_HEADS = 4
MLSTM_DH = 64
MLSTM_CONV = 3
MLSTM_CHUNK = 64
MLSTM_FGATE_OFFSET = 3.0
DIL_HEADS = 4
DIL_DH = 64
DIL_PATTERNS = ((128, 1), (512, 4), (2048, 16))
ROPE_THETA = 10000.0
HGRN_HEADS = 4
HGRN_DF = 64
HGRN_DV = 64
HGRN_CHUNK = 32
PEER_HEADS = 8
PEER_NKEYS = 128
PEER_N = PEER_NKEYS * PEER_NKEYS
PEER_DKEY = 256
PEER_TOPK = 16
PEER_BLOCK = 128

PROJ_LAYOUT = (
    ('a_q', GLA_HEADS * GLA_DK), ('a_k', GLA_HEADS * GLA_DK), ('a_v', GLA_HEADS * GLA_DV),
    ('a_g', GLA_HEADS * GLA_DV), ('a_lr', 2 * GLA_RANK),
    ('b_q', MLSTM_HEADS * MLSTM_DH), ('b_k', MLSTM_HEADS * MLSTM_DH), ('b_v', MLSTM_HEADS * MLSTM_DH),
    ('b_i', 2 * MLSTM_HEADS), ('b_f', 2 * MLSTM_HEADS), ('b_o', MLSTM_HEADS * MLSTM_DH),
    ('c_q', DIL_HEADS * DIL_DH), ('c_k', DIL_HEADS * DIL_DH), ('c_v', DIL_HEADS * DIL_DH),
    ('d_q', HGRN_HEADS * HGRN_DF), ('d_f', 2 * HGRN_HEADS * HGRN_DF), ('d_i', HGRN_HEADS * HGRN_DV),
    ('d_g', HGRN_HEADS * HGRN_DV),
)
PROJ_NAMES = tuple(n for n, _ in PROJ_LAYOUT)
PROJ_SIZES = tuple(s for _, s in PROJ_LAYOUT)
PROJ_WIDTH = sum(PROJ_SIZES)

kernel_name = 'hybrid_bidir_gla_mlstm_dilattn_hgrn2_peer'

f32 = jnp.float32


def _proj_offset(name):
    return sum(PROJ_SIZES[:PROJ_NAMES.index(name)])


def _split_proj(z):
    offs = [int(o) for o in np.cumsum(PROJ_SIZES)[:-1]]
    return dict(zip(PROJ_NAMES, jnp.split(z, offs, axis=-1)))


def rmsnorm(x, g):
    xf = x.astype(f32)
    y = xf * lax.rsqrt(jnp.mean(xf * xf, axis=-1, keepdims=True) + EPS)
    return (y * g.astype(f32)).astype(x.dtype)


def head_rmsnorm(x, g):
    H, dh = x.shape[-2:]
    xf = x.astype(f32)
    y = xf * lax.rsqrt(jnp.mean(xf * xf, axis=-1, keepdims=True) + EPS)
    return (y * g.astype(f32).reshape(H, dh)).reshape(x.shape[:-2] + (H * dh,))


def _rev(t):
    return jnp.flip(t, axis=1)


def rope(x):
    S, dh = x.shape[1], x.shape[-1]
    half = dh // 2
    inv = ROPE_THETA ** (-jnp.arange(half, dtype=f32) / half)
    ang = jnp.arange(S, dtype=f32)[:, None] * inv[None, :]
    cos = jnp.cos(ang)[None, :, None, :]
    sin = jnp.sin(ang)[None, :, None, :]
    x1 = x[..., :half].astype(f32)
    x2 = x[..., half:].astype(f32)
    return jnp.concatenate([x1 * cos - x2 * sin, x2 * cos + x1 * sin], axis=-1).astype(x.dtype)


def centred_dwconv(x, w, b):
    K = w.shape[0]
    r = K // 2
    S = x.shape[1]
    xp = jnp.pad(x, ((0, 0), (r, r), (0, 0)))
    y = b + xp[:, 0:S] * w[0]
    for j in range(1, K):
        y = y + xp[:, j:j + S] * w[j]
    return y


def chunked_gated_linear(q, k, v, log_f, chunk):
    B, S, H, dk = q.shape
    dv = v.shape[-1]
    n = S // chunk

    def to_chunks(t):
        return t.astype(f32).reshape(B, n, chunk, H, t.shape[-1]).transpose(1, 0, 3, 2, 4)

    qc, kc, vc, gc = to_chunks(q), to_chunks(k), to_chunks(v), to_chunks(log_f)
    mask = jnp.tril(jnp.ones((chunk, chunk), dtype=bool))

    def step(state, inp):
        qi, ki, vi, gi = inp
        b = jnp.cumsum(gi, axis=2)
        o_inter = jnp.einsum('bhtk,bhkv->bhtv', qi * jnp.exp(b), state)
        diff = b[:, :, :, None, :] - b[:, :, None, :, :]
        decay = jnp.exp(jnp.where(mask[:, :, None], diff, NEG_BIG))
        attn = jnp.einsum('bhtk,bhsk,bhtsk->bhts', qi, ki, decay)
        o = o_inter + jnp.einsum('bhts,bhsv->bhtv', attn, vi)
        b_last = b[:, :, -1:, :]
        k_dec = ki * jnp.exp(b_last - b)
        state = jnp.exp(b_last[:, :, 0, :])[..., None] * state + jnp.einsum('bhsk,bhsv->bhkv', k_dec, vi)
        return state, o

    init = jnp.zeros((B, H, dk, dv), f32)
    _, o = lax.scan(step, init, (qc, kc, vc, gc))
    return o.transpose(1, 0, 3, 2, 4).reshape(B, S, H, dv)


def bidir_gated_linear(q, k_f, k_b, v, g_f, g_b, chunk):
    fwd = chunked_gated_linear(q, k_f, v, g_f, chunk)
    bwd = chunked_gated_linear(_rev(q), _rev(k_b), _rev(v), _rev(g_b), chunk)
    return fwd + _rev(bwd)


def chunked_mlstm(q, k, v, i_pre, log_f, chunk):
    B, S, H, d = q.shape
    n = S // chunk

    def to_chunks(t):
        return t.astype(f32).reshape(B, n, chunk, H, d).transpose(1, 0, 3, 2, 4)

    def gate_chunks(t):
        return t.astype(f32).reshape(B, n, chunk, H).transpose(1, 0, 3, 2)

    mask = jnp.tril(jnp.ones((chunk, chunk), dtype=bool))

    def step(carry, inp):
        Cm, nv, m = carry
        qi, ki, vi, ii, fi = inp
        b = jnp.cumsum(fi, axis=-1)
        logD = jnp.where(mask, b[..., :, None] - b[..., None, :] + ii[..., None, :], NEG_BIG)
        log_prev = b + m[..., None]
        m_t = jnp.maximum(jnp.max(logD, axis=-1), log_prev)
        w_prev = jnp.exp(log_prev - m_t)
        s = jnp.einsum('bhtd,bhsd->bhts', qi, ki) * jnp.exp(logD - m_t[..., None])
        num = w_prev[..., None] * jnp.einsum('bhtd,bhde->bhte', qi, Cm) + jnp.einsum('bhts,bhse->bhte', s, vi)
        den = w_prev * jnp.einsum('bhtd,bhd->bht', qi, nv) + jnp.sum(s, axis=-1)
        h = num / jnp.maximum(jnp.abs(den), jnp.exp(-m_t))[..., None]
        b_last = b[..., -1]
        log_w = b_last[..., None] - b + ii
        m_new = jnp.maximum(b_last + m, jnp.max(log_w, axis=-1))
        w_s = jnp.exp(log_w - m_new[..., None])
        dec = jnp.exp(b_last + m - m_new)
        Cm = dec[..., None, None] * Cm + jnp.einsum('bhs,bhsd,bhse->bhde', w_s, ki, vi)
        nv = dec[..., None] * nv + jnp.einsum('bhs,bhsd->bhd', w_s, ki)
        return (Cm, nv, m_new), h

    init = (jnp.zeros((B, H, d, d), f32), jnp.zeros((B, H, d), f32), jnp.zeros((B, H), f32))
    _, h = lax.scan(step, init, (to_chunks(q), to_chunks(k), to_chunks(v), gate_chunks(i_pre), gate_chunks(log_f)))
    return h.transpose(1, 0, 3, 2, 4).reshape(B, S, H, d)


def dilated_pattern(q, k, v, window, dilation):
    B, S, H, dh = q.shape
    R = window // (2 * dilation)
    L = S // dilation
    nb = -(-L // R)
    Lp = nb * R

    def sub(t):
        return t.reshape(B, L, dilation, H, dh).transpose(0, 2, 1, 3, 4)

    def kv_blocks(t):
        tp = jnp.pad(sub(t), ((0, 0), (0, 0), (R, Lp - L + R), (0, 0), (0, 0))).reshape(B, dilation, nb + 2, R, H, dh)
        return jnp.concatenate([tp[:, :, :-2], tp[:, :, 1:-1], tp[:, :, 2:]], axis=3)

    qb = jnp.pad(sub(q), ((0, 0), (0, 0), (0, Lp - L), (0, 0), (0, 0))).reshape(B, dilation, nb, R, H, dh)
    kb, vb = kv_blocks(k), kv_blocks(v)
    qpos = jnp.arange(nb)[:, None] * R + jnp.arange(R)[None, :]
    kpos = jnp.arange(nb)[:, None] * R - R + jnp.arange(3 * R)[None, :]
    rel = kpos[:, None, :] - qpos[:, :, None]
    valid = (jnp.abs(rel) <= R) & (kpos[:, None, :] >= 0) & (kpos[:, None, :] < L)
    s = jnp.einsum('bdnqhe,bdnkhe->bdnhqk', qb, kb).astype(f32) * (dh ** -0.5)
    s = jnp.where(valid[None, None, :, None], s, NEG_BIG)
    m = jnp.max(s, axis=-1, keepdims=True)
    p = jnp.exp(s - m)
    l = jnp.sum(p, axis=-1, keepdims=True)
    o = jnp.einsum('bdnhqk,bdnkhe->bdnqhe', p / l, vb.astype(f32))
    lse = (m + jnp.log(l))[..., 0]
    o = o.reshape(B, dilation, Lp, H, dh)[:, :, :L].transpose(0, 2, 1, 3, 4).reshape(B, S, H, dh)
    lse = lse.transpose(0, 1, 2, 4, 3).reshape(B, dilation, Lp, H)[:, :, :L].transpose(0, 2, 1, 3).reshape(B, S, H)
    return o, lse


def dilated_attention(q, k, v):
    outs, lses = [], []
    for window, dilation in DIL_PATTERNS:
        o, lse = dilated_pattern(q, k, v, window, dilation)
        outs.append(o)
        lses.append(lse)
    wts = jax.nn.softmax(jnp.stack(lses, axis=0), axis=0)
    return jnp.sum(wts[..., None] * jnp.stack(outs, axis=0), axis=0)


def peer_ffn(u, w_pq, sub_keys, peer_u, peer_v):
    B, S, D = u.shape
    xs = u.reshape((B * S) // PEER_BLOCK, PEER_BLOCK, D)

    def block(xb):
        TB = xb.shape[0]
        q = (xb @ w_pq).reshape(TB, PEER_HEADS, 2, PEER_DKEY // 2)
        sc = jnp.einsum('thpc,hpnc->thpn', q, sub_keys).astype(f32)
        v1, i1 = lax.top_k(sc[:, :, 0], PEER_TOPK)
        v2, i2 = lax.top_k(sc[:, :, 1], PEER_TOPK)
        cand = (v1[..., :, None] + v2[..., None, :]).reshape(TB, PEER_HEADS, PEER_TOPK * PEER_TOPK)
        cid = (i1[..., :, None] * PEER_NKEYS + i2[..., None, :]).reshape(TB, PEER_HEADS, PEER_TOPK * PEER_TOPK)
        top_s, pos = lax.top_k(cand, PEER_TOPK)
        eid = jnp.take_along_axis(cid, pos, axis=-1)
        g = jax.nn.softmax(top_s, axis=-1)
        ue = peer_u[eid]
        ve = peer_v[eid]
        a = jax.nn.gelu(jnp.einsum('thkd,td->thk', ue, xb).astype(f32), approximate=False)
        return jnp.einsum('thk,thkd->td', (g * a).astype(xb.dtype), ve)

    return lax.map(block, xs).reshape(B, S, D)


def mixer_sublayer(u, w_in, b_in, w_lr2, b_lr2, conv_w, conv_b, lb, g_a, g_b, g_d, w_gate, w_br, w_out):
    B, S, _ = u.shape
    z = _split_proj(u @ w_in + b_in)

    qa = z['a_q'].reshape(B, S, GLA_HEADS, GLA_DK) * (GLA_DK ** -0.5)
    ka = z['a_k'].reshape(B, S, GLA_HEADS, GLA_DK)
    va = z['a_v'].reshape(B, S, GLA_HEADS, GLA_DV)
    lr = z['a_lr'].reshape(B, S, 2, GLA_RANK)
    gk = jax.nn.log_sigmoid((jnp.einsum('bsnr,nrk->bsnk', lr, w_lr2) + b_lr2).astype(f32)) / GLA_NORMALIZER
    gk = gk.reshape(B, S, 2, GLA_HEADS, GLA_DK)
    oa = bidir_gated_linear(qa, ka, ka, va, gk[:, :, 0], gk[:, :, 1], GLA_CHUNK)
    ya = head_rmsnorm(oa, g_a) * jax.nn.silu(z['a_g'].astype(f32))

    qk = jax.nn.silu(centred_dwconv(jnp.concatenate([z['b_q'], z['b_k']], axis=-1), conv_w, conv_b))
    qb, kb = jnp.split(qk, 2, axis=-1)
    qb = qb.reshape(B, S, MLSTM_HEADS, MLSTM_DH)
    kb = kb.reshape(B, S, MLSTM_HEADS, MLSTM_DH) * (MLSTM_DH ** -0.5)
    vb = z['b_v'].reshape(B, S, MLSTM_HEADS, MLSTM_DH)
    ig = z['b_i'].reshape(B, S, 2, MLSTM_HEADS).astype(f32)
    lf = jax.nn.log_sigmoid(z['b_f'].reshape(B, S, 2, MLSTM_HEADS).astype(f32))
    hf = chunked_mlstm(qb, kb, vb, ig[:, :, 0], lf[:, :, 0], MLSTM_CHUNK)
    hb = chunked_mlstm(_rev(qb), _rev(kb), _rev(vb), _rev(ig[:, :, 1]), _rev(lf[:, :, 1]), MLSTM_CHUNK)
    yb = jax.nn.sigmoid(z['b_o'].astype(f32)) * head_rmsnorm(hf + _rev(hb), g_b)

    qc = rope(z['c_q'].reshape(B, S, DIL_HEADS, DIL_DH))
    kc = rope(z['c_k'].reshape(B, S, DIL_HEADS, DIL_DH))
    vc = z['c_v'].reshape(B, S, DIL_HEADS, DIL_DH)
    yc = dilated_attention(qc, kc, vc).reshape(B, S, DIL_HEADS * DIL_DH)

    qd = jax.nn.silu(z['d_q'].astype(f32)).reshape(B, S, HGRN_HEADS, HGRN_DF)
    fz = z['d_f'].reshape(B, S, 2, HGRN_HEADS * HGRN_DF).astype(f32)
    lbf = lb[None, None].astype(f32)
    log_f = jnp.log(lbf + (1.0 - lbf) * jax.nn.sigmoid(fz))
    kd = (1.0 - lbf) * jax.nn.sigmoid(-fz)
    log_f = log_f.reshape(B, S, 2, HGRN_HEADS, HGRN_DF)
    kd = kd.reshape(B, S, 2, HGRN_HEADS, HGRN_DF)
    vd = z['d_i'].reshape(B, S, HGRN_HEADS, HGRN_DV)
    od = bidir_gated_linear(qd, kd[:, :, 0], kd[:, :, 1], vd, log_f[:, :, 0], log_f[:, :, 1], HGRN_CHUNK)
    yd = head_rmsnorm(od, g_d) * jax.nn.silu(z['d_g'].astype(f32))

    merged = None
    for n, y in enumerate((ya, yb, yc, yd)):
        term = jax.nn.sigmoid(u @ w_gate[n]) * (y.astype(u.dtype) @ w_br[n])
        merged = term if merged is None else merged + term
    return merged @ w_out


def trunk(x, p, g_mix, w_in, b_in, w_lr2, b_lr2, conv_w, conv_b, lb_param, g_a, g_b, g_d, w_gate, w_br,
          w_out, g_ffn, w_pq, sub_keys, peer_u, peer_v, g_pe, w_pe, w_pg, g_final):
    lbp = jax.nn.softmax(lb_param.astype(f32), axis=1)
    lb_all = jnp.cumsum(lbp, axis=1) - lbp[:, :1]
    h = x
    for l in range(DEPTH):
        u = rmsnorm(h, g_mix[l])
        h = h + mixer_sublayer(u, w_in[l], b_in[l], w_lr2[l], b_lr2[l], conv_w[l], conv_b[l], lb_all[:, l],
                               g_a[l], g_b[l], g_d[l], w_gate[l], w_br[l], w_out[l])
        h = h + peer_ffn(rmsnorm(h, g_ffn[l]), w_pq[l], sub_keys[l], peer_u[l], peer_v[l])
        gate = jax.nn.sigmoid(rmsnorm(h, g_pe[l]) @ w_pg[l])
        h = h + gate * (p[l] @ w_pe[l])
    return rmsnorm(h, g_final)


def setup_inputs(seed: int = 0) -> dict:
    key = jax.random.key(seed)
    ks = iter(jax.random.split(key, 32))

    def nrm(shape, scale):
        return jax.random.normal(next(ks), shape, jnp.float32) * scale

    def gain(shape):
        return 1.0 + nrm(shape, 0.02)

    L, D = DEPTH, D_MODEL
    x_prompt = nrm((BATCH, SEQ, D), 1.0)
    x_sample = nrm((DEC_BATCH, DEC_SEQ, D), 1.0)
    p_prompt = nrm((DEPTH, BATCH, SEQ, PE_DIM), 1.0)
    p_sample = nrm((DEPTH, DEC_BATCH, DEC_SEQ, PE_DIM), 1.0)
    b_in = nrm((L, PROJ_WIDTH), 0.02)
    fo = _proj_offset('b_f')
    b_in = b_in.at[:, fo:fo + 2 * MLSTM_HEADS].add(MLSTM_FGATE_OFFSET)
    return {
        'x_prompt': x_prompt,
        'x_sample': x_sample,
        'p_prompt': p_prompt,
        'p_sample': p_sample,
        'g_mix': gain((L, D)),
        'w_in': nrm((L, D, PROJ_WIDTH), D ** -0.5),
        'b_in': b_in,
        'w_lr2': nrm((L, 2, GLA_RANK, GLA_HEADS * GLA_DK), GLA_RANK ** -0.5),
        'b_lr2': nrm((L, 2, GLA_HEADS * GLA_DK), 0.02),
        'conv_w': nrm((L, MLSTM_CONV, 2 * MLSTM_HEADS * MLSTM_DH), MLSTM_CONV ** -0.5),
        'conv_b': nrm((L, 2 * MLSTM_HEADS * MLSTM_DH), 0.02),
        'lb_param': nrm((2, L, HGRN_HEADS * HGRN_DF), 0.5),
        'g_a': gain((L, GLA_HEADS * GLA_DV)),
        'g_b': gain((L, MLSTM_HEADS * MLSTM_DH)),
        'g_d': gain((L, HGRN_HEADS * HGRN_DV)),
        'w_gate': nrm((L, N_BRANCH, D, D), D ** -0.5),
        'w_br': nrm((L, N_BRANCH, BRANCH_W, D), BRANCH_W ** -0.5),
        'w_out': nrm((L, D, D), D ** -0.5),
        'g_ffn': gain((L, D)),
        'w_pq': nrm((L, D, PEER_HEADS * PEER_DKEY), D ** -0.5),
        'sub_keys': nrm((L, PEER_HEADS, 2, PEER_NKEYS, PEER_DKEY // 2), (PEER_DKEY // 2) ** -0.5),
        'peer_u': nrm((L, PEER_N, D), D ** -0.5),
        'peer_v': nrm((L, PEER_N, D), 0.35),
        'g_pe': gain((L, D)),
        'w_pe': nrm((L, PE_DIM, D), PE_DIM ** -0.5),
        'w_pg': nrm((L, D, D), D ** -0.5),
        'g_final': gain((D,)),
    }


def reference(x_prompt, x_sample, p_prompt, p_sample, g_mix, w_in, b_in, w_lr2, b_lr2, conv_w, conv_b,
              lb_param, g_a, g_b, g_d, w_gate, w_br, w_out, g_ffn, w_pq, sub_keys, peer_u, peer_v,
              g_pe, w_pe, w_pg, g_final):
    y_prompt = trunk(x_prompt, p_prompt, g_mix, w_in, b_in, w_lr2, b_lr2, conv_w, conv_b, lb_param, g_a, g_b,
                     g_d, w_gate, w_br, w_out, g_ffn, w_pq, sub_keys, peer_u, peer_v, g_pe, w_pe, w_pg, g_final)
    y_sample = trunk(x_sample, p_sample, g_mix, w_in, b_in, w_lr2, b_lr2, conv_w, conv_b, lb_param, g_a, g_b,
                     g_d, w_gate, w_br, w_out, g_ffn, w_pq, sub_keys, peer_u, peer_v, g_pe, w_pe, w_pg, g_final)
    return (y_prompt, y_sample)
```

```python
import functools
import math

import numpy as np
import jax
import jax.numpy as jnp
from jax import lax
from jax.experimental import pallas as pl
from jax.experimental.pallas import tpu as pltpu

f32 = jnp.float32
bf16 = jnp.bfloat16
i32 = jnp.int32

D_MODEL = 1024
DEPTH = 2
PE_DIM = 256
EPS = 1e-6
NEG_BIG = -1e30
HEADS = 4
HEAD_W = 64
BRANCH_W = HEADS * HEAD_W
GLA_DK = 32
GLA_RANK = 16
GLA_NORMALIZER = 16.0
GLA_CHUNK = 32
MLSTM_CHUNK = 64
HGRN_DF = 64
HGRN_CHUNK = 32
DIL_PATTERNS = ((128, 1), (512, 4), (2048, 16))
ROPE_THETA = 10000.0
PEER_HEADS = 8
PEER_NKEYS = 128
PEER_TOPK = 16
LANES = 128
EXP_CLAMP = 80.0

_PROJ_LAYOUT = (
    ('a_q', 128), ('a_k', 128), ('a_v', 256), ('a_g', 256), ('a_lr', 32),
    ('b_q', 256), ('b_k', 256), ('b_v', 256), ('b_i', 8), ('b_f', 8), ('b_o', 256),
    ('c_q', 256), ('c_k', 256), ('c_v', 256),
    ('d_q', 256), ('d_f', 512), ('d_i', 256), ('d_g', 256),
)
_PROJ_OFF = {}
_o = 0
for _n, _s in _PROJ_LAYOUT:
    _PROJ_OFF[_n] = (_o, _o + _s)
    _o += _s

ZA_W = 896
ZB_W = 2048
ZC_W = 768
ZD_W = 1280

VMEM_LIMIT = 56 * 1024 * 1024


def _cparams(*sem):
    return pltpu.CompilerParams(dimension_semantics=sem, vmem_limit_bytes=VMEM_LIMIT)


def _const_spec(shape):
    nd = len(shape)
    return pl.BlockSpec(shape, lambda *_: (0,) * nd, pipeline_mode=pl.Buffered(1))


def _mm(a, b):
    return jnp.dot(a, b, preferred_element_type=f32)


def _mm_nt(a, b):
    return lax.dot_general(a, b, (((1,), (1,)), ((), ())), preferred_element_type=f32)


def _mm_tn(a, b):
    return lax.dot_general(a, b, (((0,), (0,)), ((), ())), preferred_element_type=f32)


def _split3(x):
    hi = x.astype(bf16)
    r1 = x - hi.astype(f32)
    mid = r1.astype(bf16)
    lo = (r1 - mid.astype(f32)).astype(bf16)
    return hi, mid, lo


def _lmul3(m, x):
    hi, mid, lo = _split3(x)
    return _mm(m, hi) + _mm(m, mid) + _mm(m, lo)


def _rmul3(x, m):
    hi, mid, lo = _split3(x)
    return _mm(hi, m) + _mm(mid, m) + _mm(lo, m)


def _rms(x, g):
    return x * lax.rsqrt(jnp.mean(x * x, axis=-1, keepdims=True) + EPS) * g


def _iota(shape, dim):
    return lax.broadcasted_iota(i32, shape, dim)


def _tri(c, reverse):
    r, k = _iota((c, c), 0), _iota((c, c), 1)
    return jnp.where((k >= r) if reverse else (k <= r), 1.0, 0.0).astype(bf16)


def _block_avg(w, blk):
    same = (_iota((w, w), 0) // blk) == (_iota((w, w), 1) // blk)
    return jnp.where(same, 1.0 / blk, 0.0).astype(bf16)


def _rows(n, blk, fn):
    def body(i, c):
        fn(pl.multiple_of(i * blk, blk), i)
        return c
    lax.fori_loop(0, n // blk, body, 0)


def _head_rmsnorm(o, g, mavg):
    ms = _rmul3(o * o, mavg)
    return o * lax.rsqrt(ms + EPS) * g


def _proj_kernel(h_ref, g_ref, wa, wb, wc, wd, ba, bb, bc, bd, za, zb, zc, zd):
    u = _rms(h_ref[...], g_ref[...]).astype(bf16)
    for w, b, z in ((wa, ba, za), (wb, bb, zb), (wc, bc, zc), (wd, bd, zd)):
        z[...] = _mm(u, w[...]) + b[...]


def _proj(h, g, ws, bs, tm=512):
    t = h.shape[0]
    widths = (ZA_W, ZB_W, ZC_W, ZD_W)
    tok = lambda w: pl.BlockSpec((tm, w), lambda i: (i, 0))
    return pl.pallas_call(
        _proj_kernel,
        grid=(t // tm,),
        in_specs=[tok(D_MODEL), _const_spec((1, D_MODEL))]
        + [_const_spec((D_MODEL, w)) for w in widths] + [_const_spec((1, w)) for w in widths],
        out_specs=[tok(w) for w in widths],
        out_shape=[jax.ShapeDtypeStruct((t, w), f32) for w in widths],
        compiler_params=_cparams("parallel"),
        name="proj",
    )(h, g, *ws, *bs)


def _gl_pass(load, acc_ref, st_ref, *, s, c, dk, reverse, first):
    h, dv = HEADS, HEAD_W
    kw, vw, hc = h * dk, h * dv, h * c
    n = s // c
    tri = _tri(c, reverse)
    khead = _iota((c, kw), 1) // dk
    vhead = _iota((c, vw), 1) // dv
    sidx, trow = _iota((c, hc), 1) % c, _iota((c, hc), 0)
    causal = (sidx >= trow) if reverse else (sidx <= trow)
    bd = (_iota((vw, kw), 0) // dv) == (_iota((vw, kw), 1) // dk)
    st_ref[...] = jnp.zeros((vw, kw), f32)

    def body(ci, carry):
        cidx = (n - 1 - ci) if reverse else ci
        st = pl.multiple_of(cidx * c, c)
        q, k, v, g = load(st)
        b = _lmul3(tri, g)
        last = b[0:1] if reverse else b[c - 1:c]
        mid = b[c // 2:c // 2 + 1]
        qt = (q * jnp.exp(jnp.minimum(b - mid, EXP_CLAMP))).astype(bf16)
        kt = k * jnp.exp(jnp.minimum(mid - b, EXP_CLAMP))
        kexp = jnp.concatenate([jnp.where(khead == j, kt, 0.0).astype(bf16) for j in range(h)], axis=0)
        vexp = jnp.concatenate([jnp.where(vhead == j, v, 0.0).astype(bf16) for j in range(h)], axis=0)
        att = jnp.where(causal, _mm_nt(qt, kexp), 0.0).astype(bf16)
        o = _mm(att, vexp)
        state = st_ref[...]
        o = o + _mm_nt((q * jnp.exp(b)).astype(bf16), state.astype(bf16))
        kd = (k * jnp.exp(last - b)).astype(bf16)
        upd = _mm_tn(v.astype(bf16), kd)
        st_ref[...] = state * jnp.exp(last) + jnp.where(bd, upd, 0.0)
        if first:
            acc_ref[pl.ds(st, c), :] = o
        else:
            acc_ref[pl.ds(st, c), :] += o
        return carry

    lax.fori_loop(0, n, body, 0)


def _gla_kernel(z_ref, wlr_ref, blr_ref, g_ref, o_ref, gk_ref, acc_ref, st_ref, *, s):
    rb = 256

    def gates(st, i):
        lr = z_ref[pl.ds(st, rb), 768:896].astype(bf16)
        x = _mm(lr, wlr_ref[...]) + blr_ref[...]
        gk_ref[pl.ds(st, rb), :] = jax.nn.log_sigmoid(x) * (1.0 / GLA_NORMALIZER)
    _rows(s, rb, gates)

    scale = GLA_DK ** -0.5
    for d in range(2):
        def load(st, d=d):
            q = z_ref[pl.ds(st, GLA_CHUNK), 0:128] * scale
            k = z_ref[pl.ds(st, GLA_CHUNK), 128:256]
            v = z_ref[pl.ds(st, GLA_CHUNK), 256:512]
            g = gk_ref[pl.ds(st, GLA_CHUNK), d * 128:(d + 1) * 128]
            return q, k, v, g
        _gl_pass(load, acc_ref, st_ref, s=s, c=GLA_CHUNK, dk=GLA_DK, reverse=(d == 1), first=(d == 0))

    mavg = _block_avg(BRANCH_W, HEAD_W)

    def out(st, i):
        y = _head_rmsnorm(acc_ref[pl.ds(st, rb), :], g_ref[...], mavg)
        gate = z_ref[pl.ds(st, rb), 512:768]
        o_ref[pl.ds(st, rb), :] = y * (gate * jax.nn.sigmoid(gate))
    _rows(s, rb, out)


def _gla(za, wlr, blr, g, nb, s):
    return pl.pallas_call(
        functools.partial(_gla_kernel, s=s),
        grid=(nb,),
        in_specs=[pl.BlockSpec((None, s, ZA_W), lambda b: (b, 0, 0)),
                  _const_spec((LANES, 256)), _const_spec((1, 256)), _const_spec((1, BRANCH_W))],
        out_specs=pl.BlockSpec((None, s, BRANCH_W), lambda b: (b, 0, 0)),
        out_shape=jax.ShapeDtypeStruct((nb, s, BRANCH_W), f32),
        scratch_shapes=[pltpu.VMEM((s, 256), f32), pltpu.VMEM((s, BRANCH_W), f32),
                        pltpu.VMEM((BRANCH_W, HEADS * GLA_DK), f32)],
        compiler_params=_cparams("parallel"),
        name="gla",
    )(za.reshape(nb, s, ZA_W), wlr, blr, g)


def _hgrn_kernel(z_ref, lbp_ref, g_ref, o_ref, q_ref, lf_ref, kd_ref, acc_ref, st_ref, *, s, layer):
    rb = 256
    x = lbp_ref[...]
    e = jnp.exp(x - jnp.max(x, axis=0, keepdims=True))
    p = e / jnp.sum(e, axis=0, keepdims=True)
    lb = p[0:1]
    for j in range(1, layer + 1):
        lb = lb + p[j:j + 1]
    lb = lb - p[0:1]

    def prep(st, i):
        qz = z_ref[pl.ds(st, rb), 0:256]
        q_ref[pl.ds(st, rb), :] = qz * jax.nn.sigmoid(qz)
        fz = z_ref[pl.ds(st, rb), 256:768]
        lf_ref[pl.ds(st, rb), :] = jnp.log(lb + (1.0 - lb) * jax.nn.sigmoid(fz))
        kd_ref[pl.ds(st, rb), :] = (1.0 - lb) * jax.nn.sigmoid(-fz)
    _rows(s, rb, prep)

    kw = HEADS * HGRN_DF
    for d in range(2):
        def load(st, d=d):
            q = q_ref[pl.ds(st, HGRN_CHUNK), :]
            k = kd_ref[pl.ds(st, HGRN_CHUNK), d * kw:(d + 1) * kw]
            v = z_ref[pl.ds(st, HGRN_CHUNK), 768:1024]
            g = lf_ref[pl.ds(st, HGRN_CHUNK), d * kw:(d + 1) * kw]
            return q, k, v, g
        _gl_pass(load, acc_ref, st_ref, s=s, c=HGRN_CHUNK, dk=HGRN_DF, reverse=(d == 1), first=(d == 0))

    mavg = _block_avg(BRANCH_W, HEAD_W)

    def out(st, i):
        y = _head_rmsnorm(acc_ref[pl.ds(st, rb), :], g_ref[...], mavg)
        gate = z_ref[pl.ds(st, rb), 1024:1280]
        o_ref[pl.ds(st, rb), :] = y * (gate * jax.nn.sigmoid(gate))
    _rows(s, rb, out)


def _hgrn(zd, lbp, g, nb, s, layer):
    kw = HEADS * HGRN_DF
    return pl.pallas_call(
        functools.partial(_hgrn_kernel, s=s, layer=layer),
        grid=(nb,),
        in_specs=[pl.BlockSpec((None, s, ZD_W), lambda b: (b, 0, 0)),
                  _const_spec((DEPTH, 2 * kw)), _const_spec((1, BRANCH_W))],
        out_specs=pl.BlockSpec((None, s, BRANCH_W), lambda b: (b, 0, 0)),
        out_shape=jax.ShapeDtypeStruct((nb, s, BRANCH_W), f32),
        scratch_shapes=[pltpu.VMEM((s, kw), f32), pltpu.VMEM((s, 2 * kw), f32), pltpu.VMEM((s, 2 * kw), f32),
                        pltpu.VMEM((s, BRANCH_W), f32), pltpu.VMEM((BRANCH_W, kw), f32)],
        compiler_params=_cparams("parallel"),
        name="hgrn",
    )(zd.reshape(nb, s, ZD_W), lbp, g)


def _mlstm_pass(z_ref, q_ref, k_ref, acc_ref, st_ref, nv_ref, m_ref, *, s, reverse, first, ioff, foff):
    c, w = MLSTM_CHUNK, BRANCH_W
    n = s // c
    tri = _tri(c, reverse)
    ones = jnp.ones((c, c), bf16)
    lane, row = _iota((c, w), 1), _iota((c, w), 0)
    head, sidx = lane // HEAD_W, lane % HEAD_W
    diag = sidx == row
    causal = (sidx >= row) if reverse else (sidx <= row)
    same = (_iota((w, w), 0) // HEAD_W) == (_iota((w, w), 1) // HEAD_W)
    bones = jnp.where(same, 1.0, 0.0).astype(bf16)
    st_ref[...] = jnp.zeros((w, w), f32)
    nv_ref[...] = jnp.zeros((1, w), f32)
    m_ref[...] = jnp.zeros((1, w), f32)

    def body(ci, carry):
        cidx = (n - 1 - ci) if reverse else ci
        st = pl.multiple_of(cidx * c, c)
        q = q_ref[pl.ds(st, c), :]
        k = k_ref[pl.ds(st, c), :]
        v = z_ref[pl.ds(st, c), 512:768]
        ig = z_ref[pl.ds(st, c), ioff:ioff + w]
        lf = jax.nn.log_sigmoid(z_ref[pl.ds(st, c), foff:foff + w])
        b = _lmul3(tri, lf)
        r = _lmul3(ones, jnp.where(diag, ig - b, 0.0))
        log_d = jnp.where(causal, b + r, NEG_BIG)
        m_prev = m_ref[...]
        log_prev = b + m_prev
        mrow = jnp.full((c, w), NEG_BIG, f32)
        for j in range(HEADS):
            mj = jnp.max(jnp.where(head == j, log_d, NEG_BIG), axis=-1, keepdims=True)
            mrow = jnp.where(head == j, mj, mrow)
        m_t = jnp.maximum(mrow, log_prev)
        w_prev = jnp.exp(log_prev - m_t)
        qb = q.astype(bf16)
        kexp = jnp.concatenate([jnp.where(head == j, k, 0.0).astype(bf16) for j in range(HEADS)], axis=0)
        vexp = jnp.concatenate([jnp.where(head == j, v, 0.0).astype(bf16) for j in range(HEADS)], axis=0)
        sc = _mm_nt(qb, kexp) * jnp.exp(log_d - m_t)
        state = st_ref[...]
        num = w_prev * _mm_nt(qb, state.astype(bf16)) + _mm(sc.astype(bf16), vexp)
        nv = nv_ref[...]
        den = w_prev * _rmul3(q * nv, bones) + _rmul3(sc, bones)
        hout = num / jnp.maximum(jnp.abs(den), jnp.exp(-m_t))
        last = b[0:1] if reverse else b[c - 1:c]
        log_w = last - b + ig
        m_new = jnp.maximum(last + m_prev, jnp.max(log_w, axis=0, keepdims=True))
        w_s = jnp.exp(log_w - m_new)
        dec = jnp.exp(last + m_prev - m_new)
        upd = _mm_tn((v * w_s).astype(bf16), k.astype(bf16))
        st_ref[...] = dec * state + jnp.where(same, upd, 0.0)
        nv_ref[...] = dec * nv + jnp.sum(w_s * k, axis=0, keepdims=True)
        m_ref[...] = m_new
        if first:
            acc_ref[pl.ds(st, c), :] = hout
        else:
            acc_ref[pl.ds(st, c), :] += hout
        return carry

    lax.fori_loop(0, n, body, 0)


def _mlstm_kernel(z_ref, cw_ref, cb_ref, g_ref, o_ref, q_ref, k_ref, acc_ref, st_ref, nv_ref, m_ref, *, s):
    rb = 256
    nblk = s // rb
    w = BRANCH_W

    def conv(st, i):
        x = z_ref[pl.ds(st, rb), 0:2 * w]
        pst = pl.multiple_of(jnp.maximum(st - 8, 0), 8)
        nst = pl.multiple_of(jnp.minimum(st + rb, s - 8), 8)
        prev = z_ref[pl.ds(pst, 8), 0:2 * w][7:8]
        nxt = z_ref[pl.ds(nst, 8), 0:2 * w][0:1]
        prev = jnp.where(i == 0, 0.0, prev)
        nxt = jnp.where(i == nblk - 1, 0.0, nxt)
        row = _iota((rb, 2 * w), 0)
        xm = jnp.where(row == 0, prev, pltpu.roll(x, 1, 0))
        xp = jnp.where(row == rb - 1, nxt, pltpu.roll(x, rb - 1, 0))
        y = cb_ref[...] + xm * cw_ref[0:1, :]
        y = y + x * cw_ref[1:2, :]
        y = y + xp * cw_ref[2:3, :]
        y = y * jax.nn.sigmoid(y)
        q_ref[pl.ds(st, rb), :] = y[:, 0:w]
        k_ref[pl.ds(st, rb), :] = y[:, w:2 * w] * (HEAD_W ** -0.5)
    _rows(s, rb, conv)

    for d in range(2):
        _mlstm_pass(z_ref, q_ref, k_ref, acc_ref, st_ref, nv_ref, m_ref, s=s, reverse=(d == 1), first=(d == 0),
                    ioff=1024 + d * w, foff=1536 + d * w)

    mavg = _block_avg(w, HEAD_W)

    def out(st, i):
        y = _head_rmsnorm(acc_ref[pl.ds(st, rb), :], g_ref[...], mavg)
        o_ref[pl.ds(st, rb), :] = jax.nn.sigmoid(z_ref[pl.ds(st, rb), 768:1024]) * y
    _rows(s, rb, out)


def _mlstm(zb, cw, cb, g, nb, s):
    w = BRANCH_W
    return pl.pallas_call(
        functools.partial(_mlstm_kernel, s=s),
        grid=(nb,),
        in_specs=[pl.BlockSpec((None, s, ZB_W), lambda b: (b, 0, 0)),
                  _const_spec((3, 2 * w)), _const_spec((1, 2 * w)), _const_spec((1, w))],
        out_specs=pl.BlockSpec((None, s, w), lambda b: (b, 0, 0)),
        out_shape=jax.ShapeDtypeStruct((nb, s, w), f32),
        scratch_shapes=[pltpu.VMEM((s, w), f32), pltpu.VMEM((s, w), f32), pltpu.VMEM((s, w), f32),
                        pltpu.VMEM((w, w), f32), pltpu.VMEM((1, w), f32), pltpu.VMEM((1, w), f32)],
        compiler_params=_cparams("parallel"),
        name="mlstm",
    )(zb.reshape(nb, s, ZB_W), cw, cb, g)


def _rope(x, cos, sa, sb):
    w = x.shape[-1]
    half = HEAD_W // 2
    return x * cos + pltpu.roll(x, w - half, 1) * sa + pltpu.roll(x, half, 1) * sb


def _dil_kernel(z_ref, cos_ref, sa_ref, sb_ref, o_ref, kr_ref, vm_ref, *, s, tq):
    w = BRANCH_W
    qi = pl.program_id(1)
    rb = 256

    @pl.when(qi == 0)
    def _():
        head = _iota((rb, w), 1) // HEAD_W

        def prep(st, i):
            x = z_ref[pl.ds(st, rb), w:2 * w]
            kr_ref[pl.ds(st, rb), :] = _rope(x, cos_ref[pl.ds(st, rb), :], sa_ref[pl.ds(st, rb), :],
                                             sb_ref[pl.ds(st, rb), :]).astype(bf16)
            v = z_ref[pl.ds(st, rb), 2 * w:3 * w]
            for j in range(HEADS):
                vm_ref[j, pl.ds(st, rb), :] = jnp.where(head == j, v, 0.0).astype(bf16)
        _rows(s, rb, prep)

    q0 = pl.multiple_of(qi * tq, tq)
    qr = _rope(z_ref[pl.ds(q0, tq), 0:w], cos_ref[pl.ds(q0, tq), :], sa_ref[pl.ds(q0, tq), :],
               sb_ref[pl.ds(q0, tq), :])
    rel = _iota((tq, s), 1) - (_iota((tq, s), 0) + q0)
    dist = jnp.abs(rel)
    cnt = jnp.zeros((tq, s), f32)
    for window, dil in DIL_PATTERNS:
        reach = (window // (2 * dil)) * dil
        ok = dist <= reach
        if dil > 1:
            ok = jnp.logical_and(ok, (rel & (dil - 1)) == 0)
        cnt = cnt + jnp.where(ok, 1.0, 0.0)
    valid = cnt > 0.0
    qhead = _iota((tq, w), 1) // HEAD_W
    acc = jnp.zeros((tq, w), f32)
    for j in range(HEADS):
        qh = jnp.where(qhead == j, qr, 0.0).astype(bf16)
        sc = _mm_nt(qh, kr_ref[...]) * (HEAD_W ** -0.5)
        sc = jnp.where(valid, sc, NEG_BIG)
        m = jnp.max(sc, axis=-1, keepdims=True)
        p = jnp.exp(sc - m) * cnt
        l = jnp.sum(p, axis=-1, keepdims=True)
        acc = acc + _mm(p.astype(bf16), vm_ref[j]) / l
    o_ref[...] = acc


def _dil(zc, cos, sa, sb, nb, s, tq=256):
    w = BRANCH_W
    for _, dil in DIL_PATTERNS:
        assert dil & (dil - 1) == 0 and s % dil == 0
    return pl.pallas_call(
        functools.partial(_dil_kernel, s=s, tq=tq),
        grid=(nb, s // tq),
        in_specs=[pl.BlockSpec((None, s, ZC_W), lambda b, i: (b, 0, 0)),
                  _const_spec((s, w)), _const_spec((s, w)), _const_spec((s, w))],
        out_specs=pl.BlockSpec((None, tq, w), lambda b, i: (b, i, 0)),
        out_shape=jax.ShapeDtypeStruct((nb, s, w), f32),
        scratch_shapes=[pltpu.VMEM((s, w), bf16), pltpu.VMEM((HEADS, s, w), bf16)],
        compiler_params=_cparams("parallel", "arbitrary"),
        name="dilattn",
    )(zc.reshape(nb, s, ZC_W), cos, sa, sb)


def _merge_kernel(h_ref, ya, yb, yc, yd, g_ref, wg_ref, wbr_ref, wo_ref, o_ref):
    x = h_ref[...]
    u = _rms(x, g_ref[...]).astype(bf16)
    merged = None
    for n, y in enumerate((ya, yb, yc, yd)):
        gate = jax.nn.sigmoid(_mm(u, wg_ref[n]))
        term = gate * _mm(y[...].astype(bf16), wbr_ref[n])
        merged = term if merged is None else merged + term
    o_ref[...] = x + _mm(merged.astype(bf16), wo_ref[...])


def _merge(h, ys, g, wg, wbr, wo, tm=512):
    t = h.shape[0]
    tok = lambda w: pl.BlockSpec((tm, w), lambda i: (i, 0))
    return pl.pallas_call(
        _merge_kernel,
        grid=(t // tm,),
        in_specs=[tok(D_MODEL)] + [tok(BRANCH_W)] * 4
        + [_const_spec((1, D_MODEL)), _const_spec((4, D_MODEL, D_MODEL)),
           _const_spec((4, BRANCH_W, D_MODEL)), _const_spec((D_MODEL, D_MODEL))],
        out_specs=tok(D_MODEL),
        out_shape=jax.ShapeDtypeStruct((t, D_MODEL), f32),
        compiler_params=_cparams("parallel"),
        name="merge",
    )(h, *ys, g, wg, wbr, wo)


def _stair_cells():
    return [(a, b) for a in range(PEER_TOPK) for b in range(PEER_TOPK) if (a + 1) * (b + 1) <= PEER_TOPK]


_CELLS = _stair_cells()
_NCELL = len(_CELLS)
_NCELL_PAD = -(-_NCELL // 8) * 8


def _extract_top(work, rowi, nrows, on_hit):
    tops = []
    for r in range(PEER_TOPK):
        mx = jnp.max(work, axis=0, keepdims=True)
        firstrow = jnp.min(jnp.where(work == mx, rowi, nrows), axis=0, keepdims=True)
        hit = rowi == firstrow
        work = jnp.where(hit, -jnp.inf, work)
        on_hit(r, hit)
        tops.append(mx)
    return tops


def _route_kernel(h_ref, g_ref, wpq_ref, keys_ref, sa_ref, u_ref, a_ref, cnt_ref, rk_ref, bn_ref,
                  q_ref, m_ref, *, tm):
    u = _rms(h_ref[...], g_ref[...]).astype(bf16)
    u_ref[...] = u
    q_ref[...] = _mm(u, wpq_ref[...]).astype(bf16)
    nk = PEER_NKEYS
    rowi = _iota((nk, tm), 0)
    rowc = _iota((_NCELL_PAD, tm), 0)

    def per_head(hd, carry):
        scs, tops, ranks = [], [], []
        for p in range(2):
            idx = hd * 2 + p
            qs = q_ref[:, pl.ds(pl.multiple_of(idx * nk, nk), nk)]
            sc = _mm_nt(keys_ref[idx], qs)
            state = {"rank": jnp.full((nk, tm), float(PEER_TOPK), f32)}

            def on_hit(r, hit, state=state):
                state["rank"] = jnp.where(hit, float(r), state["rank"])
            tops.append(_extract_top(sc, rowi, nk, on_hit))
            scs.append(sc)
            ranks.append(state["rank"])
        m_ref[...] = jnp.full((_NCELL_PAD, tm), -jnp.inf, f32)
        for ci, (a, b) in enumerate(_CELLS):
            m_ref[ci:ci + 1, :] = tops[0][a] + tops[1][b]
        cand = m_ref[...]
        state2 = {"sel": jnp.zeros((_NCELL_PAD, tm), f32)}

        def on_hit2(r, hit, state2=state2):
            state2["sel"] = jnp.where(hit, 1.0, state2["sel"])
        _extract_top(cand, rowc, _NCELL_PAD, on_hit2)
        sel = state2["sel"]
        wgt = jnp.where(sel > 0.0, jnp.exp(cand - cand[0:1]), 0.0)
        zsum = jnp.sum(wgt, axis=0, keepdims=True)
        rowcnt = _mm(sa_ref[...], sel.astype(bf16))
        cnt = jnp.zeros((nk, tm), f32)
        for a in range(PEER_TOPK):
            cnt = jnp.where(ranks[0] == float(a), rowcnt[a:a + 1], cnt)
        a_ref[hd] = jnp.exp(scs[0] - tops[0][0])
        cnt_ref[hd] = cnt
        rk_ref[hd] = ranks[1]
        bn_ref[hd] = jnp.exp(scs[1] - tops[1][0]) / zsum
        return carry

    lax.fori_loop(0, PEER_HEADS, per_head, 0)


def _route(h, g, wpq, keys, sa, tm=256):
    t = h.shape[0]
    nh, nk = PEER_HEADS, PEER_NKEYS
    tab = pl.BlockSpec((nh, nk, tm), lambda i: (0, 0, i))
    return pl.pallas_call(
        functools.partial(_route_kernel, tm=tm),
        grid=(t // tm,),
        in_specs=[pl.BlockSpec((tm, D_MODEL), lambda i: (i, 0)), _const_spec((1, D_MODEL)),
                  _const_spec((D_MODEL, 2 * nh * nk)), _const_spec((2 * nh, nk, nk)),
                  _const_spec((PEER_TOPK, _NCELL_PAD))],
        out_specs=[pl.BlockSpec((tm, D_MODEL), lambda i: (i, 0)), tab, tab, tab, tab],
        out_shape=[jax.ShapeDtypeStruct((t, D_MODEL), bf16)] + [jax.ShapeDtypeStruct((nh, nk, t), f32)] * 4,
        scratch_shapes=[pltpu.VMEM((tm, 2 * nh * nk), bf16), pltpu.VMEM((_NCELL_PAD, tm), f32)],
        compiler_params=_cparams("parallel"),
        name="route",
    )(h, g, wpq, keys, sa)


def _peer_kernel(u_ref, a_ref, cnt_ref, rk_ref, bn_ref, pu_ref, pvt_ref, o_ref, acc_ref, p_ref, *, eb):
    e = pl.program_id(1)
    nk = PEER_NKEYS

    @pl.when(e == 0)
    def _():
        acc_ref[...] = jnp.zeros(acc_ref.shape, f32)

    ht = _mm_nt(pu_ref[...], u_ref[...])
    for ii in range(eb // nk):
        i = e * (eb // nk) + ii
        x = ht[ii * nk:(ii + 1) * nk]
        act = 0.5 * x * (1.0 + lax.erf(x * math.sqrt(0.5)))
        gate = None
        for hd in range(PEER_HEADS):
            a = a_ref[hd, pl.ds(i, 1), :]
            c = cnt_ref[hd, pl.ds(i, 1), :]
            term = a * jnp.where(rk_ref[hd] < c, bn_ref[hd], 0.0)
            gate = term if gate is None else gate + term
        p_ref[ii * nk:(ii + 1) * nk, :] = (gate * act).astype(bf16)
    acc_ref[...] += _mm(pvt_ref[...], p_ref[...])

    @pl.when(e == pl.num_programs(1) - 1)
    def _():
        o_ref[...] = acc_ref[...].T


def _peer(u2, tabs, pu, pvt, tp=512, eb=1024):
    t = u2.shape[0]
    nh, nk = PEER_HEADS, PEER_NKEYS
    ne = pu.shape[0]
    tab = pl.BlockSpec((nh, nk, tp), lambda i, e: (0, 0, i))
    return pl.pallas_call(
        functools.partial(_peer_kernel, eb=eb),
        grid=(t // tp, ne // eb),
        in_specs=[pl.BlockSpec((tp, D_MODEL), lambda i, e: (i, 0)), tab, tab, tab, tab,
                  pl.BlockSpec((eb, D_MODEL), lambda i, e: (e, 0)),
                  pl.BlockSpec((D_MODEL, eb), lambda i, e: (0, e))],
        out_specs=pl.BlockSpec((tp, D_MODEL), lambda i, e: (i, 0)),
        out_shape=jax.ShapeDtypeStruct((t, D_MODEL), f32),
        scratch_shapes=[pltpu.VMEM((D_MODEL, tp), f32), pltpu.VMEM((eb, tp), bf16)],
        compiler_params=_cparams("parallel", "arbitrary"),
        name="peer",
    )(u2, *tabs, pu, pvt)


def _pe_kernel(h_ref, po_ref, p_ref, g_ref, wpg_ref, wpe_ref, gf_ref, o_ref, *, final):
    h = h_ref[...] + po_ref[...]
    gate = jax.nn.sigmoid(_mm(_rms(h, g_ref[...]).astype(bf16), wpg_ref[...]))
    h = h + gate * _mm(p_ref[...].astype(bf16), wpe_ref[...])
    if final:
        h = _rms(h, gf_ref[...])
    o_ref[...] = h


def _pe(h, po, p, g, wpg, wpe, gf, final, tm=512):
    t = h.shape[0]
    tok = lambda w: pl.BlockSpec((tm, w), lambda i: (i, 0))
    return pl.pallas_call(
        functools.partial(_pe_kernel, final=final),
        grid=(t // tm,),
        in_specs=[tok(D_MODEL), tok(D_MODEL), tok(PE_DIM), _const_spec((1, D_MODEL)),
                  _const_spec((D_MODEL, D_MODEL)), _const_spec((PE_DIM, D_MODEL)), _const_spec((1, D_MODEL))],
        out_specs=tok(D_MODEL),
        out_shape=jax.ShapeDtypeStruct((t, D_MODEL), f32),
        compiler_params=_cparams("parallel"),
        name="pe_embed",
    )(h, po, p, g, wpg, wpe, gf)


def _cols(w, name):
    a, b = _PROJ_OFF[name]
    return w[..., a:b]


def _proj_groups(w):
    pad = jnp.zeros(w.shape[:-1] + (ZA_W - 800,), w.dtype)
    ga = jnp.concatenate([_cols(w, n) for n in ('a_q', 'a_k', 'a_v', 'a_g', 'a_lr')] + [pad], axis=-1)
    gates = [jnp.repeat(_cols(w, n), HEAD_W, axis=-1) for n in ('b_i', 'b_f')]
    gb = jnp.concatenate([_cols(w, n) for n in ('b_q', 'b_k', 'b_v', 'b_o')] + gates, axis=-1)
    gc = jnp.concatenate([_cols(w, n) for n in ('c_q', 'c_k', 'c_v')], axis=-1)
    gd = jnp.concatenate([_cols(w, n) for n in ('d_q', 'd_f', 'd_i', 'd_g')], axis=-1)
    return ga, gb, gc, gd


def _rope_tables(s):
    half = HEAD_W // 2
    inv = ROPE_THETA ** (-jnp.arange(half, dtype=f32) / half)
    ang = jnp.arange(s, dtype=f32)[:, None] * inv[None, :]
    cos, sin = jnp.cos(ang), jnp.sin(ang)
    zero = jnp.zeros_like(sin)
    tile = lambda a, b: jnp.tile(jnp.concatenate([a, b], axis=-1), (1, HEADS))
    return tile(cos, cos), tile(-sin, zero), tile(zero, sin)


def _row_count_matrix():
    m = np.zeros((PEER_TOPK, _NCELL_PAD), np.float32)
    for ci, (a, _) in enumerate(_CELLS):
        m[a, ci] = 1.0
    return jnp.asarray(m, bf16)


def _prepare(g_mix, w_in, b_in, w_lr2, b_lr2, conv_w, conv_b, lb_param, g_a, g_b, g_d, w_gate, w_br, w_out,
             g_ffn, w_pq, sub_keys, peer_u, peer_v, g_pe, w_pe, w_pg, g_final):
    layers = []
    row = lambda a: a.reshape(1, -1)
    for l in range(DEPTH):
        wlr = jnp.zeros((LANES, 256), f32)
        for n in range(2):
            wlr = wlr.at[n * GLA_RANK:(n + 1) * GLA_RANK, n * 128:(n + 1) * 128].set(w_lr2[l, n])
        layers.append(dict(
            g_mix=row(g_mix[l]),
            w_proj=[w.astype(bf16) for w in _proj_groups(w_in[l])],
            b_proj=list(_proj_groups(row(b_in[l]))),
            wlr=wlr.astype(bf16), blr=row(b_lr2[l]),
            conv_w=conv_w[l], conv_b=row(conv_b[l]),
            g_a=row(g_a[l]), g_b=row(g_b[l]), g_d=row(g_d[l]),
            w_gate=w_gate[l].astype(bf16), w_br=w_br[l].astype(bf16), w_out=w_out[l].astype(bf16),
            g_ffn=row(g_ffn[l]), w_pq=w_pq[l].astype(bf16),
            keys=sub_keys[l].reshape(2 * PEER_HEADS, PEER_NKEYS, -1).astype(bf16),
            peer_u=peer_u[l].astype(bf16), peer_vt=peer_v[l].astype(bf16).T,
            g_pe=row(g_pe[l]), w_pe=w_pe[l].astype(bf16), w_pg=w_pg[l].astype(bf16),
        ))
    shared = dict(
        lbp=jnp.transpose(lb_param, (1, 0, 2)).reshape(DEPTH, -1),
        g_final=row(g_final), sa=_row_count_matrix(),
    )
    return layers, shared


def _trunk(x, p, layers, shared):
    nb, s, d = x.shape
    t = nb * s
    h = x.reshape(t, d)
    cos, sa, sb = _rope_tables(s)
    for l, w in enumerate(layers):
        za, zb, zc, zd = _proj(h, w['g_mix'], w['w_proj'], w['b_proj'])
        ya = _gla(za, w['wlr'], w['blr'], w['g_a'], nb, s)
        yb = _mlstm(zb, w['conv_w'], w['conv_b'], w['g_b'], nb, s)
        yc = _dil(zc, cos, sa, sb, nb, s)
        yd = _hgrn(zd, shared['lbp'], w['g_d'], nb, s, l)
        ys = [y.reshape(t, BRANCH_W) for y in (ya, yb, yc, yd)]
        h1 = _merge(h, ys, w['g_mix'], w['w_gate'], w['w_br'], w['w_out'])
        u2, *tabs = _route(h1, w['g_ffn'], w['w_pq'], w['keys'], shared['sa'])
        po = _peer(u2, tabs, w['peer_u'], w['peer_vt'])
        h = _pe(h1, po, p[l].reshape(t, PE_DIM), w['g_pe'], w['w_pg'], w['w_pe'], shared['g_final'],
                final=(l == DEPTH - 1))
    return h.reshape(nb, s, d)


def kernel(x_prompt, x_sample, p_prompt, p_sample, g_mix, w_in, b_in, w_lr2, b_lr2, conv_w, conv_b, lb_param, g_a, g_b, g_d, w_gate, w_br, w_out, g_ffn, w_pq, sub_keys, peer_u, peer_v, g_pe, w_pe, w_pg, g_final):
    layers, shared = _prepare(g_mix, w_in, b_in, w_lr2, b_lr2, conv_w, conv_b, lb_param, g_a, g_b, g_d, w_gate,
                              w_br, w_out, g_ffn, w_pq, sub_keys, peer_u, peer_v, g_pe, w_pe, w_pg, g_final)
    return (_trunk(x_prompt, p_prompt, layers, shared), _trunk(x_sample, p_sample, layers, shared))
```

```python
import functools
import math

import numpy as np
import jax
import jax.numpy as jnp
from jax import lax
from jax.experimental import pallas as pl
from jax.experimental.pallas import tpu as pltpu

f32 = jnp.float32
bf16 = jnp.bfloat16
i32 = jnp.int32

D_MODEL = 1024
DEPTH = 2
PE_DIM = 256
EPS = 1e-6
NEG_BIG = -1e30
HEADS = 4
HEAD_W = 64
BRANCH_W = HEADS * HEAD_W
GLA_DK = 32
GLA_RANK = 16
GLA_NORMALIZER = 16.0
GLA_CHUNK = 32
MLSTM_CHUNK = 64
HGRN_DF = 64
HGRN_CHUNK = 32
DIL_PATTERNS = ((128, 1), (512, 4), (2048, 16))
ROPE_THETA = 10000.0
PEER_HEADS = 8
PEER_NKEYS = 128
PEER_TOPK = 16
LANES = 128
PEER_SUB = 1024
EXP_CLAMP = 80.0

_PROJ_LAYOUT = (
    ('a_q', 128), ('a_k', 128), ('a_v', 256), ('a_g', 256), ('a_lr', 32),
    ('b_q', 256), ('b_k', 256), ('b_v', 256), ('b_i', 8), ('b_f', 8), ('b_o', 256),
    ('c_q', 256), ('c_k', 256), ('c_v', 256),
    ('d_q', 256), ('d_f', 512), ('d_i', 256), ('d_g', 256),
)
_PROJ_OFF = {}
_o = 0
for _n, _s in _PROJ_LAYOUT:
    _PROJ_OFF[_n] = (_o, _o + _s)
    _o += _s

ZA_W = 896
ZB_W = 2048
ZC_W = 768
ZD_W = 1280

VMEM_LIMIT = 56 * 1024 * 1024


def _cparams(*sem):
    return pltpu.CompilerParams(dimension_semantics=sem, vmem_limit_bytes=VMEM_LIMIT)


def _const_spec(shape):
    nd = len(shape)
    return pl.BlockSpec(shape, lambda *_: (0,) * nd, pipeline_mode=pl.Buffered(1))


def _mm(a, b):
    return jnp.dot(a, b, preferred_element_type=f32)


def _mm_nt(a, b):
    return lax.dot_general(a, b, (((1,), (1,)), ((), ())), preferred_element_type=f32)


def _mm_tn(a, b):
    return lax.dot_general(a, b, (((0,), (0,)), ((), ())), preferred_element_type=f32)


def _split3(x):
    hi = x.astype(bf16)
    r1 = x - hi.astype(f32)
    mid = r1.astype(bf16)
    lo = (r1 - mid.astype(f32)).astype(bf16)
    return hi, mid, lo


def _lmul3(m, x):
    hi, mid, lo = _split3(x)
    return _mm(m, hi) + _mm(m, mid) + _mm(m, lo)


def _rmul3(x, m):
    hi, mid, lo = _split3(x)
    return _mm(hi, m) + _mm(mid, m) + _mm(lo, m)


def _rms(x, g):
    return x * lax.rsqrt(jnp.mean(x * x, axis=-1, keepdims=True) + EPS) * g


def _iota(shape, dim):
    return lax.broadcasted_iota(i32, shape, dim)


def _tri(c, reverse):
    r, k = _iota((c, c), 0), _iota((c, c), 1)
    return jnp.where((k >= r) if reverse else (k <= r), 1.0, 0.0).astype(bf16)


def _block_avg(w, blk):
    same = (_iota((w, w), 0) // blk) == (_iota((w, w), 1) // blk)
    return jnp.where(same, 1.0 / blk, 0.0).astype(bf16)


def _rows(n, blk, fn):
    def body(i, c):
        fn(pl.multiple_of(i * blk, blk), i)
        return c
    lax.fori_loop(0, n // blk, body, 0)


def _head_rmsnorm(o, g, mavg):
    ms = _rmul3(o * o, mavg)
    return o * lax.rsqrt(ms + EPS) * g


def _proj_kernel(h_ref, g_ref, wa, wb, wc, wd, ba, bb, bc, bd, za, zb, zc, zd):
    u = _rms(h_ref[...], g_ref[...]).astype(bf16)
    for w, b, z in ((wa, ba, za), (wb, bb, zb), (wc, bc, zc), (wd, bd, zd)):
        z[...] = _mm(u, w[...]) + b[...]


def _proj(h, g, ws, bs, tm=512):
    t = h.shape[0]
    widths = (ZA_W, ZB_W, ZC_W, ZD_W)
    tok = lambda w: pl.BlockSpec((tm, w), lambda i: (i, 0))
    return pl.pallas_call(
        _proj_kernel,
        grid=(t // tm,),
        in_specs=[tok(D_MODEL), _const_spec((1, D_MODEL))]
        + [_const_spec((D_MODEL, w)) for w in widths] + [_const_spec((1, w)) for w in widths],
        out_specs=[tok(w) for w in widths],
        out_shape=[jax.ShapeDtypeStruct((t, w), f32) for w in widths],
        compiler_params=_cparams("parallel"),
        name="proj",
    )(h, g, *ws, *bs)


def _gl_pass(load, acc_ref, st_ref, *, s, c, dk, reverse, first):
    h, dv = HEADS, HEAD_W
    kw, vw, hc = h * dk, h * dv, h * c
    n = s // c
    tri = _tri(c, reverse)
    khead = _iota((c, kw), 1) // dk
    vhead = _iota((c, vw), 1) // dv
    sidx, trow = _iota((c, hc), 1) % c, _iota((c, hc), 0)
    causal = (sidx >= trow) if reverse else (sidx <= trow)
    bd = (_iota((vw, kw), 0) // dv) == (_iota((vw, kw), 1) // dk)
    st_ref[...] = jnp.zeros((vw, kw), f32)

    def body(ci, carry):
        cidx = (n - 1 - ci) if reverse else ci
        st = pl.multiple_of(cidx * c, c)
        q, k, v, g = load(st)
        b = _lmul3(tri, g)
        last = b[0:1] if reverse else b[c - 1:c]
        mid = b[c // 2:c // 2 + 1]
        qt = (q * jnp.exp(jnp.minimum(b - mid, EXP_CLAMP))).astype(bf16)
        kt = k * jnp.exp(jnp.minimum(mid - b, EXP_CLAMP))
        kexp = jnp.concatenate([jnp.where(khead == j, kt, 0.0).astype(bf16) for j in range(h)], axis=0)
        vexp = jnp.concatenate([jnp.where(vhead == j, v, 0.0).astype(bf16) for j in range(h)], axis=0)
        att = jnp.where(causal, _mm_nt(qt, kexp), 0.0).astype(bf16)
        o = _mm(att, vexp)
        state = st_ref[...]
        o = o + _mm_nt((q * jnp.exp(b)).astype(bf16), state.astype(bf16))
        kd = (k * jnp.exp(last - b)).astype(bf16)
        upd = _mm_tn(v.astype(bf16), kd)
        st_ref[...] = state * jnp.exp(last) + jnp.where(bd, upd, 0.0)
        if first:
            acc_ref[pl.ds(st, c), :] = o
        else:
            acc_ref[pl.ds(st, c), :] += o
        return carry

    lax.fori_loop(0, n, body, 0)


def _gla_kernel(z_ref, wlr_ref, blr_ref, g_ref, o_ref, gk_ref, acc_ref, st_ref, *, s):
    rb = 256

    def gates(st, i):
        lr = z_ref[pl.ds(st, rb), 768:896].astype(bf16)
        x = _mm(lr, wlr_ref[...]) + blr_ref[...]
        gk_ref[pl.ds(st, rb), :] = jax.nn.log_sigmoid(x) * (1.0 / GLA_NORMALIZER)
    _rows(s, rb, gates)

    scale = GLA_DK ** -0.5
    for d in range(2):
        def load(st, d=d):
            q = z_ref[pl.ds(st, GLA_CHUNK), 0:128] * scale
            k = z_ref[pl.ds(st, GLA_CHUNK), 128:256]
            v = z_ref[pl.ds(st, GLA_CHUNK), 256:512]
            g = gk_ref[pl.ds(st, GLA_CHUNK), d * 128:(d + 1) * 128]
            return q, k, v, g
        _gl_pass(load, acc_ref, st_ref, s=s, c=GLA_CHUNK, dk=GLA_DK, reverse=(d == 1), first=(d == 0))

    mavg = _block_avg(BRANCH_W, HEAD_W)

    def out(st, i):
        y = _head_rmsnorm(acc_ref[pl.ds(st, rb), :], g_ref[...], mavg)
        gate = z_ref[pl.ds(st, rb), 512:768]
        o_ref[pl.ds(st, rb), :] = y * (gate * jax.nn.sigmoid(gate))
    _rows(s, rb, out)


def _gla(za, wlr, blr, g, nb, s):
    return pl.pallas_call(
        functools.partial(_gla_kernel, s=s),
        grid=(nb,),
        in_specs=[pl.BlockSpec((None, s, ZA_W), lambda b: (b, 0, 0)),
                  _const_spec((LANES, 256)), _const_spec((1, 256)), _const_spec((1, BRANCH_W))],
        out_specs=pl.BlockSpec((None, s, BRANCH_W), lambda b: (b, 0, 0)),
        out_shape=jax.ShapeDtypeStruct((nb, s, BRANCH_W), f32),
        scratch_shapes=[pltpu.VMEM((s, 256), f32), pltpu.VMEM((s, BRANCH_W), f32),
                        pltpu.VMEM((BRANCH_W, HEADS * GLA_DK), f32)],
        compiler_params=_cparams("parallel"),
        name="gla",
    )(za.reshape(nb, s, ZA_W), wlr, blr, g)


def _hgrn_kernel(z_ref, lbp_ref, g_ref, o_ref, q_ref, lf_ref, kd_ref, acc_ref, st_ref, *, s, layer):
    rb = 256
    x = lbp_ref[...]
    e = jnp.exp(x - jnp.max(x, axis=0, keepdims=True))
    p = e / jnp.sum(e, axis=0, keepdims=True)
    lb = p[0:1]
    for j in range(1, layer + 1):
        lb = lb + p[j:j + 1]
    lb = lb - p[0:1]

    def prep(st, i):
        qz = z_ref[pl.ds(st, rb), 0:256]
        q_ref[pl.ds(st, rb), :] = qz * jax.nn.sigmoid(qz)
        fz = z_ref[pl.ds(st, rb), 256:768]
        lf_ref[pl.ds(st, rb), :] = jnp.log(lb + (1.0 - lb) * jax.nn.sigmoid(fz))
        kd_ref[pl.ds(st, rb), :] = (1.0 - lb) * jax.nn.sigmoid(-fz)
    _rows(s, rb, prep)

    kw = HEADS * HGRN_DF
    for d in range(2):
        def load(st, d=d):
            q = q_ref[pl.ds(st, HGRN_CHUNK), :]
            k = kd_ref[pl.ds(st, HGRN_CHUNK), d * kw:(d + 1) * kw]
            v = z_ref[pl.ds(st, HGRN_CHUNK), 768:1024]
            g = lf_ref[pl.ds(st, HGRN_CHUNK), d * kw:(d + 1) * kw]
            return q, k, v, g
        _gl_pass(load, acc_ref, st_ref, s=s, c=HGRN_CHUNK, dk=HGRN_DF, reverse=(d == 1), first=(d == 0))

    mavg = _block_avg(BRANCH_W, HEAD_W)

    def out(st, i):
        y = _head_rmsnorm(acc_ref[pl.ds(st, rb), :], g_ref[...], mavg)
        gate = z_ref[pl.ds(st, rb), 1024:1280]
        o_ref[pl.ds(st, rb), :] = y * (gate * jax.nn.sigmoid(gate))
    _rows(s, rb, out)


def _hgrn(zd, lbp, g, nb, s, layer):
    kw = HEADS * HGRN_DF
    return pl.pallas_call(
        functools.partial(_hgrn_kernel, s=s, layer=layer),
        grid=(nb,),
        in_specs=[pl.BlockSpec((None, s, ZD_W), lambda b: (b, 0, 0)),
                  _const_spec((DEPTH, 2 * kw)), _const_spec((1, BRANCH_W))],
        out_specs=pl.BlockSpec((None, s, BRANCH_W), lambda b: (b, 0, 0)),
        out_shape=jax.ShapeDtypeStruct((nb, s, BRANCH_W), f32),
        scratch_shapes=[pltpu.VMEM((s, kw), f32), pltpu.VMEM((s, 2 * kw), f32), pltpu.VMEM((s, 2 * kw), f32),
                        pltpu.VMEM((s, BRANCH_W), f32), pltpu.VMEM((BRANCH_W, kw), f32)],
        compiler_params=_cparams("parallel"),
        name="hgrn",
    )(zd.reshape(nb, s, ZD_W), lbp, g)


def _mlstm_pass(z_ref, q_ref, k_ref, acc_ref, st_ref, nv_ref, m_ref, *, s, reverse, first, ioff, foff):
    c, w = MLSTM_CHUNK, BRANCH_W
    n = s // c
    tri = _tri(c, reverse)
    ones = jnp.ones((c, c), bf16)
    lane, row = _iota((c, w), 1), _iota((c, w), 0)
    head, sidx = lane // HEAD_W, lane % HEAD_W
    diag = sidx == row
    causal = (sidx >= row) if reverse else (sidx <= row)
    same = (_iota((w, w), 0) // HEAD_W) == (_iota((w, w), 1) // HEAD_W)
    bones = jnp.where(same, 1.0, 0.0).astype(bf16)
    st_ref[...] = jnp.zeros((w, w), f32)
    nv_ref[...] = jnp.zeros((1, w), f32)
    m_ref[...] = jnp.zeros((1, w), f32)

    def body(ci, carry):
        cidx = (n - 1 - ci) if reverse else ci
        st = pl.multiple_of(cidx * c, c)
        q = q_ref[pl.ds(st, c), :]
        k = k_ref[pl.ds(st, c), :]
        v = z_ref[pl.ds(st, c), 512:768]
        ig = z_ref[pl.ds(st, c), ioff:ioff + w]
        lf = jax.nn.log_sigmoid(z_ref[pl.ds(st, c), foff:foff + w])
        b = _lmul3(tri, lf)
        r = _lmul3(ones, jnp.where(diag, ig - b, 0.0))
        log_d = jnp.where(causal, b + r, NEG_BIG)
        m_prev = m_ref[...]
        log_prev = b + m_prev
        mrow = jnp.full((c, w), NEG_BIG, f32)
        for j in range(HEADS):
            mj = jnp.max(jnp.where(head == j, log_d, NEG_BIG), axis=-1, keepdims=True)
            mrow = jnp.where(head == j, mj, mrow)
        m_t = jnp.maximum(mrow, log_prev)
        w_prev = jnp.exp(log_prev - m_t)
        qb = q.astype(bf16)
        kexp = jnp.concatenate([jnp.where(head == j, k, 0.0).astype(bf16) for j in range(HEADS)], axis=0)
        vexp = jnp.concatenate([jnp.where(head == j, v, 0.0).astype(bf16) for j in range(HEADS)], axis=0)
        sc = _mm_nt(qb, kexp) * jnp.exp(log_d - m_t)
        state = st_ref[...]
        num = w_prev * _mm_nt(qb, state.astype(bf16)) + _mm(sc.astype(bf16), vexp)
        nv = nv_ref[...]
        den = w_prev * _rmul3(q * nv, bones) + _rmul3(sc, bones)
        hout = num / jnp.maximum(jnp.abs(den), jnp.exp(-m_t))
        last = b[0:1] if reverse else b[c - 1:c]
        log_w = last - b + ig
        m_new = jnp.maximum(last + m_prev, jnp.max(log_w, axis=0, keepdims=True))
        w_s = jnp.exp(log_w - m_new)
        dec = jnp.exp(last + m_prev - m_new)
        upd = _mm_tn((v * w_s).astype(bf16), k.astype(bf16))
        st_ref[...] = dec * state + jnp.where(same, upd, 0.0)
        nv_ref[...] = dec * nv + jnp.sum(w_s * k, axis=0, keepdims=True)
        m_ref[...] = m_new
        if first:
            acc_ref[pl.ds(st, c), :] = hout
        else:
            acc_ref[pl.ds(st, c), :] += hout
        return carry

    lax.fori_loop(0, n, body, 0)


def _mlstm_kernel(z_ref, cw_ref, cb_ref, g_ref, o_ref, q_ref, k_ref, acc_ref, st_ref, nv_ref, m_ref, *, s):
    rb = 256
    nblk = s // rb
    w = BRANCH_W

    def conv(st, i):
        x = z_ref[pl.ds(st, rb), 0:2 * w]
        pst = pl.multiple_of(jnp.maximum(st - 8, 0), 8)
        nst = pl.multiple_of(jnp.minimum(st + rb, s - 8), 8)
        prev = z_ref[pl.ds(pst, 8), 0:2 * w][7:8]
        nxt = z_ref[pl.ds(nst, 8), 0:2 * w][0:1]
        prev = jnp.where(i == 0, 0.0, prev)
        nxt = jnp.where(i == nblk - 1, 0.0, nxt)
        row = _iota((rb, 2 * w), 0)
        xm = jnp.where(row == 0, prev, pltpu.roll(x, 1, 0))
        xp = jnp.where(row == rb - 1, nxt, pltpu.roll(x, rb - 1, 0))
        y = cb_ref[...] + xm * cw_ref[0:1, :]
        y = y + x * cw_ref[1:2, :]
        y = y + xp * cw_ref[2:3, :]
        y = y * jax.nn.sigmoid(y)
        q_ref[pl.ds(st, rb), :] = y[:, 0:w]
        k_ref[pl.ds(st, rb), :] = y[:, w:2 * w] * (HEAD_W ** -0.5)
    _rows(s, rb, conv)

    for d in range(2):
        _mlstm_pass(z_ref, q_ref, k_ref, acc_ref, st_ref, nv_ref, m_ref, s=s, reverse=(d == 1), first=(d == 0),
                    ioff=1024 + d * w, foff=1536 + d * w)

    mavg = _block_avg(w, HEAD_W)

    def out(st, i):
        y = _head_rmsnorm(acc_ref[pl.ds(st, rb), :], g_ref[...], mavg)
        o_ref[pl.ds(st, rb), :] = jax.nn.sigmoid(z_ref[pl.ds(st, rb), 768:1024]) * y
    _rows(s, rb, out)


def _mlstm(zb, cw, cb, g, nb, s):
    w = BRANCH_W
    return pl.pallas_call(
        functools.partial(_mlstm_kernel, s=s),
        grid=(nb,),
        in_specs=[pl.BlockSpec((None, s, ZB_W), lambda b: (b, 0, 0)),
                  _const_spec((3, 2 * w)), _const_spec((1, 2 * w)), _const_spec((1, w))],
        out_specs=pl.BlockSpec((None, s, w), lambda b: (b, 0, 0)),
        out_shape=jax.ShapeDtypeStruct((nb, s, w), f32),
        scratch_shapes=[pltpu.VMEM((s, w), f32), pltpu.VMEM((s, w), f32), pltpu.VMEM((s, w), f32),
                        pltpu.VMEM((w, w), f32), pltpu.VMEM((1, w), f32), pltpu.VMEM((1, w), f32)],
        compiler_params=_cparams("parallel"),
        name="mlstm",
    )(zb.reshape(nb, s, ZB_W), cw, cb, g)


def _rope(x, cos, sa, sb):
    w = x.shape[-1]
    half = HEAD_W // 2
    return x * cos + pltpu.roll(x, w - half, 1) * sa + pltpu.roll(x, half, 1) * sb


def _dil_kernel(z_ref, cos_ref, sa_ref, sb_ref, o_ref, kr_ref, vm_ref, *, s, tq):
    w = BRANCH_W
    qi = pl.program_id(1)
    rb = 256

    @pl.when(qi == 0)
    def _():
        head = _iota((rb, w), 1) // HEAD_W

        def prep(st, i):
            x = z_ref[pl.ds(st, rb), w:2 * w]
            kr_ref[pl.ds(st, rb), :] = _rope(x, cos_ref[pl.ds(st, rb), :], sa_ref[pl.ds(st, rb), :],
                                             sb_ref[pl.ds(st, rb), :]).astype(bf16)
            v = z_ref[pl.ds(st, rb), 2 * w:3 * w]
            for j in range(HEADS):
                vm_ref[j, pl.ds(st, rb), :] = jnp.where(head == j, v, 0.0).astype(bf16)
        _rows(s, rb, prep)

    q0 = pl.multiple_of(qi * tq, tq)
    qr = _rope(z_ref[pl.ds(q0, tq), 0:w], cos_ref[pl.ds(q0, tq), :], sa_ref[pl.ds(q0, tq), :],
               sb_ref[pl.ds(q0, tq), :])
    rel = _iota((tq, s), 1) - (_iota((tq, s), 0) + q0)
    dist = jnp.abs(rel)
    cnt = jnp.zeros((tq, s), f32)
    for window, dil in DIL_PATTERNS:
        reach = (window // (2 * dil)) * dil
        ok = dist <= reach
        if dil > 1:
            ok = jnp.logical_and(ok, (rel & (dil - 1)) == 0)
        cnt = cnt + jnp.where(ok, 1.0, 0.0)
    valid = cnt > 0.0
    qhead = _iota((tq, w), 1) // HEAD_W
    acc = jnp.zeros((tq, w), f32)
    for j in range(HEADS):
        qh = jnp.where(qhead == j, qr, 0.0).astype(bf16)
        sc = _mm_nt(qh, kr_ref[...]) * (HEAD_W ** -0.5)
        sc = jnp.where(valid, sc, NEG_BIG)
        m = jnp.max(sc, axis=-1, keepdims=True)
        p = jnp.exp(sc - m) * cnt
        l = jnp.sum(p, axis=-1, keepdims=True)
        acc = acc + _mm(p.astype(bf16), vm_ref[j]) / l
    o_ref[...] = acc


def _dil(zc, cos, sa, sb, nb, s, tq=256):
    w = BRANCH_W
    for _, dil in DIL_PATTERNS:
        assert dil & (dil - 1) == 0 and s % dil == 0
    return pl.pallas_call(
        functools.partial(_dil_kernel, s=s, tq=tq),
        grid=(nb, s // tq),
        in_specs=[pl.BlockSpec((None, s, ZC_W), lambda b, i: (b, 0, 0)),
                  _const_spec((s, w)), _const_spec((s, w)), _const_spec((s, w))],
        out_specs=pl.BlockSpec((None, tq, w), lambda b, i: (b, i, 0)),
        out_shape=jax.ShapeDtypeStruct((nb, s, w), f32),
        scratch_shapes=[pltpu.VMEM((s, w), bf16), pltpu.VMEM((HEADS, s, w), bf16)],
        compiler_params=_cparams("parallel", "arbitrary"),
        name="dilattn",
    )(zc.reshape(nb, s, ZC_W), cos, sa, sb)


def _merge_kernel(h_ref, ya, yb, yc, yd, g_ref, wg_ref, wbr_ref, wo_ref, o_ref):
    x = h_ref[...]
    u = _rms(x, g_ref[...]).astype(bf16)
    merged = None
    for n, y in enumerate((ya, yb, yc, yd)):
        gate = jax.nn.sigmoid(_mm(u, wg_ref[n]))
        term = gate * _mm(y[...].astype(bf16), wbr_ref[n])
        merged = term if merged is None else merged + term
    o_ref[...] = x + _mm(merged.astype(bf16), wo_ref[...])


def _merge(h, ys, g, wg, wbr, wo, tm=512):
    t = h.shape[0]
    tok = lambda w: pl.BlockSpec((tm, w), lambda i: (i, 0))
    return pl.pallas_call(
        _merge_kernel,
        grid=(t // tm,),
        in_specs=[tok(D_MODEL)] + [tok(BRANCH_W)] * 4
        + [_const_spec((1, D_MODEL)), _const_spec((4, D_MODEL, D_MODEL)),
           _const_spec((4, BRANCH_W, D_MODEL)), _const_spec((D_MODEL, D_MODEL))],
        out_specs=tok(D_MODEL),
        out_shape=jax.ShapeDtypeStruct((t, D_MODEL), f32),
        compiler_params=_cparams("parallel"),
        name="merge",
    )(h, *ys, g, wg, wbr, wo)


def _stair_cells():
    return [(a, b) for a in range(PEER_TOPK) for b in range(PEER_TOPK) if (a + 1) * (b + 1) <= PEER_TOPK]


_CELLS = _stair_cells()
_NCELL = len(_CELLS)
_NCELL_PAD = -(-_NCELL // 8) * 8


def _extract_top(work, rowi, nrows, on_hit):
    tops = []
    for r in range(PEER_TOPK):
        mx = jnp.max(work, axis=0, keepdims=True)
        firstrow = jnp.min(jnp.where(work == mx, rowi, nrows), axis=0, keepdims=True)
        hit = rowi == firstrow
        work = jnp.where(hit, -jnp.inf, work)
        on_hit(r, hit)
        tops.append(mx)
    return tops


def _route_kernel(h_ref, g_ref, wpq_ref, keys_ref, sa_ref, u_ref, a_ref, cnt_ref, rk_ref, bn_ref,
                  q_ref, m_ref, *, tm):
    u = _rms(h_ref[...], g_ref[...]).astype(bf16)
    u_ref[...] = u
    q_ref[...] = _mm(u, wpq_ref[...]).astype(bf16)
    nk = PEER_NKEYS
    rowi = _iota((nk, tm), 0)
    rowc = _iota((_NCELL_PAD, tm), 0)

    def per_head(hd, carry):
        scs, tops, ranks = [], [], []
        for p in range(2):
            idx = hd * 2 + p
            qs = q_ref[:, pl.ds(pl.multiple_of(idx * nk, nk), nk)]
            sc = _mm_nt(keys_ref[idx], qs)
            state = {"rank": jnp.full((nk, tm), float(PEER_TOPK), f32)}

            def on_hit(r, hit, state=state):
                state["rank"] = jnp.where(hit, float(r), state["rank"])
            tops.append(_extract_top(sc, rowi, nk, on_hit))
            scs.append(sc)
            ranks.append(state["rank"])
        m_ref[...] = jnp.full((_NCELL_PAD, tm), -jnp.inf, f32)
        for ci, (a, b) in enumerate(_CELLS):
            m_ref[ci:ci + 1, :] = tops[0][a] + tops[1][b]
        cand = m_ref[...]
        state2 = {"sel": jnp.zeros((_NCELL_PAD, tm), f32)}

        def on_hit2(r, hit, state2=state2):
            state2["sel"] = jnp.where(hit, 1.0, state2["sel"])
        _extract_top(cand, rowc, _NCELL_PAD, on_hit2)
        sel = state2["sel"]
        wgt = jnp.where(sel > 0.0, jnp.exp(cand - cand[0:1]), 0.0)
        zsum = jnp.sum(wgt, axis=0, keepdims=True)
        rowcnt = _mm(sa_ref[...], sel.astype(bf16))
        cnt = jnp.zeros((nk, tm), f32)
        for a in range(PEER_TOPK):
            cnt = jnp.where(ranks[0] == float(a), rowcnt[a:a + 1], cnt)
        rows = pl.ds(pl.multiple_of(hd * nk, nk), nk)
        amat = jnp.exp(scs[0] - tops[0][0])
        for tb in range(tm // LANES):
            a_ref[tb, rows, :] = amat[:, tb * LANES:(tb + 1) * LANES]
            cnt_ref[tb, rows, :] = cnt[:, tb * LANES:(tb + 1) * LANES]
        prow = pl.ds(pl.multiple_of(hd * (nk // 2), nk // 2), nk // 2)
        rk_ref[prow, :] = pltpu.bitcast(ranks[1].astype(bf16), jnp.uint32)
        bn_ref[prow, :] = pltpu.bitcast((jnp.exp(scs[1] - tops[1][0]) / zsum).astype(bf16), jnp.uint32)
        return carry

    lax.fori_loop(0, PEER_HEADS, per_head, 0)


def _route(h, g, wpq, keys, sa, tm=256):
    t = h.shape[0]
    nh, nk = PEER_HEADS, PEER_NKEYS
    tab = pl.BlockSpec((tm // LANES, nh * nk, LANES), lambda i: (i, 0, 0))
    ptab = pl.BlockSpec((nh * nk // 2, tm), lambda i: (0, i))
    return pl.pallas_call(
        functools.partial(_route_kernel, tm=tm),
        grid=(t // tm,),
        in_specs=[pl.BlockSpec((tm, D_MODEL), lambda i: (i, 0)), _const_spec((1, D_MODEL)),
                  _const_spec((D_MODEL, 2 * nh * nk)), _const_spec((2 * nh, nk, nk)),
                  _const_spec((PEER_TOPK, _NCELL_PAD))],
        out_specs=[pl.BlockSpec((tm, D_MODEL), lambda i: (i, 0)), tab, tab, ptab, ptab],
        out_shape=[jax.ShapeDtypeStruct((t, D_MODEL), bf16)] + [jax.ShapeDtypeStruct((t // LANES, nh * nk, LANES), f32)] * 2
        + [jax.ShapeDtypeStruct((nh * nk // 2, t), jnp.uint32)] * 2,
        scratch_shapes=[pltpu.VMEM((tm, 2 * nh * nk), bf16), pltpu.VMEM((_NCELL_PAD, tm), f32)],
        compiler_params=_cparams("parallel"),
        name="route",
    )(h, g, wpq, keys, sa)


def _gelu_bf16(x):
    return (0.5 * x * (1.0 + lax.erf(x * math.sqrt(0.5)))).astype(bf16)


def _peer_gates(act_ref, p_ref, tabs, i0, tp):
    a_ref, cnt_ref, rk_ref, bn_ref = tabs
    nk = PEER_NKEYS
    for ii in range(PEER_SUB // nk):
        for tb in range(tp // LANES):
            lanes = slice(tb * LANES, (tb + 1) * LANES)
            gate = None
            for hd in range(PEER_HEADS):
                row = pl.ds(hd * nk + i0 + ii, 1)
                a = jnp.broadcast_to(a_ref[tb, row, :], (nk, LANES)).astype(bf16)
                c = jnp.broadcast_to(cnt_ref[tb, row, :], (nk, LANES)).astype(bf16)
                prow = slice(hd * (nk // 2), (hd + 1) * (nk // 2))
                rk = pltpu.bitcast(rk_ref[prow, lanes], bf16)
                bn = pltpu.bitcast(bn_ref[prow, lanes], bf16)
                term = a * jnp.where(rk < c, bn, jnp.zeros((), bf16))
                gate = term if gate is None else gate + term
            p_ref[ii * nk:(ii + 1) * nk, lanes] = gate * act_ref[ii * nk:(ii + 1) * nk, lanes]


def _peer_kernel(u_ref, a_ref, cnt_ref, rk_ref, bn_ref, pu_ref, pvt_ref, o_ref, acc_ref, act0, act1, p0, p1,
                 *, ne, tp):
    g = pl.program_id(0)
    nblocks = pl.num_programs(0) - 2
    b1 = jnp.clip(g - 1, 0, nblocks - 1)
    b2 = jnp.clip(g - 2, 0, nblocks - 1)
    i1 = (b1 % ne) * (PEER_SUB // PEER_NKEYS)
    tabs = (a_ref, cnt_ref, rk_ref, bn_ref)

    @pl.when(g == 0)
    def _():
        for ref in (act0, act1, p0, p1):
            ref[...] = jnp.zeros(ref.shape, bf16)

    @pl.when(b2 % ne == 0)
    def _():
        acc_ref[...] = jnp.zeros(acc_ref.shape, f32)

    def stage(act_w, act_r, p_w, p_r):
        act_w[...] = _gelu_bf16(_mm_nt(pu_ref[...], u_ref[...]))
        _peer_gates(act_r, p_w, tabs, i1, tp)
        acc_ref[...] += _mm(pvt_ref[...], p_r[...])

    @pl.when(g % 2 == 0)
    def _():
        stage(act0, act1, p1, p0)

    @pl.when(g % 2 == 1)
    def _():
        stage(act1, act0, p0, p1)

    @pl.when(jnp.logical_and(g >= 2, b2 % ne == ne - 1))
    def _():
        o_ref[...] = acc_ref[...].T


def _peer(u2, tabs, pu, pvt, tp=512):
    t = u2.shape[0]
    nh, nk = PEER_HEADS, PEER_NKEYS
    ne = pu.shape[0] // PEER_SUB
    nblocks = (t // tp) * ne
    blk = lambda lag: (lambda g: jnp.clip(g - lag, 0, nblocks - 1))
    b0, b1, b2 = blk(0), blk(1), blk(2)
    tab = pl.BlockSpec((tp // LANES, nh * nk, LANES), lambda g: (b1(g) // ne, 0, 0))
    ptab = pl.BlockSpec((nh * nk // 2, tp), lambda g: (0, b1(g) // ne))
    return pl.pallas_call(
        functools.partial(_peer_kernel, ne=ne, tp=tp),
        grid=(nblocks + 2,),
        in_specs=[pl.BlockSpec((tp, D_MODEL), lambda g: (b0(g) // ne, 0)), tab, tab, ptab, ptab,
                  pl.BlockSpec((PEER_SUB, D_MODEL), lambda g: (b0(g) % ne, 0)),
                  pl.BlockSpec((D_MODEL, PEER_SUB), lambda g: (0, b2(g) % ne))],
        out_specs=pl.BlockSpec((tp, D_MODEL), lambda g: (b2(g) // ne, 0)),
        out_shape=jax.ShapeDtypeStruct((t, D_MODEL), f32),
        scratch_shapes=[pltpu.VMEM((D_MODEL, tp), f32)] + [pltpu.VMEM((PEER_SUB, tp), bf16)] * 4,
        compiler_params=_cparams("arbitrary"),
        name="peer",
    )(u2, *tabs, pu, pvt)


def _pe_kernel(h_ref, po_ref, p_ref, g_ref, wpg_ref, wpe_ref, gf_ref, o_ref, *, final):
    h = h_ref[...] + po_ref[...]
    gate = jax.nn.sigmoid(_mm(_rms(h, g_ref[...]).astype(bf16), wpg_ref[...]))
    h = h + gate * _mm(p_ref[...].astype(bf16), wpe_ref[...])
    if final:
        h = _rms(h, gf_ref[...])
    o_ref[...] = h


def _pe(h, po, p, g, wpg, wpe, gf, final, tm=512):
    t = h.shape[0]
    tok = lambda w: pl.BlockSpec((tm, w), lambda i: (i, 0))
    return pl.pallas_call(
        functools.partial(_pe_kernel, final=final),
        grid=(t // tm,),
        in_specs=[tok(D_MODEL), tok(D_MODEL), tok(PE_DIM), _const_spec((1, D_MODEL)),
                  _const_spec((D_MODEL, D_MODEL)), _const_spec((PE_DIM, D_MODEL)), _const_spec((1, D_MODEL))],
        out_specs=tok(D_MODEL),
        out_shape=jax.ShapeDtypeStruct((t, D_MODEL), f32),
        compiler_params=_cparams("parallel"),
        name="pe_embed",
    )(h, po, p, g, wpg, wpe, gf)


def _cols(w, name):
    a, b = _PROJ_OFF[name]
    return w[..., a:b]


def _proj_groups(w):
    pad = jnp.zeros(w.shape[:-1] + (ZA_W - 800,), w.dtype)
    ga = jnp.concatenate([_cols(w, n) for n in ('a_q', 'a_k', 'a_v', 'a_g', 'a_lr')] + [pad], axis=-1)
    gates = [jnp.repeat(_cols(w, n), HEAD_W, axis=-1) for n in ('b_i', 'b_f')]
    gb = jnp.concatenate([_cols(w, n) for n in ('b_q', 'b_k', 'b_v', 'b_o')] + gates, axis=-1)
    gc = jnp.concatenate([_cols(w, n) for n in ('c_q', 'c_k', 'c_v')], axis=-1)
    gd = jnp.concatenate([_cols(w, n) for n in ('d_q', 'd_f', 'd_i', 'd_g')], axis=-1)
    return ga, gb, gc, gd


def _rope_tables(s):
    half = HEAD_W // 2
    inv = ROPE_THETA ** (-jnp.arange(half, dtype=f32) / half)
    ang = jnp.arange(s, dtype=f32)[:, None] * inv[None, :]
    cos, sin = jnp.cos(ang), jnp.sin(ang)
    zero = jnp.zeros_like(sin)
    tile = lambda a, b: jnp.tile(jnp.concatenate([a, b], axis=-1), (1, HEADS))
    return tile(cos, cos), tile(-sin, zero), tile(zero, sin)


def _row_count_matrix():
    m = np.zeros((PEER_TOPK, _NCELL_PAD), np.float32)
    for ci, (a, _) in enumerate(_CELLS):
        m[a, ci] = 1.0
    return jnp.asarray(m, bf16)


def _prepare(g_mix, w_in, b_in, w_lr2, b_lr2, conv_w, conv_b, lb_param, g_a, g_b, g_d, w_gate, w_br, w_out,
             g_ffn, w_pq, sub_keys, peer_u, peer_v, g_pe, w_pe, w_pg, g_final):
    layers = []
    row = lambda a: a.reshape(1, -1)
    for l in range(DEPTH):
        wlr = jnp.zeros((LANES, 256), f32)
        for n in range(2):
            wlr = wlr.at[n * GLA_RANK:(n + 1) * GLA_RANK, n * 128:(n + 1) * 128].set(w_lr2[l, n])
        layers.append(dict(
            g_mix=row(g_mix[l]),
            w_proj=[w.astype(bf16) for w in _proj_groups(w_in[l])],
            b_proj=list(_proj_groups(row(b_in[l]))),
            wlr=wlr.astype(bf16), blr=row(b_lr2[l]),
            conv_w=conv_w[l], conv_b=row(conv_b[l]),
            g_a=row(g_a[l]), g_b=row(g_b[l]), g_d=row(g_d[l]),
            w_gate=w_gate[l].astype(bf16), w_br=w_br[l].astype(bf16), w_out=w_out[l].astype(bf16),
            g_ffn=row(g_ffn[l]), w_pq=w_pq[l].astype(bf16),
            keys=sub_keys[l].reshape(2 * PEER_HEADS, PEER_NKEYS, -1).astype(bf16),
            peer_u=peer_u[l].astype(bf16), peer_vt=peer_v[l].astype(bf16).T,
            g_pe=row(g_pe[l]), w_pe=w_pe[l].astype(bf16), w_pg=w_pg[l].astype(bf16),
        ))
    shared = dict(
        lbp=jnp.transpose(lb_param, (1, 0, 2)).reshape(DEPTH, -1),
        g_final=row(g_final), sa=_row_count_matrix(),
    )
    return layers, shared


def _trunk(x, p, layers, shared):
    nb, s, d = x.shape
    t = nb * s
    h = x.reshape(t, d)
    cos, sa, sb = _rope_tables(s)
    for l, w in enumerate(layers):
        za, zb, zc, zd = _proj(h, w['g_mix'], w['w_proj'], w['b_proj'])
        ya = _gla(za, w['wlr'], w['blr'], w['g_a'], nb, s)
        yb = _mlstm(zb, w['conv_w'], w['conv_b'], w['g_b'], nb, s)
        yc = _dil(zc, cos, sa, sb, nb, s)
        yd = _hgrn(zd, shared['lbp'], w['g_d'], nb, s, l)
        ys = [y.reshape(t, BRANCH_W) for y in (ya, yb, yc, yd)]
        h1 = _merge(h, ys, w['g_mix'], w['w_gate'], w['w_br'], w['w_out'])
        u2, *tabs = _route(h1, w['g_ffn'], w['w_pq'], w['keys'], shared['sa'])
        po = _peer(u2, tabs, w['peer_u'], w['peer_vt'])
        h = _pe(h1, po, p[l].reshape(t, PE_DIM), w['g_pe'], w['w_pg'], w['w_pe'], shared['g_final'],
                final=(l == DEPTH - 1))
    return h.reshape(nb, s, d)


def kernel(x_prompt, x_sample, p_prompt, p_sample, g_mix, w_in, b_in, w_lr2, b_lr2, conv_w, conv_b, lb_param, g_a, g_b, g_d, w_gate, w_br, w_out, g_ffn, w_pq, sub_keys, peer_u, peer_v, g_pe, w_pe, w_pg, g_final):
    layers, shared = _prepare(g_mix, w_in, b_in, w_lr2, b_lr2, conv_w, conv_b, lb_param, g_a, g_b, g_d, w_gate,
                              w_br, w_out, g_ffn, w_pq, sub_keys, peer_u, peer_v, g_pe, w_pe, w_pg, g_final)
    return (_trunk(x_prompt, p_prompt, layers, shared), _trunk(x_sample, p_sample, layers, shared))
```

```python
import functools
import math

import numpy as np
import jax
import jax.numpy as jnp
from jax import lax
from jax.experimental import pallas as pl
from jax.experimental.pallas import tpu as pltpu

f32 = jnp.float32
bf16 = jnp.bfloat16
i32 = jnp.int32

D_MODEL = 1024
DEPTH = 2
PE_DIM = 256
EPS = 1e-6
NEG_BIG = -1e30
HEADS = 4
HEAD_W = 64
BRANCH_W = HEADS * HEAD_W
GLA_DK = 32
GLA_RANK = 16
GLA_NORMALIZER = 16.0
GLA_CHUNK = 32
MLSTM_CHUNK = 64
HGRN_DF = 64
HGRN_CHUNK = 32
DIL_PATTERNS = ((128, 1), (512, 4), (2048, 16))
ROPE_THETA = 10000.0
PEER_HEADS = 8
PEER_NKEYS = 128
PEER_TOPK = 16
LANES = 128
PEER_SUB = 1024
EXP_CLAMP = 80.0

_PROJ_LAYOUT = (
    ('a_q', 128), ('a_k', 128), ('a_v', 256), ('a_g', 256), ('a_lr', 32),
    ('b_q', 256), ('b_k', 256), ('b_v', 256), ('b_i', 8), ('b_f', 8), ('b_o', 256),
    ('c_q', 256), ('c_k', 256), ('c_v', 256),
    ('d_q', 256), ('d_f', 512), ('d_i', 256), ('d_g', 256),
)
_PROJ_OFF = {}
_o = 0
for _n, _s in _PROJ_LAYOUT:
    _PROJ_OFF[_n] = (_o, _o + _s)
    _o += _s

ZA_W = 896
ZB_W = 2048
ZC_W = 768
ZD_W = 1280

VMEM_LIMIT = 56 * 1024 * 1024


def _cparams(*sem):
    return pltpu.CompilerParams(dimension_semantics=sem, vmem_limit_bytes=VMEM_LIMIT)


def _const_spec(shape):
    nd = len(shape)
    return pl.BlockSpec(shape, lambda *_: (0,) * nd, pipeline_mode=pl.Buffered(1))


def _mm(a, b):
    return jnp.dot(a, b, preferred_element_type=f32)


def _mm_nt(a, b):
    return lax.dot_general(a, b, (((1,), (1,)), ((), ())), preferred_element_type=f32)


def _mm_tn(a, b):
    return lax.dot_general(a, b, (((0,), (0,)), ((), ())), preferred_element_type=f32)


def _split3(x):
    hi = x.astype(bf16)
    r1 = x - hi.astype(f32)
    mid = r1.astype(bf16)
    lo = (r1 - mid.astype(f32)).astype(bf16)
    return hi, mid, lo


def _lmul3(m, x):
    hi, mid, lo = _split3(x)
    return _mm(m, hi) + _mm(m, mid) + _mm(m, lo)


def _rmul3(x, m):
    hi, mid, lo = _split3(x)
    return _mm(hi, m) + _mm(mid, m) + _mm(lo, m)


def _rms(x, g):
    return x * lax.rsqrt(jnp.mean(x * x, axis=-1, keepdims=True) + EPS) * g


def _iota(shape, dim):
    return lax.broadcasted_iota(i32, shape, dim)


def _tri(c, reverse):
    r, k = _iota((c, c), 0), _iota((c, c), 1)
    return jnp.where((k >= r) if reverse else (k <= r), 1.0, 0.0).astype(bf16)


def _block_avg(w, blk):
    same = (_iota((w, w), 0) // blk) == (_iota((w, w), 1) // blk)
    return jnp.where(same, 1.0 / blk, 0.0).astype(bf16)


def _rows(n, blk, fn):
    def body(i, c):
        fn(pl.multiple_of(i * blk, blk), i)
        return c
    lax.fori_loop(0, n // blk, body, 0)


def _head_rmsnorm(o, g, mavg):
    ms = _rmul3(o * o, mavg)
    return o * lax.rsqrt(ms + EPS) * g


def _proj_kernel(h_ref, g_ref, wa, wb, wc, wd, ba, bb, bc, bd, za, zb, zc, zd):
    u = _rms(h_ref[...], g_ref[...]).astype(bf16)
    for w, b, z in ((wa, ba, za), (wb, bb, zb), (wc, bc, zc), (wd, bd, zd)):
        z[...] = _mm(u, w[...]) + b[...]


def _proj(h, g, ws, bs, tm=512):
    t = h.shape[0]
    widths = (ZA_W, ZB_W, ZC_W, ZD_W)
    tok = lambda w: pl.BlockSpec((tm, w), lambda i: (i, 0))
    return pl.pallas_call(
        _proj_kernel,
        grid=(t // tm,),
        in_specs=[tok(D_MODEL), _const_spec((1, D_MODEL))]
        + [_const_spec((D_MODEL, w)) for w in widths] + [_const_spec((1, w)) for w in widths],
        out_specs=[tok(w) for w in widths],
        out_shape=[jax.ShapeDtypeStruct((t, w), f32) for w in widths],
        compiler_params=_cparams("parallel"),
        name="proj",
    )(h, g, *ws, *bs)


def _gl_pass(load, acc_ref, st_ref, *, s, c, dk, reverse, first):
    h, dv = HEADS, HEAD_W
    kw, vw, hc = h * dk, h * dv, h * c
    n = s // c
    tri = _tri(c, reverse)
    khead = _iota((c, kw), 1) // dk
    vhead = _iota((c, vw), 1) // dv
    sidx, trow = _iota((c, hc), 1) % c, _iota((c, hc), 0)
    causal = (sidx >= trow) if reverse else (sidx <= trow)
    bd = (_iota((vw, kw), 0) // dv) == (_iota((vw, kw), 1) // dk)
    st_ref[...] = jnp.zeros((vw, kw), f32)

    def body(ci, carry):
        cidx = (n - 1 - ci) if reverse else ci
        st = pl.multiple_of(cidx * c, c)
        q, k, v, g = load(st)
        b = _lmul3(tri, g)
        last = b[0:1] if reverse else b[c - 1:c]
        mid = b[c // 2:c // 2 + 1]
        qt = (q * jnp.exp(jnp.minimum(b - mid, EXP_CLAMP))).astype(bf16)
        kt = k * jnp.exp(jnp.minimum(mid - b, EXP_CLAMP))
        kexp = jnp.concatenate([jnp.where(khead == j, kt, 0.0).astype(bf16) for j in range(h)], axis=0)
        vexp = jnp.concatenate([jnp.where(vhead == j, v, 0.0).astype(bf16) for j in range(h)], axis=0)
        att = jnp.where(causal, _mm_nt(qt, kexp), 0.0).astype(bf16)
        o = _mm(att, vexp)
        state = st_ref[...]
        o = o + _mm_nt((q * jnp.exp(b)).astype(bf16), state.astype(bf16))
        kd = (k * jnp.exp(last - b)).astype(bf16)
        upd = _mm_tn(v.astype(bf16), kd)
        st_ref[...] = state * jnp.exp(last) + jnp.where(bd, upd, 0.0)
        if first:
            acc_ref[pl.ds(st, c), :] = o
        else:
            acc_ref[pl.ds(st, c), :] += o
        return carry

    lax.fori_loop(0, n, body, 0)


def _gla_kernel(z_ref, wlr_ref, blr_ref, g_ref, o_ref, gk_ref, acc_ref, st_ref, *, s):
    rb = 256

    def gates(st, i):
        lr = z_ref[pl.ds(st, rb), 768:896].astype(bf16)
        x = _mm(lr, wlr_ref[...]) + blr_ref[...]
        gk_ref[pl.ds(st, rb), :] = jax.nn.log_sigmoid(x) * (1.0 / GLA_NORMALIZER)
    _rows(s, rb, gates)

    scale = GLA_DK ** -0.5
    for d in range(2):
        def load(st, d=d):
            q = z_ref[pl.ds(st, GLA_CHUNK), 0:128] * scale
            k = z_ref[pl.ds(st, GLA_CHUNK), 128:256]
            v = z_ref[pl.ds(st, GLA_CHUNK), 256:512]
            g = gk_ref[pl.ds(st, GLA_CHUNK), d * 128:(d + 1) * 128]
            return q, k, v, g
        _gl_pass(load, acc_ref, st_ref, s=s, c=GLA_CHUNK, dk=GLA_DK, reverse=(d == 1), first=(d == 0))

    mavg = _block_avg(BRANCH_W, HEAD_W)

    def out(st, i):
        y = _head_rmsnorm(acc_ref[pl.ds(st, rb), :], g_ref[...], mavg)
        gate = z_ref[pl.ds(st, rb), 512:768]
        o_ref[pl.ds(st, rb), :] = y * (gate * jax.nn.sigmoid(gate))
    _rows(s, rb, out)


def _gla(za, wlr, blr, g, nb, s):
    return pl.pallas_call(
        functools.partial(_gla_kernel, s=s),
        grid=(nb,),
        in_specs=[pl.BlockSpec((None, s, ZA_W), lambda b: (b, 0, 0)),
                  _const_spec((LANES, 256)), _const_spec((1, 256)), _const_spec((1, BRANCH_W))],
        out_specs=pl.BlockSpec((None, s, BRANCH_W), lambda b: (b, 0, 0)),
        out_shape=jax.ShapeDtypeStruct((nb, s, BRANCH_W), f32),
        scratch_shapes=[pltpu.VMEM((s, 256), f32), pltpu.VMEM((s, BRANCH_W), f32),
                        pltpu.VMEM((BRANCH_W, HEADS * GLA_DK), f32)],
        compiler_params=_cparams("parallel"),
        name="gla",
    )(za.reshape(nb, s, ZA_W), wlr, blr, g)


def _hgrn_kernel(z_ref, lbp_ref, g_ref, o_ref, q_ref, lf_ref, kd_ref, acc_ref, st_ref, *, s, layer):
    rb = 256
    x = lbp_ref[...]
    e = jnp.exp(x - jnp.max(x, axis=0, keepdims=True))
    p = e / jnp.sum(e, axis=0, keepdims=True)
    lb = p[0:1]
    for j in range(1, layer + 1):
        lb = lb + p[j:j + 1]
    lb = lb - p[0:1]

    def prep(st, i):
        qz = z_ref[pl.ds(st, rb), 0:256]
        q_ref[pl.ds(st, rb), :] = qz * jax.nn.sigmoid(qz)
        fz = z_ref[pl.ds(st, rb), 256:768]
        lf_ref[pl.ds(st, rb), :] = jnp.log(lb + (1.0 - lb) * jax.nn.sigmoid(fz))
        kd_ref[pl.ds(st, rb), :] = (1.0 - lb) * jax.nn.sigmoid(-fz)
    _rows(s, rb, prep)

    kw = HEADS * HGRN_DF
    for d in range(2):
        def load(st, d=d):
            q = q_ref[pl.ds(st, HGRN_CHUNK), :]
            k = kd_ref[pl.ds(st, HGRN_CHUNK), d * kw:(d + 1) * kw]
            v = z_ref[pl.ds(st, HGRN_CHUNK), 768:1024]
            g = lf_ref[pl.ds(st, HGRN_CHUNK), d * kw:(d + 1) * kw]
            return q, k, v, g
        _gl_pass(load, acc_ref, st_ref, s=s, c=HGRN_CHUNK, dk=HGRN_DF, reverse=(d == 1), first=(d == 0))

    mavg = _block_avg(BRANCH_W, HEAD_W)

    def out(st, i):
        y = _head_rmsnorm(acc_ref[pl.ds(st, rb), :], g_ref[...], mavg)
        gate = z_ref[pl.ds(st, rb), 1024:1280]
        o_ref[pl.ds(st, rb), :] = y * (gate * jax.nn.sigmoid(gate))
    _rows(s, rb, out)


def _hgrn(zd, lbp, g, nb, s, layer):
    kw = HEADS * HGRN_DF
    return pl.pallas_call(
        functools.partial(_hgrn_kernel, s=s, layer=layer),
        grid=(nb,),
        in_specs=[pl.BlockSpec((None, s, ZD_W), lambda b: (b, 0, 0)),
                  _const_spec((DEPTH, 2 * kw)), _const_spec((1, BRANCH_W))],
        out_specs=pl.BlockSpec((None, s, BRANCH_W), lambda b: (b, 0, 0)),
        out_shape=jax.ShapeDtypeStruct((nb, s, BRANCH_W), f32),
        scratch_shapes=[pltpu.VMEM((s, kw), f32), pltpu.VMEM((s, 2 * kw), f32), pltpu.VMEM((s, 2 * kw), f32),
                        pltpu.VMEM((s, BRANCH_W), f32), pltpu.VMEM((BRANCH_W, kw), f32)],
        compiler_params=_cparams("parallel"),
        name="hgrn",
    )(zd.reshape(nb, s, ZD_W), lbp, g)


def _mlstm_pass(z_ref, q_ref, k_ref, acc_ref, st_ref, nv_ref, m_ref, *, s, reverse, first, ioff, foff):
    c, w = MLSTM_CHUNK, BRANCH_W
    n = s // c
    tri = _tri(c, reverse)
    ones = jnp.ones((c, c), bf16)
    lane, row = _iota((c, w), 1), _iota((c, w), 0)
    head, sidx = lane // HEAD_W, lane % HEAD_W
    diag = sidx == row
    causal = (sidx >= row) if reverse else (sidx <= row)
    same = (_iota((w, w), 0) // HEAD_W) == (_iota((w, w), 1) // HEAD_W)
    bones = jnp.where(same, 1.0, 0.0).astype(bf16)
    st_ref[...] = jnp.zeros((w, w), f32)
    nv_ref[...] = jnp.zeros((1, w), f32)
    m_ref[...] = jnp.zeros((1, w), f32)

    def body(ci, carry):
        cidx = (n - 1 - ci) if reverse else ci
        st = pl.multiple_of(cidx * c, c)
        q = q_ref[pl.ds(st, c), :]
        k = k_ref[pl.ds(st, c), :]
        v = z_ref[pl.ds(st, c), 512:768]
        ig = z_ref[pl.ds(st, c), ioff:ioff + w]
        lf = jax.nn.log_sigmoid(z_ref[pl.ds(st, c), foff:foff + w])
        b = _lmul3(tri, lf)
        r = _lmul3(ones, jnp.where(diag, ig - b, 0.0))
        log_d = jnp.where(causal, b + r, NEG_BIG)
        m_prev = m_ref[...]
        log_prev = b + m_prev
        mrow = jnp.full((c, w), NEG_BIG, f32)
        for j in range(HEADS):
            mj = jnp.max(jnp.where(head == j, log_d, NEG_BIG), axis=-1, keepdims=True)
            mrow = jnp.where(head == j, mj, mrow)
        m_t = jnp.maximum(mrow, log_prev)
        w_prev = jnp.exp(log_prev - m_t)
        qb = q.astype(bf16)
        kexp = jnp.concatenate([jnp.where(head == j, k, 0.0).astype(bf16) for j in range(HEADS)], axis=0)
        vexp = jnp.concatenate([jnp.where(head == j, v, 0.0).astype(bf16) for j in range(HEADS)], axis=0)
        sc = _mm_nt(qb, kexp) * jnp.exp(log_d - m_t)
        state = st_ref[...]
        num = w_prev * _mm_nt(qb, state.astype(bf16)) + _mm(sc.astype(bf16), vexp)
        nv = nv_ref[...]
        den = w_prev * _rmul3(q * nv, bones) + _rmul3(sc, bones)
        hout = num / jnp.maximum(jnp.abs(den), jnp.exp(-m_t))
        last = b[0:1] if reverse else b[c - 1:c]
        log_w = last - b + ig
        m_new = jnp.maximum(last + m_prev, jnp.max(log_w, axis=0, keepdims=True))
        w_s = jnp.exp(log_w - m_new)
        dec = jnp.exp(last + m_prev - m_new)
        upd = _mm_tn((v * w_s).astype(bf16), k.astype(bf16))
        st_ref[...] = dec * state + jnp.where(same, upd, 0.0)
        nv_ref[...] = dec * nv + jnp.sum(w_s * k, axis=0, keepdims=True)
        m_ref[...] = m_new
        if first:
            acc_ref[pl.ds(st, c), :] = hout
        else:
            acc_ref[pl.ds(st, c), :] += hout
        return carry

    lax.fori_loop(0, n, body, 0)


def _mlstm_kernel(z_ref, cw_ref, cb_ref, g_ref, o_ref, q_ref, k_ref, acc_ref, st_ref, nv_ref, m_ref, *, s):
    rb = 256
    nblk = s // rb
    w = BRANCH_W

    def conv(st, i):
        x = z_ref[pl.ds(st, rb), 0:2 * w]
        pst = pl.multiple_of(jnp.maximum(st - 8, 0), 8)
        nst = pl.multiple_of(jnp.minimum(st + rb, s - 8), 8)
        prev = z_ref[pl.ds(pst, 8), 0:2 * w][7:8]
        nxt = z_ref[pl.ds(nst, 8), 0:2 * w][0:1]
        prev = jnp.where(i == 0, 0.0, prev)
        nxt = jnp.where(i == nblk - 1, 0.0, nxt)
        row = _iota((rb, 2 * w), 0)
        xm = jnp.where(row == 0, prev, pltpu.roll(x, 1, 0))
        xp = jnp.where(row == rb - 1, nxt, pltpu.roll(x, rb - 1, 0))
        y = cb_ref[...] + xm * cw_ref[0:1, :]
        y = y + x * cw_ref[1:2, :]
        y = y + xp * cw_ref[2:3, :]
        y = y * jax.nn.sigmoid(y)
        q_ref[pl.ds(st, rb), :] = y[:, 0:w]
        k_ref[pl.ds(st, rb), :] = y[:, w:2 * w] * (HEAD_W ** -0.5)
    _rows(s, rb, conv)

    for d in range(2):
        _mlstm_pass(z_ref, q_ref, k_ref, acc_ref, st_ref, nv_ref, m_ref, s=s, reverse=(d == 1), first=(d == 0),
                    ioff=1024 + d * w, foff=1536 + d * w)

    mavg = _block_avg(w, HEAD_W)

    def out(st, i):
        y = _head_rmsnorm(acc_ref[pl.ds(st, rb), :], g_ref[...], mavg)
        o_ref[pl.ds(st, rb), :] = jax.nn.sigmoid(z_ref[pl.ds(st, rb), 768:1024]) * y
    _rows(s, rb, out)


def _mlstm(zb, cw, cb, g, nb, s):
    w = BRANCH_W
    return pl.pallas_call(
        functools.partial(_mlstm_kernel, s=s),
        grid=(nb,),
        in_specs=[pl.BlockSpec((None, s, ZB_W), lambda b: (b, 0, 0)),
                  _const_spec((3, 2 * w)), _const_spec((1, 2 * w)), _const_spec((1, w))],
        out_specs=pl.BlockSpec((None, s, w), lambda b: (b, 0, 0)),
        out_shape=jax.ShapeDtypeStruct((nb, s, w), f32),
        scratch_shapes=[pltpu.VMEM((s, w), f32), pltpu.VMEM((s, w), f32), pltpu.VMEM((s, w), f32),
                        pltpu.VMEM((w, w), f32), pltpu.VMEM((1, w), f32), pltpu.VMEM((1, w), f32)],
        compiler_params=_cparams("parallel"),
        name="mlstm",
    )(zb.reshape(nb, s, ZB_W), cw, cb, g)


def _rope(x, cos, sa, sb):
    w = x.shape[-1]
    half = HEAD_W // 2
    return x * cos + pltpu.roll(x, w - half, 1) * sa + pltpu.roll(x, half, 1) * sb


def _dil_kernel(z_ref, cos_ref, sa_ref, sb_ref, o_ref, kr_ref, vm_ref, *, s, tq):
    w = BRANCH_W
    qi = pl.program_id(1)
    rb = 256

    @pl.when(qi == 0)
    def _():
        head = _iota((rb, w), 1) // HEAD_W

        def prep(st, i):
            x = z_ref[pl.ds(st, rb), w:2 * w]
            kr_ref[pl.ds(st, rb), :] = _rope(x, cos_ref[pl.ds(st, rb), :], sa_ref[pl.ds(st, rb), :],
                                             sb_ref[pl.ds(st, rb), :]).astype(bf16)
            v = z_ref[pl.ds(st, rb), 2 * w:3 * w]
            for j in range(HEADS):
                vm_ref[j, pl.ds(st, rb), :] = jnp.where(head == j, v, 0.0).astype(bf16)
        _rows(s, rb, prep)

    q0 = pl.multiple_of(qi * tq, tq)
    qr = _rope(z_ref[pl.ds(q0, tq), 0:w], cos_ref[pl.ds(q0, tq), :], sa_ref[pl.ds(q0, tq), :],
               sb_ref[pl.ds(q0, tq), :])
    rel = _iota((tq, s), 1) - (_iota((tq, s), 0) + q0)
    dist = jnp.abs(rel)
    cnt = jnp.zeros((tq, s), f32)
    for window, dil in DIL_PATTERNS:
        reach = (window // (2 * dil)) * dil
        ok = dist <= reach
        if dil > 1:
            ok = jnp.logical_and(ok, (rel & (dil - 1)) == 0)
        cnt = cnt + jnp.where(ok, 1.0, 0.0)
    valid = cnt > 0.0
    qhead = _iota((tq, w), 1) // HEAD_W
    acc = jnp.zeros((tq, w), f32)
    for j in range(HEADS):
        qh = jnp.where(qhead == j, qr, 0.0).astype(bf16)
        sc = _mm_nt(qh, kr_ref[...]) * (HEAD_W ** -0.5)
        sc = jnp.where(valid, sc, NEG_BIG)
        m = jnp.max(sc, axis=-1, keepdims=True)
        p = jnp.exp(sc - m) * cnt
        l = jnp.sum(p, axis=-1, keepdims=True)
        acc = acc + _mm(p.astype(bf16), vm_ref[j]) / l
    o_ref[...] = acc


def _dil(zc, cos, sa, sb, nb, s, tq=256):
    w = BRANCH_W
    for _, dil in DIL_PATTERNS:
        assert dil & (dil - 1) == 0 and s % dil == 0
    return pl.pallas_call(
        functools.partial(_dil_kernel, s=s, tq=tq),
        grid=(nb, s // tq),
        in_specs=[pl.BlockSpec((None, s, ZC_W), lambda b, i: (b, 0, 0)),
                  _const_spec((s, w)), _const_spec((s, w)), _const_spec((s, w))],
        out_specs=pl.BlockSpec((None, tq, w), lambda b, i: (b, i, 0)),
        out_shape=jax.ShapeDtypeStruct((nb, s, w), f32),
        scratch_shapes=[pltpu.VMEM((s, w), bf16), pltpu.VMEM((HEADS, s, w), bf16)],
        compiler_params=_cparams("parallel", "arbitrary"),
        name="dilattn",
    )(zc.reshape(nb, s, ZC_W), cos, sa, sb)


def _merge_kernel(h_ref, ya, yb, yc, yd, g_ref, wg_ref, wbr_ref, wo_ref, o_ref):
    x = h_ref[...]
    u = _rms(x, g_ref[...]).astype(bf16)
    merged = None
    for n, y in enumerate((ya, yb, yc, yd)):
        gate = jax.nn.sigmoid(_mm(u, wg_ref[n]))
        term = gate * _mm(y[...].astype(bf16), wbr_ref[n])
        merged = term if merged is None else merged + term
    o_ref[...] = x + _mm(merged.astype(bf16), wo_ref[...])


def _merge(h, ys, g, wg, wbr, wo, tm=512):
    t = h.shape[0]
    tok = lambda w: pl.BlockSpec((tm, w), lambda i: (i, 0))
    return pl.pallas_call(
        _merge_kernel,
        grid=(t // tm,),
        in_specs=[tok(D_MODEL)] + [tok(BRANCH_W)] * 4
        + [_const_spec((1, D_MODEL)), _const_spec((4, D_MODEL, D_MODEL)),
           _const_spec((4, BRANCH_W, D_MODEL)), _const_spec((D_MODEL, D_MODEL))],
        out_specs=tok(D_MODEL),
        out_shape=jax.ShapeDtypeStruct((t, D_MODEL), f32),
        compiler_params=_cparams("parallel"),
        name="merge",
    )(h, *ys, g, wg, wbr, wo)


def _stair_cells():
    return [(a, b) for a in range(PEER_TOPK) for b in range(PEER_TOPK) if (a + 1) * (b + 1) <= PEER_TOPK]


_CELLS = _stair_cells()
_NCELL = len(_CELLS)
_NCELL_PAD = -(-_NCELL // 8) * 8


def _extract_top(work, rowf, nrows, exact):
    tops = []
    rank = jnp.full(work.shape, float(PEER_TOPK), f32)
    for r in range(PEER_TOPK):
        mx = jnp.max(work, axis=0, keepdims=True)
        hit = work == mx
        if exact:
            cand = jnp.where(hit, rowf, float(nrows))
            hit = cand == jnp.min(cand, axis=0, keepdims=True)
        work = jnp.where(hit, -jnp.inf, work)
        rank = jnp.where(hit, float(r), rank)
        tops.append(mx)
    removed = jnp.sum(jnp.where(rank < float(PEER_TOPK), 1.0, 0.0), axis=0, keepdims=True)
    return tops, rank, removed


def _route_kernel(h_ref, g_ref, wpq_ref, keys_ref, sa_ref, u_ref, a_ref, cnt_ref, rk_ref, bn_ref,
                  q_ref, m_ref, *, tm):
    u = _rms(h_ref[...], g_ref[...]).astype(bf16)
    u_ref[...] = u
    q_ref[...] = _mm(u, wpq_ref[...]).astype(bf16)
    nk = PEER_NKEYS
    rowf = _iota((nk, tm), 0).astype(f32)
    rowcf = _iota((_NCELL_PAD, tm), 0).astype(f32)
    topk = float(PEER_TOPK)

    def head_tables(hd, exact):
        scs, tops, ranks = [], [], []
        tied = jnp.zeros((1, tm), f32)
        for p in range(2):
            idx = hd * 2 + p
            qs = q_ref[:, pl.ds(pl.multiple_of(idx * nk, nk), nk)]
            sc = _mm_nt(keys_ref[idx], qs)
            top, rank, removed = _extract_top(sc, rowf, nk, exact)
            tied = jnp.where(removed == topk, tied, 1.0)
            scs.append(sc)
            tops.append(top)
            ranks.append(rank)
        m_ref[...] = jnp.full((_NCELL_PAD, tm), -jnp.inf, f32)
        for ci, (a, b) in enumerate(_CELLS):
            m_ref[ci:ci + 1, :] = tops[0][a] + tops[1][b]
        cand = m_ref[...]
        _, crank, removed = _extract_top(cand, rowcf, _NCELL_PAD, exact)
        tied = jnp.where(removed == topk, tied, 1.0)
        chosen = crank < topk
        wgt = jnp.where(chosen, jnp.exp(cand - cand[0:1]), 0.0)
        zsum = jnp.sum(wgt, axis=0, keepdims=True)
        rowcnt = _mm(sa_ref[...], jnp.where(chosen, 1.0, 0.0).astype(bf16))
        cnt = jnp.zeros((nk, tm), f32)
        for a in range(PEER_TOPK):
            cnt = jnp.where(ranks[0] == float(a), rowcnt[a:a + 1], cnt)
        rows = pl.ds(pl.multiple_of(hd * nk, nk), nk)
        amat = jnp.exp(scs[0] - tops[0][0])
        prow = pl.ds(pl.multiple_of(hd * (nk // 2), nk // 2), nk // 2)
        rkp = pltpu.bitcast(ranks[1].astype(bf16), jnp.uint32)
        bnp = pltpu.bitcast((jnp.exp(scs[1] - tops[1][0]) / zsum).astype(bf16), jnp.uint32)
        for tb in range(tm // LANES):
            lanes = slice(tb * LANES, (tb + 1) * LANES)
            a_ref[tb, rows, :] = amat[:, lanes]
            cnt_ref[tb, rows, :] = cnt[:, lanes]
            rk_ref[tb, prow, :] = rkp[:, lanes]
            bn_ref[tb, prow, :] = bnp[:, lanes]
        return tied

    def per_head(hd, carry):
        tied = head_tables(hd, False)

        @pl.when(jnp.max(tied) > 0.0)
        def _():
            head_tables(hd, True)
        return carry

    lax.fori_loop(0, PEER_HEADS, per_head, 0)


def _route(h, g, wpq, keys, sa, tm=256):
    t = h.shape[0]
    nh, nk = PEER_HEADS, PEER_NKEYS
    tab = pl.BlockSpec((tm // LANES, nh * nk, LANES), lambda i: (i, 0, 0))
    ptab = pl.BlockSpec((tm // LANES, nh * nk // 2, LANES), lambda i: (i, 0, 0))
    return pl.pallas_call(
        functools.partial(_route_kernel, tm=tm),
        grid=(t // tm,),
        in_specs=[pl.BlockSpec((tm, D_MODEL), lambda i: (i, 0)), _const_spec((1, D_MODEL)),
                  _const_spec((D_MODEL, 2 * nh * nk)), _const_spec((2 * nh, nk, nk)),
                  _const_spec((PEER_TOPK, _NCELL_PAD))],
        out_specs=[pl.BlockSpec((tm, D_MODEL), lambda i: (i, 0)), tab, tab, ptab, ptab],
        out_shape=[jax.ShapeDtypeStruct((t, D_MODEL), bf16)] + [jax.ShapeDtypeStruct((t // LANES, nh * nk, LANES), f32)] * 2
        + [jax.ShapeDtypeStruct((t // LANES, nh * nk // 2, LANES), jnp.uint32)] * 2,
        scratch_shapes=[pltpu.VMEM((tm, 2 * nh * nk), bf16), pltpu.VMEM((_NCELL_PAD, tm), f32)],
        compiler_params=_cparams("parallel"),
        name="route",
    )(h, g, wpq, keys, sa)


def _gelu_bf16(x):
    return (0.5 * x * (1.0 + lax.erf(x * math.sqrt(0.5)))).astype(bf16)


def _peer_gates(act_ref, p_ref, tabs, i0, tp):
    a_ref, cnt_ref, rk_ref, bn_ref = tabs
    nk = PEER_NKEYS
    for ii in range(PEER_SUB // nk):
        for tb in range(tp // LANES):
            lanes = slice(tb * LANES, (tb + 1) * LANES)
            gate = None
            for hd in range(PEER_HEADS):
                row = pl.ds(hd * nk + i0 + ii, 1)
                a = jnp.broadcast_to(a_ref[tb, row, :], (nk, LANES)).astype(bf16)
                c = jnp.broadcast_to(cnt_ref[tb, row, :], (nk, LANES)).astype(bf16)
                prow = slice(hd * (nk // 2), (hd + 1) * (nk // 2))
                rk = pltpu.bitcast(rk_ref[tb, prow, :], bf16)
                bn = pltpu.bitcast(bn_ref[tb, prow, :], bf16)
                term = a * jnp.where(rk < c, bn, jnp.zeros((), bf16))
                gate = term if gate is None else gate + term
            p_ref[ii * nk:(ii + 1) * nk, lanes] = gate * act_ref[ii * nk:(ii + 1) * nk, lanes]


def _peer_kernel(u_ref, a_ref, cnt_ref, rk_ref, bn_ref, pu_ref, pvt_ref, o_ref, acc_ref, act0, act1, p0, p1,
                 *, ne, tp):
    g = pl.program_id(0)
    nblocks = pl.num_programs(0) - 2
    b1 = jnp.clip(g - 1, 0, nblocks - 1)
    b2 = jnp.clip(g - 2, 0, nblocks - 1)
    i1 = (b1 % ne) * (PEER_SUB // PEER_NKEYS)
    tabs = (a_ref, cnt_ref, rk_ref, bn_ref)

    @pl.when(g == 0)
    def _():
        for ref in (act0, act1, p0, p1):
            ref[...] = jnp.zeros(ref.shape, bf16)

    @pl.when(b2 % ne == 0)
    def _():
        acc_ref[...] = jnp.zeros(acc_ref.shape, f32)

    def stage(act_w, act_r, p_w, p_r):
        _peer_gates(act_r, p_w, tabs, i1, tp)
        act_w[...] = _gelu_bf16(_mm_nt(pu_ref[...], u_ref[...]))
        acc_ref[...] += _mm(pvt_ref[...], p_r[...])

    @pl.when(g % 2 == 0)
    def _():
        stage(act0, act1, p1, p0)

    @pl.when(g % 2 == 1)
    def _():
        stage(act1, act0, p0, p1)

    @pl.when(jnp.logical_and(g >= 2, b2 % ne == ne - 1))
    def _():
        o_ref[...] = acc_ref[...].T


def _peer(u2, tabs, pu, pvt, tp=512):
    t = u2.shape[0]
    nh, nk = PEER_HEADS, PEER_NKEYS
    ne = pu.shape[0] // PEER_SUB
    nblocks = (t // tp) * ne
    blk = lambda lag: (lambda g: jnp.clip(g - lag, 0, nblocks - 1))
    b0, b1, b2 = blk(0), blk(1), blk(2)
    tab = pl.BlockSpec((tp // LANES, nh * nk, LANES), lambda g: (b1(g) // ne, 0, 0))
    ptab = pl.BlockSpec((tp // LANES, nh * nk // 2, LANES), lambda g: (b1(g) // ne, 0, 0))
    return pl.pallas_call(
        functools.partial(_peer_kernel, ne=ne, tp=tp),
        grid=(nblocks + 2,),
        in_specs=[pl.BlockSpec((tp, D_MODEL), lambda g: (b0(g) // ne, 0)), tab, tab, ptab, ptab,
                  pl.BlockSpec((PEER_SUB, D_MODEL), lambda g: (b0(g) % ne, 0)),
                  pl.BlockSpec((D_MODEL, PEER_SUB), lambda g: (0, b2(g) % ne))],
        out_specs=pl.BlockSpec((tp, D_MODEL), lambda g: (b2(g) // ne, 0)),
        out_shape=jax.ShapeDtypeStruct((t, D_MODEL), f32),
        scratch_shapes=[pltpu.VMEM((D_MODEL, tp), f32)] + [pltpu.VMEM((PEER_SUB, tp), bf16)] * 4,
        compiler_params=_cparams("arbitrary"),
        name="peer",
    )(u2, *tabs, pu, pvt)


def _pe_kernel(h_ref, po_ref, p_ref, g_ref, wpg_ref, wpe_ref, gf_ref, o_ref, *, final):
    h = h_ref[...] + po_ref[...]
    gate = jax.nn.sigmoid(_mm(_rms(h, g_ref[...]).astype(bf16), wpg_ref[...]))
    h = h + gate * _mm(p_ref[...].astype(bf16), wpe_ref[...])
    if final:
        h = _rms(h, gf_ref[...])
    o_ref[...] = h


def _pe(h, po, p, g, wpg, wpe, gf, final, tm=512):
    t = h.shape[0]
    tok = lambda w: pl.BlockSpec((tm, w), lambda i: (i, 0))
    return pl.pallas_call(
        functools.partial(_pe_kernel, final=final),
        grid=(t // tm,),
        in_specs=[tok(D_MODEL), tok(D_MODEL), tok(PE_DIM), _const_spec((1, D_MODEL)),
                  _const_spec((D_MODEL, D_MODEL)), _const_spec((PE_DIM, D_MODEL)), _const_spec((1, D_MODEL))],
        out_specs=tok(D_MODEL),
        out_shape=jax.ShapeDtypeStruct((t, D_MODEL), f32),
        compiler_params=_cparams("parallel"),
        name="pe_embed",
    )(h, po, p, g, wpg, wpe, gf)


def _cols(w, name):
    a, b = _PROJ_OFF[name]
    return w[..., a:b]


def _proj_groups(w):
    pad = jnp.zeros(w.shape[:-1] + (ZA_W - 800,), w.dtype)
    ga = jnp.concatenate([_cols(w, n) for n in ('a_q', 'a_k', 'a_v', 'a_g', 'a_lr')] + [pad], axis=-1)
    gates = [jnp.repeat(_cols(w, n), HEAD_W, axis=-1) for n in ('b_i', 'b_f')]
    gb = jnp.concatenate([_cols(w, n) for n in ('b_q', 'b_k', 'b_v', 'b_o')] + gates, axis=-1)
    gc = jnp.concatenate([_cols(w, n) for n in ('c_q', 'c_k', 'c_v')], axis=-1)
    gd = jnp.concatenate([_cols(w, n) for n in ('d_q', 'd_f', 'd_i', 'd_g')], axis=-1)
    return ga, gb, gc, gd


def _rope_tables(s):
    half = HEAD_W // 2
    inv = ROPE_THETA ** (-jnp.arange(half, dtype=f32) / half)
    ang = jnp.arange(s, dtype=f32)[:, None] * inv[None, :]
    cos, sin = jnp.cos(ang), jnp.sin(ang)
    zero = jnp.zeros_like(sin)
    tile = lambda a, b: jnp.tile(jnp.concatenate([a, b], axis=-1), (1, HEADS))
    return tile(cos, cos), tile(-sin, zero), tile(zero, sin)


def _row_count_matrix():
    m = np.zeros((PEER_TOPK, _NCELL_PAD), np.float32)
    for ci, (a, _) in enumerate(_CELLS):
        m[a, ci] = 1.0
    return jnp.asarray(m, bf16)


def _prepare(g_mix, w_in, b_in, w_lr2, b_lr2, conv_w, conv_b, lb_param, g_a, g_b, g_d, w_gate, w_br, w_out,
             g_ffn, w_pq, sub_keys, peer_u, peer_v, g_pe, w_pe, w_pg, g_final):
    layers = []
    row = lambda a: a.reshape(1, -1)
    for l in range(DEPTH):
        wlr = jnp.zeros((LANES, 256), f32)
        for n in range(2):
            wlr = wlr.at[n * GLA_RANK:(n + 1) * GLA_RANK, n * 128:(n + 1) * 128].set(w_lr2[l, n])
        layers.append(dict(
            g_mix=row(g_mix[l]),
            w_proj=[w.astype(bf16) for w in _proj_groups(w_in[l])],
            b_proj=list(_proj_groups(row(b_in[l]))),
            wlr=wlr.astype(bf16), blr=row(b_lr2[l]),
            conv_w=conv_w[l], conv_b=row(conv_b[l]),
            g_a=row(g_a[l]), g_b=row(g_b[l]), g_d=row(g_d[l]),
            w_gate=w_gate[l].astype(bf16), w_br=w_br[l].astype(bf16), w_out=w_out[l].astype(bf16),
            g_ffn=row(g_ffn[l]), w_pq=w_pq[l].astype(bf16),
            keys=sub_keys[l].reshape(2 * PEER_HEADS, PEER_NKEYS, -1).astype(bf16),
            peer_u=peer_u[l].astype(bf16), peer_vt=peer_v[l].astype(bf16).T,
            g_pe=row(g_pe[l]), w_pe=w_pe[l].astype(bf16), w_pg=w_pg[l].astype(bf16),
        ))
    shared = dict(
        lbp=jnp.transpose(lb_param, (1, 0, 2)).reshape(DEPTH, -1),
        g_final=row(g_final), sa=_row_count_matrix(),
    )
    return layers, shared


def _trunk(x, p, layers, shared):
    nb, s, d = x.shape
    t = nb * s
    h = x.reshape(t, d)
    cos, sa, sb = _rope_tables(s)
    for l, w in enumerate(layers):
        za, zb, zc, zd = _proj(h, w['g_mix'], w['w_proj'], w['b_proj'])
        ya = _gla(za, w['wlr'], w['blr'], w['g_a'], nb, s)
        yb = _mlstm(zb, w['conv_w'], w['conv_b'], w['g_b'], nb, s)
        yc = _dil(zc, cos, sa, sb, nb, s)
        yd = _hgrn(zd, shared['lbp'], w['g_d'], nb, s, l)
        ys = [y.reshape(t, BRANCH_W) for y in (ya, yb, yc, yd)]
        h1 = _merge(h, ys, w['g_mix'], w['w_gate'], w['w_br'], w['w_out'])
        u2, *tabs = _route(h1, w['g_ffn'], w['w_pq'], w['keys'], shared['sa'])
        po = _peer(u2, tabs, w['peer_u'], w['peer_vt'])
        h = _pe(h1, po, p[l].reshape(t, PE_DIM), w['g_pe'], w['w_pg'], w['w_pe'], shared['g_final'],
                final=(l == DEPTH - 1))
    return h.reshape(nb, s, d)


def kernel(x_prompt, x_sample, p_prompt, p_sample, g_mix, w_in, b_in, w_lr2, b_lr2, conv_w, conv_b, lb_param, g_a, g_b, g_d, w_gate, w_br, w_out, g_ffn, w_pq, sub_keys, peer_u, peer_v, g_pe, w_pe, w_pg, g_final):
    layers, shared = _prepare(g_mix, w_in, b_in, w_lr2, b_lr2, conv_w, conv_b, lb_param, g_a, g_b, g_d, w_gate,
                              w_br, w_out, g_ffn, w_pq, sub_keys, peer_u, peer_v, g_pe, w_pe, w_pg, g_final)
    return (_trunk(x_prompt, p_prompt, layers, shared), _trunk(x_sample, p_sample, layers, shared))
```

```python
import functools
import math

import numpy as np
import jax
import jax.numpy as jnp
from jax import lax
from jax.experimental import pallas as pl
from jax.experimental.pallas import tpu as pltpu

f32 = jnp.float32
bf16 = jnp.bfloat16
i32 = jnp.int32

D_MODEL = 1024
DEPTH = 2
PE_DIM = 256
EPS = 1e-6
NEG_BIG = -1e30
HEADS = 4
HEAD_W = 64
BRANCH_W = HEADS * HEAD_W
GLA_DK = 32
GLA_RANK = 16
GLA_NORMALIZER = 16.0
GLA_CHUNK = 32
MLSTM_CHUNK = 64
HGRN_DF = 64
HGRN_CHUNK = 32
DIL_PATTERNS = ((128, 1), (512, 4), (2048, 16))
ROPE_THETA = 10000.0
PEER_HEADS = 8
PEER_NKEYS = 128
PEER_TOPK = 16
LANES = 128
PEER_SUB = 1024
EXP_CLAMP = 80.0
GL_UNROLL = 4
MLSTM_UNROLL = 2

_PROJ_LAYOUT = (
    ('a_q', 128), ('a_k', 128), ('a_v', 256), ('a_g', 256), ('a_lr', 32),
    ('b_q', 256), ('b_k', 256), ('b_v', 256), ('b_i', 8), ('b_f', 8), ('b_o', 256),
    ('c_q', 256), ('c_k', 256), ('c_v', 256),
    ('d_q', 256), ('d_f', 512), ('d_i', 256), ('d_g', 256),
)
_PROJ_OFF = {}
_o = 0
for _n, _s in _PROJ_LAYOUT:
    _PROJ_OFF[_n] = (_o, _o + _s)
    _o += _s

ZA_W = 896
ZB_W = 2048
ZC_W = 768
ZD_W = 1280

VMEM_LIMIT = 56 * 1024 * 1024


def _cparams(*sem):
    return pltpu.CompilerParams(dimension_semantics=sem, vmem_limit_bytes=VMEM_LIMIT)


def _const_spec(shape):
    nd = len(shape)
    return pl.BlockSpec(shape, lambda *_: (0,) * nd, pipeline_mode=pl.Buffered(1))


def _mm(a, b):
    return jnp.dot(a, b, preferred_element_type=f32)


def _mm_nt(a, b):
    return lax.dot_general(a, b, (((1,), (1,)), ((), ())), preferred_element_type=f32)


def _mm_tn(a, b):
    return lax.dot_general(a, b, (((0,), (0,)), ((), ())), preferred_element_type=f32)


def _split3(x):
    hi = x.astype(bf16)
    r1 = x - hi.astype(f32)
    mid = r1.astype(bf16)
    lo = (r1 - mid.astype(f32)).astype(bf16)
    return hi, mid, lo


def _lmul3(m, x):
    w = x.shape[1]
    y = _mm(m, jnp.concatenate(_split3(x), axis=1))
    return y[:, 0:w] + y[:, w:2 * w] + y[:, 2 * w:3 * w]


def _rmul3(x, m):
    r = x.shape[0]
    y = _mm(jnp.concatenate(_split3(x), axis=0), m)
    return y[0:r] + y[r:2 * r] + y[2 * r:3 * r]


def _rms(x, g):
    return x * lax.rsqrt(jnp.mean(x * x, axis=-1, keepdims=True) + EPS) * g


def _iota(shape, dim):
    return lax.broadcasted_iota(i32, shape, dim)


def _tri(c, reverse):
    r, k = _iota((c, c), 0), _iota((c, c), 1)
    return jnp.where((k >= r) if reverse else (k <= r), 1.0, 0.0).astype(bf16)


def _block_avg(w, blk):
    same = (_iota((w, w), 0) // blk) == (_iota((w, w), 1) // blk)
    return jnp.where(same, 1.0 / blk, 0.0).astype(bf16)


def _rows(n, blk, fn):
    def body(i, c):
        fn(pl.multiple_of(i * blk, blk), i)
        return c
    lax.fori_loop(0, n // blk, body, 0)


def _head_rmsnorm(o, g, mavg):
    ms = _rmul3(o * o, mavg)
    return o * lax.rsqrt(ms + EPS) * g


def _proj_kernel(h_ref, g_ref, wa, wb, wc, wd, ba, bb, bc, bd, za, zb, zc, zd):
    u = _rms(h_ref[...], g_ref[...]).astype(bf16)
    for w, b, z in ((wa, ba, za), (wb, bb, zb), (wc, bc, zc), (wd, bd, zd)):
        z[...] = _mm(u, w[...]) + b[...]


def _proj(h, g, ws, bs, tm=512):
    t = h.shape[0]
    widths = (ZA_W, ZB_W, ZC_W, ZD_W)
    tok = lambda w: pl.BlockSpec((tm, w), lambda i: (i, 0))
    return pl.pallas_call(
        _proj_kernel,
        grid=(t // tm,),
        in_specs=[tok(D_MODEL), _const_spec((1, D_MODEL))]
        + [_const_spec((D_MODEL, w)) for w in widths] + [_const_spec((1, w)) for w in widths],
        out_specs=[tok(w) for w in widths],
        out_shape=[jax.ShapeDtypeStruct((t, w), f32) for w in widths],
        compiler_params=_cparams("parallel"),
        name="proj",
    )(h, g, *ws, *bs)


def _gl_bidir(loads, acc_refs, st_refs, *, s, c, dk):
    h, dv = HEADS, HEAD_W
    kw, vw, hc = h * dk, h * dv, h * c
    n, un = s // c, GL_UNROLL
    khead = _iota((c, kw), 1) // dk
    vhead = _iota((c, vw), 1) // dv
    sidx, trow = _iota((c, hc), 1) % c, _iota((c, hc), 0)
    bd = (_iota((vw, kw), 0) // dv) == (_iota((vw, kw), 1) // dk)
    tris = (_tri(c, False), _tri(c, True))
    causals = (sidx <= trow, sidx >= trow)
    for st_ref in st_refs:
        st_ref[...] = jnp.zeros((vw, kw), f32)

    streams = [(d, u) for u in range(un) for d in range(2)]

    def body(ci, carry):
        idx = [(ci * un + u) if d == 0 else (n - 1 - ci * un - u) for d, u in streams]
        sts = [pl.multiple_of(i * c, c) for i in idx]
        ins = [loads[d](st) for (d, _), st in zip(streams, sts)]
        qs, ks, vs = [x[0] for x in ins], [x[1] for x in ins], [x[2] for x in ins]
        bs = [_lmul3(tris[d], x[3]) for (d, _), x in zip(streams, ins)]
        lasts = [b[0:1] if d == 1 else b[c - 1:c] for (d, _), b in zip(streams, bs)]
        mids = [b[c // 2:c // 2 + 1] for b in bs]
        qts = [(q * jnp.exp(jnp.minimum(b - m, EXP_CLAMP))).astype(bf16) for q, b, m in zip(qs, bs, mids)]
        kts = [k * jnp.exp(jnp.minimum(m - b, EXP_CLAMP)) for k, b, m in zip(ks, bs, mids)]
        kexps = [jnp.concatenate([jnp.where(khead == j, kt, 0.0).astype(bf16) for j in range(h)], axis=0)
                 for kt in kts]
        vexps = [jnp.concatenate([jnp.where(vhead == j, v, 0.0).astype(bf16) for j in range(h)], axis=0)
                 for v in vs]
        atts = [jnp.where(causals[d], _mm_nt(qt, kexp), 0.0).astype(bf16)
                for (d, _), qt, kexp in zip(streams, qts, kexps)]
        os_ = [_mm(att, vexp) for att, vexp in zip(atts, vexps)]
        qis = [(q * jnp.exp(b)).astype(bf16) for q, b in zip(qs, bs)]
        kds = [(k * jnp.exp(l - b)).astype(bf16) for k, l, b in zip(ks, lasts, bs)]
        upds = [jnp.where(bd, _mm_tn(v.astype(bf16), kd), 0.0) for v, kd in zip(vs, kds)]
        decs = [jnp.exp(l) for l in lasts]
        states = [st_ref[...] for st_ref in st_refs]
        for i, (d, _) in enumerate(streams):
            acc_refs[d][pl.ds(sts[i], c), :] = os_[i] + _mm_nt(qis[i], states[d].astype(bf16))
            states[d] = states[d] * decs[i] + upds[i]
        for d in range(2):
            st_refs[d][...] = states[d]
        return carry

    lax.fori_loop(0, n // un, body, 0)


def _gla_kernel(z_ref, wlr_ref, blr_ref, g_ref, o_ref, gk_ref, accf_ref, accb_ref, stf_ref, stb_ref, *, s):
    rb = 256

    def gates(st, i):
        lr = z_ref[pl.ds(st, rb), 768:896].astype(bf16)
        x = _mm(lr, wlr_ref[...]) + blr_ref[...]
        gk_ref[pl.ds(st, rb), :] = jax.nn.log_sigmoid(x) * (1.0 / GLA_NORMALIZER)
    _rows(s, rb, gates)

    scale = GLA_DK ** -0.5

    def load(st, d):
        q = z_ref[pl.ds(st, GLA_CHUNK), 0:128] * scale
        k = z_ref[pl.ds(st, GLA_CHUNK), 128:256]
        v = z_ref[pl.ds(st, GLA_CHUNK), 256:512]
        g = gk_ref[pl.ds(st, GLA_CHUNK), d * 128:(d + 1) * 128]
        return q, k, v, g
    _gl_bidir([functools.partial(load, d=d) for d in range(2)], (accf_ref, accb_ref), (stf_ref, stb_ref),
              s=s, c=GLA_CHUNK, dk=GLA_DK)

    mavg = _block_avg(BRANCH_W, HEAD_W)

    def out(st, i):
        y = _head_rmsnorm(accf_ref[pl.ds(st, rb), :] + accb_ref[pl.ds(st, rb), :], g_ref[...], mavg)
        gate = z_ref[pl.ds(st, rb), 512:768]
        o_ref[pl.ds(st, rb), :] = y * (gate * jax.nn.sigmoid(gate))
    _rows(s, rb, out)


def _gla(za, wlr, blr, g, nb, s):
    return pl.pallas_call(
        functools.partial(_gla_kernel, s=s),
        grid=(nb,),
        in_specs=[pl.BlockSpec((None, s, ZA_W), lambda b: (b, 0, 0)),
                  _const_spec((LANES, 256)), _const_spec((1, 256)), _const_spec((1, BRANCH_W))],
        out_specs=pl.BlockSpec((None, s, BRANCH_W), lambda b: (b, 0, 0)),
        out_shape=jax.ShapeDtypeStruct((nb, s, BRANCH_W), f32),
        scratch_shapes=[pltpu.VMEM((s, 256), f32)] + [pltpu.VMEM((s, BRANCH_W), f32)] * 2
        + [pltpu.VMEM((BRANCH_W, HEADS * GLA_DK), f32)] * 2,
        compiler_params=_cparams("parallel"),
        name="gla",
    )(za.reshape(nb, s, ZA_W), wlr, blr, g)


def _hgrn_kernel(z_ref, lbp_ref, g_ref, o_ref, q_ref, lf_ref, kd_ref, accf_ref, accb_ref, stf_ref, stb_ref,
                 *, s, layer):
    rb = 256
    x = lbp_ref[...]
    e = jnp.exp(x - jnp.max(x, axis=0, keepdims=True))
    p = e / jnp.sum(e, axis=0, keepdims=True)
    lb = p[0:1]
    for j in range(1, layer + 1):
        lb = lb + p[j:j + 1]
    lb = lb - p[0:1]

    def prep(st, i):
        qz = z_ref[pl.ds(st, rb), 0:256]
        q_ref[pl.ds(st, rb), :] = qz * jax.nn.sigmoid(qz)
        fz = z_ref[pl.ds(st, rb), 256:768]
        lf_ref[pl.ds(st, rb), :] = jnp.log(lb + (1.0 - lb) * jax.nn.sigmoid(fz))
        kd_ref[pl.ds(st, rb), :] = (1.0 - lb) * jax.nn.sigmoid(-fz)
    _rows(s, rb, prep)

    kw = HEADS * HGRN_DF

    def load(st, d):
        q = q_ref[pl.ds(st, HGRN_CHUNK), :]
        k = kd_ref[pl.ds(st, HGRN_CHUNK), d * kw:(d + 1) * kw]
        v = z_ref[pl.ds(st, HGRN_CHUNK), 768:1024]
        g = lf_ref[pl.ds(st, HGRN_CHUNK), d * kw:(d + 1) * kw]
        return q, k, v, g
    _gl_bidir([functools.partial(load, d=d) for d in range(2)], (accf_ref, accb_ref), (stf_ref, stb_ref),
              s=s, c=HGRN_CHUNK, dk=HGRN_DF)

    mavg = _block_avg(BRANCH_W, HEAD_W)

    def out(st, i):
        y = _head_rmsnorm(accf_ref[pl.ds(st, rb), :] + accb_ref[pl.ds(st, rb), :], g_ref[...], mavg)
        gate = z_ref[pl.ds(st, rb), 1024:1280]
        o_ref[pl.ds(st, rb), :] = y * (gate * jax.nn.sigmoid(gate))
    _rows(s, rb, out)


def _hgrn(zd, lbp, g, nb, s, layer):
    kw = HEADS * HGRN_DF
    return pl.pallas_call(
        functools.partial(_hgrn_kernel, s=s, layer=layer),
        grid=(nb,),
        in_specs=[pl.BlockSpec((None, s, ZD_W), lambda b: (b, 0, 0)),
                  _const_spec((DEPTH, 2 * kw)), _const_spec((1, BRANCH_W))],
        out_specs=pl.BlockSpec((None, s, BRANCH_W), lambda b: (b, 0, 0)),
        out_shape=jax.ShapeDtypeStruct((nb, s, BRANCH_W), f32),
        scratch_shapes=[pltpu.VMEM((s, kw), f32), pltpu.VMEM((s, 2 * kw), f32), pltpu.VMEM((s, 2 * kw), f32)]
        + [pltpu.VMEM((s, BRANCH_W), f32)] * 2 + [pltpu.VMEM((BRANCH_W, kw), f32)] * 2,
        compiler_params=_cparams("parallel"),
        name="hgrn",
    )(zd.reshape(nb, s, ZD_W), lbp, g)


def _mlstm_bidir(z_ref, q_ref, k_ref, acc_refs, st_refs, nv_refs, m_refs, *, s):
    c, w = MLSTM_CHUNK, BRANCH_W
    n, un = s // c, MLSTM_UNROLL
    ones = jnp.ones((c, c), bf16)
    lane, row = _iota((c, w), 1), _iota((c, w), 0)
    head, sidx = lane // HEAD_W, lane % HEAD_W
    diag = sidx == row
    same = (_iota((w, w), 0) // HEAD_W) == (_iota((w, w), 1) // HEAD_W)
    bones = jnp.where(same, 1.0, 0.0).astype(bf16)
    tris = (_tri(c, False), _tri(c, True))
    causals = (sidx <= row, sidx >= row)
    for d in range(2):
        st_refs[d][...] = jnp.zeros((w, w), f32)
        nv_refs[d][...] = jnp.zeros((1, w), f32)
        m_refs[d][...] = jnp.zeros((1, w), f32)

    streams = [(d, u) for u in range(un) for d in range(2)]

    def seg_max(x):
        out = jnp.full((c, w), NEG_BIG, f32)
        for j in range(HEADS):
            mj = jnp.max(jnp.where(head == j, x, NEG_BIG), axis=-1, keepdims=True)
            out = jnp.where(head == j, mj, out)
        return out

    def body(ci, carry):
        idx = [(ci * un + u) if d == 0 else (n - 1 - ci * un - u) for d, u in streams]
        sts = [pl.multiple_of(i * c, c) for i in idx]
        qs = [q_ref[pl.ds(st, c), :] for st in sts]
        ks = [k_ref[pl.ds(st, c), :] for st in sts]
        vs = [z_ref[pl.ds(st, c), 512:768] for st in sts]
        igs = [z_ref[pl.ds(st, c), 1024 + d * w:1024 + (d + 1) * w] for (d, _), st in zip(streams, sts)]
        lfs = [jax.nn.log_sigmoid(z_ref[pl.ds(st, c), 1536 + d * w:1536 + (d + 1) * w])
               for (d, _), st in zip(streams, sts)]
        bs = [_lmul3(tris[d], lf) for (d, _), lf in zip(streams, lfs)]
        rs = [_lmul3(ones, jnp.where(diag, ig - b, 0.0)) for ig, b in zip(igs, bs)]
        log_ds = [jnp.where(causals[d], b + r, NEG_BIG) for (d, _), b, r in zip(streams, bs, rs)]
        mrows = [seg_max(x) for x in log_ds]
        lasts = [b[0:1] if d == 1 else b[c - 1:c] for (d, _), b in zip(streams, bs)]
        log_ws = [l - b + ig for l, b, ig in zip(lasts, bs, igs)]
        mws = [jnp.max(x, axis=0, keepdims=True) for x in log_ws]
        qbs = [q.astype(bf16) for q in qs]
        kexps = [jnp.concatenate([jnp.where(head == j, k, 0.0).astype(bf16) for j in range(HEADS)], axis=0)
                 for k in ks]
        vexps = [jnp.concatenate([jnp.where(head == j, v, 0.0).astype(bf16) for j in range(HEADS)], axis=0)
                 for v in vs]
        qks = [_mm_nt(qb, kexp) for qb, kexp in zip(qbs, kexps)]
        ms = [m_ref[...] for m_ref in m_refs]
        m_ts, w_prevs, w_ss, decs = [], [], [], []
        for i, (d, _) in enumerate(streams):
            log_prev = bs[i] + ms[d]
            m_t = jnp.maximum(mrows[i], log_prev)
            m_new = jnp.maximum(lasts[i] + ms[d], mws[i])
            m_ts.append(m_t)
            w_prevs.append(jnp.exp(log_prev - m_t))
            w_ss.append(jnp.exp(log_ws[i] - m_new))
            decs.append(jnp.exp(lasts[i] + ms[d] - m_new))
            ms[d] = m_new
        scs = [qk * jnp.exp(log_d - m_t) for qk, log_d, m_t in zip(qks, log_ds, m_ts)]
        nums = [_mm(sc.astype(bf16), vexp) for sc, vexp in zip(scs, vexps)]
        dens = [_rmul3(sc, bones) for sc in scs]
        upds = [jnp.where(same, _mm_tn((v * w_s).astype(bf16), k.astype(bf16)), 0.0)
                for v, w_s, k in zip(vs, w_ss, ks)]
        nvus = [jnp.sum(w_s * k, axis=0, keepdims=True) for w_s, k in zip(w_ss, ks)]
        states = [st_ref[...] for st_ref in st_refs]
        nvs = [nv_ref[...] for nv_ref in nv_refs]
        for i, (d, _) in enumerate(streams):
            num = w_prevs[i] * _mm_nt(qbs[i], states[d].astype(bf16)) + nums[i]
            den = w_prevs[i] * _rmul3(qs[i] * nvs[d], bones) + dens[i]
            acc_refs[d][pl.ds(sts[i], c), :] = num / jnp.maximum(jnp.abs(den), jnp.exp(-m_ts[i]))
            states[d] = decs[i] * states[d] + upds[i]
            nvs[d] = decs[i] * nvs[d] + nvus[i]
        for d in range(2):
            st_refs[d][...] = states[d]
            nv_refs[d][...] = nvs[d]
            m_refs[d][...] = ms[d]
        return carry

    lax.fori_loop(0, n // un, body, 0)


def _mlstm_kernel(z_ref, cw_ref, cb_ref, g_ref, o_ref, q_ref, k_ref, accf_ref, accb_ref, stf_ref, stb_ref,
                  nvf_ref, nvb_ref, mf_ref, mb_ref, *, s):
    rb = 256
    nblk = s // rb
    w = BRANCH_W

    def conv(st, i):
        x = z_ref[pl.ds(st, rb), 0:2 * w]
        pst = pl.multiple_of(jnp.maximum(st - 8, 0), 8)
        nst = pl.multiple_of(jnp.minimum(st + rb, s - 8), 8)
        prev = z_ref[pl.ds(pst, 8), 0:2 * w][7:8]
        nxt = z_ref[pl.ds(nst, 8), 0:2 * w][0:1]
        prev = jnp.where(i == 0, 0.0, prev)
        nxt = jnp.where(i == nblk - 1, 0.0, nxt)
        row = _iota((rb, 2 * w), 0)
        xm = jnp.where(row == 0, prev, pltpu.roll(x, 1, 0))
        xp = jnp.where(row == rb - 1, nxt, pltpu.roll(x, rb - 1, 0))
        y = cb_ref[...] + xm * cw_ref[0:1, :]
        y = y + x * cw_ref[1:2, :]
        y = y + xp * cw_ref[2:3, :]
        y = y * jax.nn.sigmoid(y)
        q_ref[pl.ds(st, rb), :] = y[:, 0:w]
        k_ref[pl.ds(st, rb), :] = y[:, w:2 * w] * (HEAD_W ** -0.5)
    _rows(s, rb, conv)

    _mlstm_bidir(z_ref, q_ref, k_ref, (accf_ref, accb_ref), (stf_ref, stb_ref), (nvf_ref, nvb_ref),
                 (mf_ref, mb_ref), s=s)

    mavg = _block_avg(w, HEAD_W)

    def out(st, i):
        y = _head_rmsnorm(accf_ref[pl.ds(st, rb), :] + accb_ref[pl.ds(st, rb), :], g_ref[...], mavg)
        o_ref[pl.ds(st, rb), :] = jax.nn.sigmoid(z_ref[pl.ds(st, rb), 768:1024]) * y
    _rows(s, rb, out)


def _mlstm(zb, cw, cb, g, nb, s):
    w = BRANCH_W
    return pl.pallas_call(
        functools.partial(_mlstm_kernel, s=s),
        grid=(nb,),
        in_specs=[pl.BlockSpec((None, s, ZB_W), lambda b: (b, 0, 0)),
                  _const_spec((3, 2 * w)), _const_spec((1, 2 * w)), _const_spec((1, w))],
        out_specs=pl.BlockSpec((None, s, w), lambda b: (b, 0, 0)),
        out_shape=jax.ShapeDtypeStruct((nb, s, w), f32),
        scratch_shapes=[pltpu.VMEM((s, w), f32)] * 4 + [pltpu.VMEM((w, w), f32)] * 2 + [pltpu.VMEM((1, w), f32)] * 4,
        compiler_params=_cparams("parallel"),
        name="mlstm",
    )(zb.reshape(nb, s, ZB_W), cw, cb, g)


def _rope(x, cos, sa, sb):
    w = x.shape[-1]
    half = HEAD_W // 2
    return x * cos + pltpu.roll(x, w - half, 1) * sa + pltpu.roll(x, half, 1) * sb


def _dil_kernel(z_ref, cos_ref, sa_ref, sb_ref, o_ref, kr_ref, vm_ref, *, s, tq):
    w = BRANCH_W
    qi = pl.program_id(1)
    rb = 256

    @pl.when(qi == 0)
    def _():
        head = _iota((rb, w), 1) // HEAD_W

        def prep(st, i):
            x = z_ref[pl.ds(st, rb), w:2 * w]
            kr_ref[pl.ds(st, rb), :] = _rope(x, cos_ref[pl.ds(st, rb), :], sa_ref[pl.ds(st, rb), :],
                                             sb_ref[pl.ds(st, rb), :]).astype(bf16)
            v = z_ref[pl.ds(st, rb), 2 * w:3 * w]
            for j in range(HEADS):
                vm_ref[j, pl.ds(st, rb), :] = jnp.where(head == j, v, 0.0).astype(bf16)
        _rows(s, rb, prep)

    q0 = pl.multiple_of(qi * tq, tq)
    qr = _rope(z_ref[pl.ds(q0, tq), 0:w], cos_ref[pl.ds(q0, tq), :], sa_ref[pl.ds(q0, tq), :],
               sb_ref[pl.ds(q0, tq), :])
    rel = _iota((tq, s), 1) - (_iota((tq, s), 0) + q0)
    dist = jnp.abs(rel)
    cnt = jnp.zeros((tq, s), f32)
    for window, dil in DIL_PATTERNS:
        reach = (window // (2 * dil)) * dil
        ok = dist <= reach
        if dil > 1:
            ok = jnp.logical_and(ok, (rel & (dil - 1)) == 0)
        cnt = cnt + jnp.where(ok, 1.0, 0.0)
    valid = cnt > 0.0
    qhead = _iota((tq, w), 1) // HEAD_W
    acc = jnp.zeros((tq, w), f32)
    for j in range(HEADS):
        qh = jnp.where(qhead == j, qr, 0.0).astype(bf16)
        sc = _mm_nt(qh, kr_ref[...]) * (HEAD_W ** -0.5)
        sc = jnp.where(valid, sc, NEG_BIG)
        m = jnp.max(sc, axis=-1, keepdims=True)
        p = jnp.exp(sc - m) * cnt
        l = jnp.sum(p, axis=-1, keepdims=True)
        acc = acc + _mm(p.astype(bf16), vm_ref[j]) / l
    o_ref[...] = acc


def _dil(zc, cos, sa, sb, nb, s, tq=256):
    w = BRANCH_W
    for _, dil in DIL_PATTERNS:
        assert dil & (dil - 1) == 0 and s % dil == 0
    return pl.pallas_call(
        functools.partial(_dil_kernel, s=s, tq=tq),
        grid=(nb, s // tq),
        in_specs=[pl.BlockSpec((None, s, ZC_W), lambda b, i: (b, 0, 0)),
                  _const_spec((s, w)), _const_spec((s, w)), _const_spec((s, w))],
        out_specs=pl.BlockSpec((None, tq, w), lambda b, i: (b, i, 0)),
        out_shape=jax.ShapeDtypeStruct((nb, s, w), f32),
        scratch_shapes=[pltpu.VMEM((s, w), bf16), pltpu.VMEM((HEADS, s, w), bf16)],
        compiler_params=_cparams("parallel", "arbitrary"),
        name="dilattn",
    )(zc.reshape(nb, s, ZC_W), cos, sa, sb)


def _merge_kernel(h_ref, ya, yb, yc, yd, g_ref, wg_ref, wbr_ref, wo_ref, o_ref):
    x = h_ref[...]
    u = _rms(x, g_ref[...]).astype(bf16)
    merged = None
    for n, y in enumerate((ya, yb, yc, yd)):
        gate = jax.nn.sigmoid(_mm(u, wg_ref[n]))
        term = gate * _mm(y[...].astype(bf16), wbr_ref[n])
        merged = term if merged is None else merged + term
    o_ref[...] = x + _mm(merged.astype(bf16), wo_ref[...])


def _merge(h, ys, g, wg, wbr, wo, tm=512):
    t = h.shape[0]
    tok = lambda w: pl.BlockSpec((tm, w), lambda i: (i, 0))
    return pl.pallas_call(
        _merge_kernel,
        grid=(t // tm,),
        in_specs=[tok(D_MODEL)] + [tok(BRANCH_W)] * 4
        + [_const_spec((1, D_MODEL)), _const_spec((4, D_MODEL, D_MODEL)),
           _const_spec((4, BRANCH_W, D_MODEL)), _const_spec((D_MODEL, D_MODEL))],
        out_specs=tok(D_MODEL),
        out_shape=jax.ShapeDtypeStruct((t, D_MODEL), f32),
        compiler_params=_cparams("parallel"),
        name="merge",
    )(h, *ys, g, wg, wbr, wo)


def _stair_cells():
    return [(a, b) for a in range(PEER_TOPK) for b in range(PEER_TOPK) if (a + 1) * (b + 1) <= PEER_TOPK]


_CELLS = _stair_cells()
_NCELL = len(_CELLS)
_NCELL_PAD = -(-_NCELL // 8) * 8


def _extract_top(work, rowf, nrows, exact):
    tops = []
    rank = jnp.full(work.shape, float(PEER_TOPK), f32)
    for r in range(PEER_TOPK):
        mx = jnp.max(work, axis=0, keepdims=True)
        hit = work == mx
        if exact:
            cand = jnp.where(hit, rowf, float(nrows))
            hit = cand == jnp.min(cand, axis=0, keepdims=True)
        work = jnp.where(hit, -jnp.inf, work)
        rank = jnp.where(hit, float(r), rank)
        tops.append(mx)
    removed = jnp.sum(jnp.where(rank < float(PEER_TOPK), 1.0, 0.0), axis=0, keepdims=True)
    return tops, rank, removed


def _route_kernel(h_ref, g_ref, wpq_ref, keys_ref, sa_ref, u_ref, a_ref, cnt_ref, rk_ref, bn_ref,
                  q_ref, m_ref, *, tm):
    u = _rms(h_ref[...], g_ref[...]).astype(bf16)
    u_ref[...] = u
    q_ref[...] = _mm(u, wpq_ref[...]).astype(bf16)
    nk = PEER_NKEYS
    rowf = _iota((nk, tm), 0).astype(f32)
    rowcf = _iota((_NCELL_PAD, tm), 0).astype(f32)
    topk = float(PEER_TOPK)

    def head_tables(hd, exact):
        scs, tops, ranks = [], [], []
        tied = jnp.zeros((1, tm), f32)
        for p in range(2):
            idx = hd * 2 + p
            qs = q_ref[:, pl.ds(pl.multiple_of(idx * nk, nk), nk)]
            sc = _mm_nt(keys_ref[idx], qs)
            top, rank, removed = _extract_top(sc, rowf, nk, exact)
            tied = jnp.where(removed == topk, tied, 1.0)
            scs.append(sc)
            tops.append(top)
            ranks.append(rank)
        m_ref[...] = jnp.full((_NCELL_PAD, tm), -jnp.inf, f32)
        for ci, (a, b) in enumerate(_CELLS):
            m_ref[ci:ci + 1, :] = tops[0][a] + tops[1][b]
        cand = m_ref[...]
        _, crank, removed = _extract_top(cand, rowcf, _NCELL_PAD, exact)
        tied = jnp.where(removed == topk, tied, 1.0)
        chosen = crank < topk
        wgt = jnp.where(chosen, jnp.exp(cand - cand[0:1]), 0.0)
        zsum = jnp.sum(wgt, axis=0, keepdims=True)
        rowcnt = _mm(sa_ref[...], jnp.where(chosen, 1.0, 0.0).astype(bf16))
        cnt = jnp.zeros((nk, tm), f32)
        for a in range(PEER_TOPK):
            cnt = jnp.where(ranks[0] == float(a), rowcnt[a:a + 1], cnt)
        rows = pl.ds(pl.multiple_of(hd * nk, nk), nk)
        amat = jnp.exp(scs[0] - tops[0][0])
        prow = pl.ds(pl.multiple_of(hd * (nk // 2), nk // 2), nk // 2)
        rkp = pltpu.bitcast(ranks[1].astype(bf16), jnp.uint32)
        bnp = pltpu.bitcast((jnp.exp(scs[1] - tops[1][0]) / zsum).astype(bf16), jnp.uint32)
        for tb in range(tm // LANES):
            lanes = slice(tb * LANES, (tb + 1) * LANES)
            a_ref[tb, rows, :] = amat[:, lanes]
            cnt_ref[tb, rows, :] = cnt[:, lanes]
            rk_ref[tb, prow, :] = rkp[:, lanes]
            bn_ref[tb, prow, :] = bnp[:, lanes]
        return tied

    def per_head(hd, carry):
        tied = head_tables(hd, False)

        @pl.when(jnp.max(tied) > 0.0)
        def _():
            head_tables(hd, True)
        return carry

    lax.fori_loop(0, PEER_HEADS, per_head, 0)


def _route(h, g, wpq, keys, sa, tm=256):
    t = h.shape[0]
    nh, nk = PEER_HEADS, PEER_NKEYS
    tab = pl.BlockSpec((tm // LANES, nh * nk, LANES), lambda i: (i, 0, 0))
    ptab = pl.BlockSpec((tm // LANES, nh * nk // 2, LANES), lambda i: (i, 0, 0))
    return pl.pallas_call(
        functools.partial(_route_kernel, tm=tm),
        grid=(t // tm,),
        in_specs=[pl.BlockSpec((tm, D_MODEL), lambda i: (i, 0)), _const_spec((1, D_MODEL)),
                  _const_spec((D_MODEL, 2 * nh * nk)), _const_spec((2 * nh, nk, nk)),
                  _const_spec((PEER_TOPK, _NCELL_PAD))],
        out_specs=[pl.BlockSpec((tm, D_MODEL), lambda i: (i, 0)), tab, tab, ptab, ptab],
        out_shape=[jax.ShapeDtypeStruct((t, D_MODEL), bf16)] + [jax.ShapeDtypeStruct((t // LANES, nh * nk, LANES), f32)] * 2
        + [jax.ShapeDtypeStruct((t // LANES, nh * nk // 2, LANES), jnp.uint32)] * 2,
        scratch_shapes=[pltpu.VMEM((tm, 2 * nh * nk), bf16), pltpu.VMEM((_NCELL_PAD, tm), f32)],
        compiler_params=_cparams("parallel"),
        name="route",
    )(h, g, wpq, keys, sa)


def _gelu_bf16(x):
    return (0.5 * x * (1.0 + lax.erf(x * math.sqrt(0.5)))).astype(bf16)


def _peer_gates(act_ref, p_ref, tabs, i0, tp):
    a_ref, cnt_ref, rk_ref, bn_ref = tabs
    nk = PEER_NKEYS
    for ii in range(PEER_SUB // nk):
        for tb in range(tp // LANES):
            lanes = slice(tb * LANES, (tb + 1) * LANES)
            gate = None
            for hd in range(PEER_HEADS):
                row = pl.ds(hd * nk + i0 + ii, 1)
                a = jnp.broadcast_to(a_ref[tb, row, :], (nk, LANES)).astype(bf16)
                c = jnp.broadcast_to(cnt_ref[tb, row, :], (nk, LANES)).astype(bf16)
                prow = slice(hd * (nk // 2), (hd + 1) * (nk // 2))
                rk = pltpu.bitcast(rk_ref[tb, prow, :], bf16)
                bn = pltpu.bitcast(bn_ref[tb, prow, :], bf16)
                term = a * jnp.where(rk < c, bn, jnp.zeros((), bf16))
                gate = term if gate is None else gate + term
            p_ref[ii * nk:(ii + 1) * nk, lanes] = gate * act_ref[ii * nk:(ii + 1) * nk, lanes]


def _peer_kernel(u_ref, a_ref, cnt_ref, rk_ref, bn_ref, pu_ref, pvt_ref, o_ref, acc_ref, act0, act1, p0, p1,
                 *, ne, tp):
    g = pl.program_id(0)
    nblocks = pl.num_programs(0) - 2
    b1 = jnp.clip(g - 1, 0, nblocks - 1)
    b2 = jnp.clip(g - 2, 0, nblocks - 1)
    i1 = (b1 % ne) * (PEER_SUB // PEER_NKEYS)
    tabs = (a_ref, cnt_ref, rk_ref, bn_ref)

    @pl.when(g == 0)
    def _():
        for ref in (act0, act1, p0, p1):
            ref[...] = jnp.zeros(ref.shape, bf16)

    @pl.when(b2 % ne == 0)
    def _():
        acc_ref[...] = jnp.zeros(acc_ref.shape, f32)

    def stage(act_w, act_r, p_w, p_r):
        _peer_gates(act_r, p_w, tabs, i1, tp)
        act_w[...] = _gelu_bf16(_mm_nt(pu_ref[...], u_ref[...]))
        acc_ref[...] += _mm(pvt_ref[...], p_r[...])

    @pl.when(g % 2 == 0)
    def _():
        stage(act0, act1, p1, p0)

    @pl.when(g % 2 == 1)
    def _():
        stage(act1, act0, p0, p1)

    @pl.when(jnp.logical_and(g >= 2, b2 % ne == ne - 1))
    def _():
        o_ref[...] = acc_ref[...].T


def _peer(u2, tabs, pu, pvt, tp=512):
    t = u2.shape[0]
    nh, nk = PEER_HEADS, PEER_NKEYS
    ne = pu.shape[0] // PEER_SUB
    nblocks = (t // tp) * ne
    blk = lambda lag: (lambda g: jnp.clip(g - lag, 0, nblocks - 1))
    b0, b1, b2 = blk(0), blk(1), blk(2)
    tab = pl.BlockSpec((tp // LANES, nh * nk, LANES), lambda g: (b1(g) // ne, 0, 0))
    ptab = pl.BlockSpec((tp // LANES, nh * nk // 2, LANES), lambda g: (b1(g) // ne, 0, 0))
    return pl.pallas_call(
        functools.partial(_peer_kernel, ne=ne, tp=tp),
        grid=(nblocks + 2,),
        in_specs=[pl.BlockSpec((tp, D_MODEL), lambda g: (b0(g) // ne, 0)), tab, tab, ptab, ptab,
                  pl.BlockSpec((PEER_SUB, D_MODEL), lambda g: (b0(g) % ne, 0)),
                  pl.BlockSpec((D_MODEL, PEER_SUB), lambda g: (0, b2(g) % ne))],
        out_specs=pl.BlockSpec((tp, D_MODEL), lambda g: (b2(g) // ne, 0)),
        out_shape=jax.ShapeDtypeStruct((t, D_MODEL), f32),
        scratch_shapes=[pltpu.VMEM((D_MODEL, tp), f32)] + [pltpu.VMEM((PEER_SUB, tp), bf16)] * 4,
        compiler_params=_cparams("arbitrary"),
        name="peer",
    )(u2, *tabs, pu, pvt)


def _pe_kernel(h_ref, po_ref, p_ref, g_ref, wpg_ref, wpe_ref, gf_ref, o_ref, *, final):
    h = h_ref[...] + po_ref[...]
    gate = jax.nn.sigmoid(_mm(_rms(h, g_ref[...]).astype(bf16), wpg_ref[...]))
    h = h + gate * _mm(p_ref[...].astype(bf16), wpe_ref[...])
    if final:
        h = _rms(h, gf_ref[...])
    o_ref[...] = h


def _pe(h, po, p, g, wpg, wpe, gf, final, tm=512):
    t = h.shape[0]
    tok = lambda w: pl.BlockSpec((tm, w), lambda i: (i, 0))
    return pl.pallas_call(
        functools.partial(_pe_kernel, final=final),
        grid=(t // tm,),
        in_specs=[tok(D_MODEL), tok(D_MODEL), tok(PE_DIM), _const_spec((1, D_MODEL)),
                  _const_spec((D_MODEL, D_MODEL)), _const_spec((PE_DIM, D_MODEL)), _const_spec((1, D_MODEL))],
        out_specs=tok(D_MODEL),
        out_shape=jax.ShapeDtypeStruct((t, D_MODEL), f32),
        compiler_params=_cparams("parallel"),
        name="pe_embed",
    )(h, po, p, g, wpg, wpe, gf)


def _cols(w, name):
    a, b = _PROJ_OFF[name]
    return w[..., a:b]


def _proj_groups(w):
    pad = jnp.zeros(w.shape[:-1] + (ZA_W - 800,), w.dtype)
    ga = jnp.concatenate([_cols(w, n) for n in ('a_q', 'a_k', 'a_v', 'a_g', 'a_lr')] + [pad], axis=-1)
    gates = [jnp.repeat(_cols(w, n), HEAD_W, axis=-1) for n in ('b_i', 'b_f')]
    gb = jnp.concatenate([_cols(w, n) for n in ('b_q', 'b_k', 'b_v', 'b_o')] + gates, axis=-1)
    gc = jnp.concatenate([_cols(w, n) for n in ('c_q', 'c_k', 'c_v')], axis=-1)
    gd = jnp.concatenate([_cols(w, n) for n in ('d_q', 'd_f', 'd_i', 'd_g')], axis=-1)
    return ga, gb, gc, gd


def _rope_tables(s):
    half = HEAD_W // 2
    inv = ROPE_THETA ** (-jnp.arange(half, dtype=f32) / half)
    ang = jnp.arange(s, dtype=f32)[:, None] * inv[None, :]
    cos, sin = jnp.cos(ang), jnp.sin(ang)
    zero = jnp.zeros_like(sin)
    tile = lambda a, b: jnp.tile(jnp.concatenate([a, b], axis=-1), (1, HEADS))
    return tile(cos, cos), tile(-sin, zero), tile(zero, sin)


def _row_count_matrix():
    m = np.zeros((PEER_TOPK, _NCELL_PAD), np.float32)
    for ci, (a, _) in enumerate(_CELLS):
        m[a, ci] = 1.0
    return jnp.asarray(m, bf16)


def _prepare(g_mix, w_in, b_in, w_lr2, b_lr2, conv_w, conv_b, lb_param, g_a, g_b, g_d, w_gate, w_br, w_out,
             g_ffn, w_pq, sub_keys, peer_u, peer_v, g_pe, w_pe, w_pg, g_final):
    layers = []
    row = lambda a: a.reshape(1, -1)
    for l in range(DEPTH):
        wlr = jnp.zeros((LANES, 256), f32)
        for n in range(2):
            wlr = wlr.at[n * GLA_RANK:(n + 1) * GLA_RANK, n * 128:(n + 1) * 128].set(w_lr2[l, n])
        layers.append(dict(
            g_mix=row(g_mix[l]),
            w_proj=[w.astype(bf16) for w in _proj_groups(w_in[l])],
            b_proj=list(_proj_groups(row(b_in[l]))),
            wlr=wlr.astype(bf16), blr=row(b_lr2[l]),
            conv_w=conv_w[l], conv_b=row(conv_b[l]),
            g_a=row(g_a[l]), g_b=row(g_b[l]), g_d=row(g_d[l]),
            w_gate=w_gate[l].astype(bf16), w_br=w_br[l].astype(bf16), w_out=w_out[l].astype(bf16),
            g_ffn=row(g_ffn[l]), w_pq=w_pq[l].astype(bf16),
            keys=sub_keys[l].reshape(2 * PEER_HEADS, PEER_NKEYS, -1).astype(bf16),
            peer_u=peer_u[l].astype(bf16), peer_vt=peer_v[l].astype(bf16).T,
            g_pe=row(g_pe[l]), w_pe=w_pe[l].astype(bf16), w_pg=w_pg[l].astype(bf16),
        ))
    shared = dict(
        lbp=jnp.transpose(lb_param, (1, 0, 2)).reshape(DEPTH, -1),
        g_final=row(g_final), sa=_row_count_matrix(),
    )
    return layers, shared


def _trunk(x, p, layers, shared):
    nb, s, d = x.shape
    t = nb * s
    h = x.reshape(t, d)
    cos, sa, sb = _rope_tables(s)
    for l, w in enumerate(layers):
        za, zb, zc, zd = _proj(h, w['g_mix'], w['w_proj'], w['b_proj'])
        ya = _gla(za, w['wlr'], w['blr'], w['g_a'], nb, s)
        yb = _mlstm(zb, w['conv_w'], w['conv_b'], w['g_b'], nb, s)
        yc = _dil(zc, cos, sa, sb, nb, s)
        yd = _hgrn(zd, shared['lbp'], w['g_d'], nb, s, l)
        ys = [y.reshape(t, BRANCH_W) for y in (ya, yb, yc, yd)]
        h1 = _merge(h, ys, w['g_mix'], w['w_gate'], w['w_br'], w['w_out'])
        u2, *tabs = _route(h1, w['g_ffn'], w['w_pq'], w['keys'], shared['sa'])
        po = _peer(u2, tabs, w['peer_u'], w['peer_vt'])
        h = _pe(h1, po, p[l].reshape(t, PE_DIM), w['g_pe'], w['w_pg'], w['w_pe'], shared['g_final'],
                final=(l == DEPTH - 1))
    return h.reshape(nb, s, d)


def kernel(x_prompt, x_sample, p_prompt, p_sample, g_mix, w_in, b_in, w_lr2, b_lr2, conv_w, conv_b, lb_param, g_a, g_b, g_d, w_gate, w_br, w_out, g_ffn, w_pq, sub_keys, peer_u, peer_v, g_pe, w_pe, w_pg, g_final):
    layers, shared = _prepare(g_mix, w_in, b_in, w_lr2, b_lr2, conv_w, conv_b, lb_param, g_a, g_b, g_d, w_gate,
                              w_br, w_out, g_ffn, w_pq, sub_keys, peer_u, peer_v, g_pe, w_pe, w_pg, g_final)
    return (_trunk(x_prompt, p_prompt, layers, shared), _trunk(x_sample, p_sample, layers, shared))
```

```python
import functools
import math

import numpy as np
import jax
import jax.numpy as jnp
from jax import lax
from jax.experimental import pallas as pl
from jax.experimental.pallas import tpu as pltpu

f32 = jnp.float32
bf16 = jnp.bfloat16
i32 = jnp.int32

D_MODEL = 1024
DEPTH = 2
PE_DIM = 256
EPS = 1e-6
NEG_BIG = -1e30
HEADS = 4
HEAD_W = 64
BRANCH_W = HEADS * HEAD_W
GLA_DK = 32
GLA_RANK = 16
GLA_NORMALIZER = 16.0
GLA_CHUNK = 32
MLSTM_CHUNK = 64
HGRN_DF = 64
HGRN_CHUNK = 32
DIL_PATTERNS = ((128, 1), (512, 4), (2048, 16))
ROPE_THETA = 10000.0
PEER_HEADS = 8
PEER_NKEYS = 128
PEER_TOPK = 16
LANES = 128
PEER_SUB = 1024
MXU_TILE = 256
EXP_CLAMP = 80.0
GL_UNROLL = 4
MLSTM_UNROLL = 2

_PROJ_LAYOUT = (
    ('a_q', 128), ('a_k', 128), ('a_v', 256), ('a_g', 256), ('a_lr', 32),
    ('b_q', 256), ('b_k', 256), ('b_v', 256), ('b_i', 8), ('b_f', 8), ('b_o', 256),
    ('c_q', 256), ('c_k', 256), ('c_v', 256),
    ('d_q', 256), ('d_f', 512), ('d_i', 256), ('d_g', 256),
)
_PROJ_OFF = {}
_o = 0
for _n, _s in _PROJ_LAYOUT:
    _PROJ_OFF[_n] = (_o, _o + _s)
    _o += _s

ZA_W = 896
ZB_W = 2048
ZC_W = 768
ZD_W = 1280

VMEM_LIMIT = 56 * 1024 * 1024


def _cparams(*sem):
    return pltpu.CompilerParams(dimension_semantics=sem, vmem_limit_bytes=VMEM_LIMIT)


def _const_spec(shape):
    nd = len(shape)
    return pl.BlockSpec(shape, lambda *_: (0,) * nd, pipeline_mode=pl.Buffered(1))


def _mm(a, b):
    return jnp.dot(a, b, preferred_element_type=f32)


def _mm_nt(a, b):
    return lax.dot_general(a, b, (((1,), (1,)), ((), ())), preferred_element_type=f32)


def _mm_tn(a, b):
    return lax.dot_general(a, b, (((0,), (0,)), ((), ())), preferred_element_type=f32)


def _split3(x):
    hi = x.astype(bf16)
    r1 = x - hi.astype(f32)
    mid = r1.astype(bf16)
    lo = (r1 - mid.astype(f32)).astype(bf16)
    return hi, mid, lo


def _lmul3(m, x):
    w = x.shape[1]
    y = _mm(m, jnp.concatenate(_split3(x), axis=1))
    return y[:, 0:w] + y[:, w:2 * w] + y[:, 2 * w:3 * w]


def _rmul3(x, m):
    r = x.shape[0]
    y = _mm(jnp.concatenate(_split3(x), axis=0), m)
    return y[0:r] + y[r:2 * r] + y[2 * r:3 * r]


def _rms(x, g):
    return x * lax.rsqrt(jnp.mean(x * x, axis=-1, keepdims=True) + EPS) * g


def _iota(shape, dim):
    return lax.broadcasted_iota(i32, shape, dim)


def _tri(c, reverse):
    r, k = _iota((c, c), 0), _iota((c, c), 1)
    return jnp.where((k >= r) if reverse else (k <= r), 1.0, 0.0).astype(bf16)


def _block_avg(w, blk):
    same = (_iota((w, w), 0) // blk) == (_iota((w, w), 1) // blk)
    return jnp.where(same, 1.0 / blk, 0.0).astype(bf16)


def _rows(n, blk, fn):
    def body(i, c):
        fn(pl.multiple_of(i * blk, blk), i)
        return c
    lax.fori_loop(0, n // blk, body, 0)


def _head_rmsnorm(o, g, mavg):
    ms = _rmul3(o * o, mavg)
    return o * lax.rsqrt(ms + EPS) * g


def _proj_kernel(h_ref, g_ref, wa, wb, wc, wd, ba, bb, bc, bd, za, zb, zc, zd):
    u = _rms(h_ref[...], g_ref[...]).astype(bf16)
    for w, b, z in ((wa, ba, za), (wb, bb, zb), (wc, bc, zc), (wd, bd, zd)):
        z[...] = _mm(u, w[...]) + b[...]


def _proj(h, g, ws, bs, tm=512):
    t = h.shape[0]
    widths = (ZA_W, ZB_W, ZC_W, ZD_W)
    tok = lambda w: pl.BlockSpec((tm, w), lambda i: (i, 0))
    return pl.pallas_call(
        _proj_kernel,
        grid=(t // tm,),
        in_specs=[tok(D_MODEL), _const_spec((1, D_MODEL))]
        + [_const_spec((D_MODEL, w)) for w in widths] + [_const_spec((1, w)) for w in widths],
        out_specs=[tok(w) for w in widths],
        out_shape=[jax.ShapeDtypeStruct((t, w), f32) for w in widths],
        compiler_params=_cparams("parallel"),
        name="proj",
    )(h, g, *ws, *bs)


def _gl_bidir(loads, acc_refs, st_refs, *, s, c, dk):
    h, dv = HEADS, HEAD_W
    kw, vw, hc = h * dk, h * dv, h * c
    n, un = s // c, GL_UNROLL
    khead = _iota((c, kw), 1) // dk
    vhead = _iota((c, vw), 1) // dv
    sidx, trow = _iota((c, hc), 1) % c, _iota((c, hc), 0)
    bd = (_iota((vw, kw), 0) // dv) == (_iota((vw, kw), 1) // dk)
    tris = (_tri(c, False), _tri(c, True))
    causals = (sidx <= trow, sidx >= trow)
    for st_ref in st_refs:
        st_ref[...] = jnp.zeros((vw, kw), f32)

    streams = [(d, u) for u in range(un) for d in range(2)]

    def body(ci, carry):
        idx = [(ci * un + u) if d == 0 else (n - 1 - ci * un - u) for d, u in streams]
        sts = [pl.multiple_of(i * c, c) for i in idx]
        ins = [loads[d](st) for (d, _), st in zip(streams, sts)]
        qs, ks, vs = [x[0] for x in ins], [x[1] for x in ins], [x[2] for x in ins]
        bs = [_lmul3(tris[d], x[3]) for (d, _), x in zip(streams, ins)]
        lasts = [b[0:1] if d == 1 else b[c - 1:c] for (d, _), b in zip(streams, bs)]
        mids = [b[c // 2:c // 2 + 1] for b in bs]
        qts = [(q * jnp.exp(jnp.minimum(b - m, EXP_CLAMP))).astype(bf16) for q, b, m in zip(qs, bs, mids)]
        kts = [k * jnp.exp(jnp.minimum(m - b, EXP_CLAMP)) for k, b, m in zip(ks, bs, mids)]
        kexps = [jnp.concatenate([jnp.where(khead == j, kt, 0.0).astype(bf16) for j in range(h)], axis=0)
                 for kt in kts]
        vexps = [jnp.concatenate([jnp.where(vhead == j, v, 0.0).astype(bf16) for j in range(h)], axis=0)
                 for v in vs]
        atts = [jnp.where(causals[d], _mm_nt(qt, kexp), 0.0).astype(bf16)
                for (d, _), qt, kexp in zip(streams, qts, kexps)]
        os_ = [_mm(att, vexp) for att, vexp in zip(atts, vexps)]
        qis = [(q * jnp.exp(b)).astype(bf16) for q, b in zip(qs, bs)]
        kds = [(k * jnp.exp(l - b)).astype(bf16) for k, l, b in zip(ks, lasts, bs)]
        upds = [jnp.where(bd, _mm_tn(v.astype(bf16), kd), 0.0) for v, kd in zip(vs, kds)]
        decs = [jnp.exp(l) for l in lasts]
        states = [st_ref[...] for st_ref in st_refs]
        for i, (d, _) in enumerate(streams):
            acc_refs[d][pl.ds(sts[i], c), :] = os_[i] + _mm_nt(qis[i], states[d].astype(bf16))
            states[d] = states[d] * decs[i] + upds[i]
        for d in range(2):
            st_refs[d][...] = states[d]
        return carry

    lax.fori_loop(0, n // un, body, 0)


def _gla_kernel(z_ref, wlr_ref, blr_ref, g_ref, o_ref, gk_ref, accf_ref, accb_ref, stf_ref, stb_ref, *, s):
    rb = 256

    def gates(st, i):
        lr = z_ref[pl.ds(st, rb), 768:896].astype(bf16)
        x = _mm(lr, wlr_ref[...]) + blr_ref[...]
        gk_ref[pl.ds(st, rb), :] = jax.nn.log_sigmoid(x) * (1.0 / GLA_NORMALIZER)
    _rows(s, rb, gates)

    scale = GLA_DK ** -0.5

    def load(st, d):
        q = z_ref[pl.ds(st, GLA_CHUNK), 0:128] * scale
        k = z_ref[pl.ds(st, GLA_CHUNK), 128:256]
        v = z_ref[pl.ds(st, GLA_CHUNK), 256:512]
        g = gk_ref[pl.ds(st, GLA_CHUNK), d * 128:(d + 1) * 128]
        return q, k, v, g
    _gl_bidir([functools.partial(load, d=d) for d in range(2)], (accf_ref, accb_ref), (stf_ref, stb_ref),
              s=s, c=GLA_CHUNK, dk=GLA_DK)

    mavg = _block_avg(BRANCH_W, HEAD_W)

    def out(st, i):
        y = _head_rmsnorm(accf_ref[pl.ds(st, rb), :] + accb_ref[pl.ds(st, rb), :], g_ref[...], mavg)
        gate = z_ref[pl.ds(st, rb), 512:768]
        o_ref[pl.ds(st, rb), :] = y * (gate * jax.nn.sigmoid(gate))
    _rows(s, rb, out)


def _gla(za, wlr, blr, g, nb, s):
    return pl.pallas_call(
        functools.partial(_gla_kernel, s=s),
        grid=(nb,),
        in_specs=[pl.BlockSpec((None, s, ZA_W), lambda b: (b, 0, 0)),
                  _const_spec((LANES, 256)), _const_spec((1, 256)), _const_spec((1, BRANCH_W))],
        out_specs=pl.BlockSpec((None, s, BRANCH_W), lambda b: (b, 0, 0)),
        out_shape=jax.ShapeDtypeStruct((nb, s, BRANCH_W), f32),
        scratch_shapes=[pltpu.VMEM((s, 256), f32)] + [pltpu.VMEM((s, BRANCH_W), f32)] * 2
        + [pltpu.VMEM((BRANCH_W, HEADS * GLA_DK), f32)] * 2,
        compiler_params=_cparams("parallel"),
        name="gla",
    )(za.reshape(nb, s, ZA_W), wlr, blr, g)


def _hgrn_kernel(z_ref, lbp_ref, g_ref, o_ref, q_ref, lf_ref, kd_ref, accf_ref, accb_ref, stf_ref, stb_ref,
                 *, s, layer):
    rb = 256
    x = lbp_ref[...]
    e = jnp.exp(x - jnp.max(x, axis=0, keepdims=True))
    p = e / jnp.sum(e, axis=0, keepdims=True)
    lb = p[0:1]
    for j in range(1, layer + 1):
        lb = lb + p[j:j + 1]
    lb = lb - p[0:1]

    def prep(st, i):
        qz = z_ref[pl.ds(st, rb), 0:256]
        q_ref[pl.ds(st, rb), :] = qz * jax.nn.sigmoid(qz)
        fz = z_ref[pl.ds(st, rb), 256:768]
        lf_ref[pl.ds(st, rb), :] = jnp.log(lb + (1.0 - lb) * jax.nn.sigmoid(fz))
        kd_ref[pl.ds(st, rb), :] = (1.0 - lb) * jax.nn.sigmoid(-fz)
    _rows(s, rb, prep)

    kw = HEADS * HGRN_DF

    def load(st, d):
        q = q_ref[pl.ds(st, HGRN_CHUNK), :]
        k = kd_ref[pl.ds(st, HGRN_CHUNK), d * kw:(d + 1) * kw]
        v = z_ref[pl.ds(st, HGRN_CHUNK), 768:1024]
        g = lf_ref[pl.ds(st, HGRN_CHUNK), d * kw:(d + 1) * kw]
        return q, k, v, g
    _gl_bidir([functools.partial(load, d=d) for d in range(2)], (accf_ref, accb_ref), (stf_ref, stb_ref),
              s=s, c=HGRN_CHUNK, dk=HGRN_DF)

    mavg = _block_avg(BRANCH_W, HEAD_W)

    def out(st, i):
        y = _head_rmsnorm(accf_ref[pl.ds(st, rb), :] + accb_ref[pl.ds(st, rb), :], g_ref[...], mavg)
        gate = z_ref[pl.ds(st, rb), 1024:1280]
        o_ref[pl.ds(st, rb), :] = y * (gate * jax.nn.sigmoid(gate))
    _rows(s, rb, out)


def _hgrn(zd, lbp, g, nb, s, layer):
    kw = HEADS * HGRN_DF
    return pl.pallas_call(
        functools.partial(_hgrn_kernel, s=s, layer=layer),
        grid=(nb,),
        in_specs=[pl.BlockSpec((None, s, ZD_W), lambda b: (b, 0, 0)),
                  _const_spec((DEPTH, 2 * kw)), _const_spec((1, BRANCH_W))],
        out_specs=pl.BlockSpec((None, s, BRANCH_W), lambda b: (b, 0, 0)),
        out_shape=jax.ShapeDtypeStruct((nb, s, BRANCH_W), f32),
        scratch_shapes=[pltpu.VMEM((s, kw), f32), pltpu.VMEM((s, 2 * kw), f32), pltpu.VMEM((s, 2 * kw), f32)]
        + [pltpu.VMEM((s, BRANCH_W), f32)] * 2 + [pltpu.VMEM((BRANCH_W, kw), f32)] * 2,
        compiler_params=_cparams("parallel"),
        name="hgrn",
    )(zd.reshape(nb, s, ZD_W), lbp, g)


def _mlstm_bidir(z_ref, q_ref, k_ref, acc_refs, st_refs, nv_refs, m_refs, *, s):
    c, w = MLSTM_CHUNK, BRANCH_W
    n, un = s // c, MLSTM_UNROLL
    ones = jnp.ones((c, c), bf16)
    lane, row = _iota((c, w), 1), _iota((c, w), 0)
    head, sidx = lane // HEAD_W, lane % HEAD_W
    diag = sidx == row
    same = (_iota((w, w), 0) // HEAD_W) == (_iota((w, w), 1) // HEAD_W)
    bones = jnp.where(same, 1.0, 0.0).astype(bf16)
    tris = (_tri(c, False), _tri(c, True))
    causals = (sidx <= row, sidx >= row)
    for d in range(2):
        st_refs[d][...] = jnp.zeros((w, w), f32)
        nv_refs[d][...] = jnp.zeros((1, w), f32)
        m_refs[d][...] = jnp.zeros((1, w), f32)

    streams = [(d, u) for u in range(un) for d in range(2)]

    def seg_max(x):
        out = jnp.full((c, w), NEG_BIG, f32)
        for j in range(HEADS):
            mj = jnp.max(jnp.where(head == j, x, NEG_BIG), axis=-1, keepdims=True)
            out = jnp.where(head == j, mj, out)
        return out

    def body(ci, carry):
        idx = [(ci * un + u) if d == 0 else (n - 1 - ci * un - u) for d, u in streams]
        sts = [pl.multiple_of(i * c, c) for i in idx]
        qs = [q_ref[pl.ds(st, c), :] for st in sts]
        ks = [k_ref[pl.ds(st, c), :] for st in sts]
        vs = [z_ref[pl.ds(st, c), 512:768] for st in sts]
        igs = [z_ref[pl.ds(st, c), 1024 + d * w:1024 + (d + 1) * w] for (d, _), st in zip(streams, sts)]
        lfs = [jax.nn.log_sigmoid(z_ref[pl.ds(st, c), 1536 + d * w:1536 + (d + 1) * w])
               for (d, _), st in zip(streams, sts)]
        bs = [_lmul3(tris[d], lf) for (d, _), lf in zip(streams, lfs)]
        rs = [_lmul3(ones, jnp.where(diag, ig - b, 0.0)) for ig, b in zip(igs, bs)]
        log_ds = [jnp.where(causals[d], b + r, NEG_BIG) for (d, _), b, r in zip(streams, bs, rs)]
        mrows = [seg_max(x) for x in log_ds]
        lasts = [b[0:1] if d == 1 else b[c - 1:c] for (d, _), b in zip(streams, bs)]
        log_ws = [l - b + ig for l, b, ig in zip(lasts, bs, igs)]
        mws = [jnp.max(x, axis=0, keepdims=True) for x in log_ws]
        qbs = [q.astype(bf16) for q in qs]
        kexps = [jnp.concatenate([jnp.where(head == j, k, 0.0).astype(bf16) for j in range(HEADS)], axis=0)
                 for k in ks]
        vexps = [jnp.concatenate([jnp.where(head == j, v, 0.0).astype(bf16) for j in range(HEADS)], axis=0)
                 for v in vs]
        qks = [_mm_nt(qb, kexp) for qb, kexp in zip(qbs, kexps)]
        ms = [m_ref[...] for m_ref in m_refs]
        m_ts, w_prevs, w_ss, decs = [], [], [], []
        for i, (d, _) in enumerate(streams):
            log_prev = bs[i] + ms[d]
            m_t = jnp.maximum(mrows[i], log_prev)
            m_new = jnp.maximum(lasts[i] + ms[d], mws[i])
            m_ts.append(m_t)
            w_prevs.append(jnp.exp(log_prev - m_t))
            w_ss.append(jnp.exp(log_ws[i] - m_new))
            decs.append(jnp.exp(lasts[i] + ms[d] - m_new))
            ms[d] = m_new
        scs = [qk * jnp.exp(log_d - m_t) for qk, log_d, m_t in zip(qks, log_ds, m_ts)]
        nums = [_mm(sc.astype(bf16), vexp) for sc, vexp in zip(scs, vexps)]
        dens = [_rmul3(sc, bones) for sc in scs]
        upds = [jnp.where(same, _mm_tn((v * w_s).astype(bf16), k.astype(bf16)), 0.0)
                for v, w_s, k in zip(vs, w_ss, ks)]
        nvus = [jnp.sum(w_s * k, axis=0, keepdims=True) for w_s, k in zip(w_ss, ks)]
        states = [st_ref[...] for st_ref in st_refs]
        nvs = [nv_ref[...] for nv_ref in nv_refs]
        for i, (d, _) in enumerate(streams):
            num = w_prevs[i] * _mm_nt(qbs[i], states[d].astype(bf16)) + nums[i]
            den = w_prevs[i] * _rmul3(qs[i] * nvs[d], bones) + dens[i]
            acc_refs[d][pl.ds(sts[i], c), :] = num / jnp.maximum(jnp.abs(den), jnp.exp(-m_ts[i]))
            states[d] = decs[i] * states[d] + upds[i]
            nvs[d] = decs[i] * nvs[d] + nvus[i]
        for d in range(2):
            st_refs[d][...] = states[d]
            nv_refs[d][...] = nvs[d]
            m_refs[d][...] = ms[d]
        return carry

    lax.fori_loop(0, n // un, body, 0)


def _mlstm_kernel(z_ref, cw_ref, cb_ref, g_ref, o_ref, q_ref, k_ref, accf_ref, accb_ref, stf_ref, stb_ref,
                  nvf_ref, nvb_ref, mf_ref, mb_ref, *, s):
    rb = 256
    nblk = s // rb
    w = BRANCH_W

    def conv(st, i):
        x = z_ref[pl.ds(st, rb), 0:2 * w]
        pst = pl.multiple_of(jnp.maximum(st - 8, 0), 8)
        nst = pl.multiple_of(jnp.minimum(st + rb, s - 8), 8)
        prev = z_ref[pl.ds(pst, 8), 0:2 * w][7:8]
        nxt = z_ref[pl.ds(nst, 8), 0:2 * w][0:1]
        prev = jnp.where(i == 0, 0.0, prev)
        nxt = jnp.where(i == nblk - 1, 0.0, nxt)
        row = _iota((rb, 2 * w), 0)
        xm = jnp.where(row == 0, prev, pltpu.roll(x, 1, 0))
        xp = jnp.where(row == rb - 1, nxt, pltpu.roll(x, rb - 1, 0))
        y = cb_ref[...] + xm * cw_ref[0:1, :]
        y = y + x * cw_ref[1:2, :]
        y = y + xp * cw_ref[2:3, :]
        y = y * jax.nn.sigmoid(y)
        q_ref[pl.ds(st, rb), :] = y[:, 0:w]
        k_ref[pl.ds(st, rb), :] = y[:, w:2 * w] * (HEAD_W ** -0.5)
    _rows(s, rb, conv)

    _mlstm_bidir(z_ref, q_ref, k_ref, (accf_ref, accb_ref), (stf_ref, stb_ref), (nvf_ref, nvb_ref),
                 (mf_ref, mb_ref), s=s)

    mavg = _block_avg(w, HEAD_W)

    def out(st, i):
        y = _head_rmsnorm(accf_ref[pl.ds(st, rb), :] + accb_ref[pl.ds(st, rb), :], g_ref[...], mavg)
        o_ref[pl.ds(st, rb), :] = jax.nn.sigmoid(z_ref[pl.ds(st, rb), 768:1024]) * y
    _rows(s, rb, out)


def _mlstm(zb, cw, cb, g, nb, s):
    w = BRANCH_W
    return pl.pallas_call(
        functools.partial(_mlstm_kernel, s=s),
        grid=(nb,),
        in_specs=[pl.BlockSpec((None, s, ZB_W), lambda b: (b, 0, 0)),
                  _const_spec((3, 2 * w)), _const_spec((1, 2 * w)), _const_spec((1, w))],
        out_specs=pl.BlockSpec((None, s, w), lambda b: (b, 0, 0)),
        out_shape=jax.ShapeDtypeStruct((nb, s, w), f32),
        scratch_shapes=[pltpu.VMEM((s, w), f32)] * 4 + [pltpu.VMEM((w, w), f32)] * 2 + [pltpu.VMEM((1, w), f32)] * 4,
        compiler_params=_cparams("parallel"),
        name="mlstm",
    )(zb.reshape(nb, s, ZB_W), cw, cb, g)


def _rope(x, cos, sa, sb):
    w = x.shape[-1]
    half = HEAD_W // 2
    return x * cos + pltpu.roll(x, w - half, 1) * sa + pltpu.roll(x, half, 1) * sb


def _dil_kernel(z_ref, cos_ref, sa_ref, sb_ref, o_ref, kr_ref, vm_ref, *, s, tq):
    w = BRANCH_W
    qi = pl.program_id(1)
    rb = 256

    @pl.when(qi == 0)
    def _():
        head = _iota((rb, w), 1) // HEAD_W

        def prep(st, i):
            x = z_ref[pl.ds(st, rb), w:2 * w]
            kr_ref[pl.ds(st, rb), :] = _rope(x, cos_ref[pl.ds(st, rb), :], sa_ref[pl.ds(st, rb), :],
                                             sb_ref[pl.ds(st, rb), :]).astype(bf16)
            v = z_ref[pl.ds(st, rb), 2 * w:3 * w]
            for j in range(HEADS):
                vm_ref[j, pl.ds(st, rb), :] = jnp.where(head == j, v, 0.0).astype(bf16)
        _rows(s, rb, prep)

    q0 = pl.multiple_of(qi * tq, tq)
    qr = _rope(z_ref[pl.ds(q0, tq), 0:w], cos_ref[pl.ds(q0, tq), :], sa_ref[pl.ds(q0, tq), :],
               sb_ref[pl.ds(q0, tq), :])
    rel = _iota((tq, s), 1) - (_iota((tq, s), 0) + q0)
    dist = jnp.abs(rel)
    cnt = jnp.zeros((tq, s), f32)
    for window, dil in DIL_PATTERNS:
        reach = (window // (2 * dil)) * dil
        ok = dist <= reach
        if dil > 1:
            ok = jnp.logical_and(ok, (rel & (dil - 1)) == 0)
        cnt = cnt + jnp.where(ok, 1.0, 0.0)
    valid = cnt > 0.0
    qhead = _iota((tq, w), 1) // HEAD_W
    acc = jnp.zeros((tq, w), f32)
    for j in range(HEADS):
        qh = jnp.where(qhead == j, qr, 0.0).astype(bf16)
        sc = _mm_nt(qh, kr_ref[...]) * (HEAD_W ** -0.5)
        sc = jnp.where(valid, sc, NEG_BIG)
        m = jnp.max(sc, axis=-1, keepdims=True)
        p = jnp.exp(sc - m) * cnt
        l = jnp.sum(p, axis=-1, keepdims=True)
        acc = acc + _mm(p.astype(bf16), vm_ref[j]) / l
    o_ref[...] = acc


def _dil(zc, cos, sa, sb, nb, s, tq=256):
    w = BRANCH_W
    for _, dil in DIL_PATTERNS:
        assert dil & (dil - 1) == 0 and s % dil == 0
    return pl.pallas_call(
        functools.partial(_dil_kernel, s=s, tq=tq),
        grid=(nb, s // tq),
        in_specs=[pl.BlockSpec((None, s, ZC_W), lambda b, i: (b, 0, 0)),
                  _const_spec((s, w)), _const_spec((s, w)), _const_spec((s, w))],
        out_specs=pl.BlockSpec((None, tq, w), lambda b, i: (b, i, 0)),
        out_shape=jax.ShapeDtypeStruct((nb, s, w), f32),
        scratch_shapes=[pltpu.VMEM((s, w), bf16), pltpu.VMEM((HEADS, s, w), bf16)],
        compiler_params=_cparams("parallel", "arbitrary"),
        name="dilattn",
    )(zc.reshape(nb, s, ZC_W), cos, sa, sb)


def _merge_kernel(h_ref, ya, yb, yc, yd, g_ref, wg_ref, wbr_ref, wo_ref, o_ref):
    x = h_ref[...]
    u = _rms(x, g_ref[...]).astype(bf16)
    merged = None
    for n, y in enumerate((ya, yb, yc, yd)):
        gate = jax.nn.sigmoid(_mm(u, wg_ref[n]))
        term = gate * _mm(y[...].astype(bf16), wbr_ref[n])
        merged = term if merged is None else merged + term
    o_ref[...] = x + _mm(merged.astype(bf16), wo_ref[...])


def _merge(h, ys, g, wg, wbr, wo, tm=512):
    t = h.shape[0]
    tok = lambda w: pl.BlockSpec((tm, w), lambda i: (i, 0))
    return pl.pallas_call(
        _merge_kernel,
        grid=(t // tm,),
        in_specs=[tok(D_MODEL)] + [tok(BRANCH_W)] * 4
        + [_const_spec((1, D_MODEL)), _const_spec((4, D_MODEL, D_MODEL)),
           _const_spec((4, BRANCH_W, D_MODEL)), _const_spec((D_MODEL, D_MODEL))],
        out_specs=tok(D_MODEL),
        out_shape=jax.ShapeDtypeStruct((t, D_MODEL), f32),
        compiler_params=_cparams("parallel"),
        name="merge",
    )(h, *ys, g, wg, wbr, wo)


def _stair_cells():
    return [(a, b) for a in range(PEER_TOPK) for b in range(PEER_TOPK) if (a + 1) * (b + 1) <= PEER_TOPK]


_CELLS = _stair_cells()
_NCELL = len(_CELLS)
_NCELL_PAD = -(-_NCELL // 8) * 8


def _extract_top(work, rowf, nrows, exact):
    tops = []
    rank = jnp.full(work.shape, float(PEER_TOPK), f32)
    for r in range(PEER_TOPK):
        mx = jnp.max(work, axis=0, keepdims=True)
        hit = work == mx
        if exact:
            cand = jnp.where(hit, rowf, float(nrows))
            hit = cand == jnp.min(cand, axis=0, keepdims=True)
        work = jnp.where(hit, -jnp.inf, work)
        rank = jnp.where(hit, float(r), rank)
        tops.append(mx)
    removed = jnp.sum(jnp.where(rank < float(PEER_TOPK), 1.0, 0.0), axis=0, keepdims=True)
    return tops, rank, removed


def _route_kernel(h_ref, g_ref, wpq_ref, keys_ref, sa_ref, ut_ref, a_ref, cnt_ref, rk_ref, bn_ref,
                  q_ref, m_ref, *, tm):
    uf = _rms(h_ref[...], g_ref[...])
    ut_ref[...] = uf.T.astype(bf16)
    q_ref[...] = _mm(uf.astype(bf16), wpq_ref[...]).astype(bf16)
    nk = PEER_NKEYS
    rowf = _iota((nk, tm), 0).astype(f32)
    rowcf = _iota((_NCELL_PAD, tm), 0).astype(f32)
    topk = float(PEER_TOPK)

    def head_tables(hd, exact):
        scs, tops, ranks = [], [], []
        tied = jnp.zeros((1, tm), f32)
        for p in range(2):
            idx = hd * 2 + p
            qs = q_ref[:, pl.ds(pl.multiple_of(idx * nk, nk), nk)]
            sc = _mm_nt(keys_ref[idx], qs)
            top, rank, removed = _extract_top(sc, rowf, nk, exact)
            tied = jnp.where(removed == topk, tied, 1.0)
            scs.append(sc)
            tops.append(top)
            ranks.append(rank)
        m_ref[...] = jnp.full((_NCELL_PAD, tm), -jnp.inf, f32)
        for ci, (a, b) in enumerate(_CELLS):
            m_ref[ci:ci + 1, :] = tops[0][a] + tops[1][b]
        cand = m_ref[...]
        _, crank, removed = _extract_top(cand, rowcf, _NCELL_PAD, exact)
        tied = jnp.where(removed == topk, tied, 1.0)
        chosen = crank < topk
        wgt = jnp.where(chosen, jnp.exp(cand - cand[0:1]), 0.0)
        zsum = jnp.sum(wgt, axis=0, keepdims=True)
        rowcnt = _mm(sa_ref[...], jnp.where(chosen, 1.0, 0.0).astype(bf16))
        cnt = jnp.zeros((nk, tm), f32)
        for a in range(PEER_TOPK):
            cnt = jnp.where(ranks[0] == float(a), rowcnt[a:a + 1], cnt)
        rows = pl.ds(pl.multiple_of(hd * nk, nk), nk)
        amat = jnp.exp(scs[0] - tops[0][0])
        prow = pl.ds(pl.multiple_of(hd * (nk // 2), nk // 2), nk // 2)
        rkp = pltpu.bitcast(ranks[1].astype(bf16), jnp.uint32)
        bnp = pltpu.bitcast((jnp.exp(scs[1] - tops[1][0]) / zsum).astype(bf16), jnp.uint32)
        for tb in range(tm // LANES):
            lanes = slice(tb * LANES, (tb + 1) * LANES)
            a_ref[tb, rows, :] = amat[:, lanes]
            cnt_ref[tb, rows, :] = cnt[:, lanes]
            rk_ref[tb, prow, :] = rkp[:, lanes]
            bn_ref[tb, prow, :] = bnp[:, lanes]
        return tied

    def per_head(hd, carry):
        tied = head_tables(hd, False)

        @pl.when(jnp.max(tied) > 0.0)
        def _():
            head_tables(hd, True)
        return carry

    lax.fori_loop(0, PEER_HEADS, per_head, 0)


def _route(h, g, wpq, keys, sa, tm=256):
    t = h.shape[0]
    nh, nk = PEER_HEADS, PEER_NKEYS
    tab = pl.BlockSpec((tm // LANES, nh * nk, LANES), lambda i: (i, 0, 0))
    ptab = pl.BlockSpec((tm // LANES, nh * nk // 2, LANES), lambda i: (i, 0, 0))
    return pl.pallas_call(
        functools.partial(_route_kernel, tm=tm),
        grid=(t // tm,),
        in_specs=[pl.BlockSpec((tm, D_MODEL), lambda i: (i, 0)), _const_spec((1, D_MODEL)),
                  _const_spec((D_MODEL, 2 * nh * nk)), _const_spec((2 * nh, nk, nk)),
                  _const_spec((PEER_TOPK, _NCELL_PAD))],
        out_specs=[pl.BlockSpec((D_MODEL, tm), lambda i: (0, i)), tab, tab, ptab, ptab],
        out_shape=[jax.ShapeDtypeStruct((D_MODEL, t), bf16)] + [jax.ShapeDtypeStruct((t // LANES, nh * nk, LANES), f32)] * 2
        + [jax.ShapeDtypeStruct((t // LANES, nh * nk // 2, LANES), jnp.uint32)] * 2,
        scratch_shapes=[pltpu.VMEM((tm, 2 * nh * nk), bf16), pltpu.VMEM((_NCELL_PAD, tm), f32)],
        compiler_params=_cparams("parallel"),
        name="route",
    )(h, g, wpq, keys, sa)


def _gelu_bf16(x):
    return (0.5 * x * (1.0 + lax.erf(x * math.sqrt(0.5)))).astype(bf16)


def _peer_gate_block(act_ref, p_ref, tabs, i0, ii, tb):
    a_ref, cnt_ref, rk_ref, bn_ref = tabs
    nk = PEER_NKEYS
    lanes = slice(tb * LANES, (tb + 1) * LANES)
    gate = None
    for hd in range(PEER_HEADS):
        row = pl.ds(hd * nk + i0 + ii, 1)
        a = jnp.broadcast_to(a_ref[tb, row, :], (nk, LANES)).astype(bf16)
        c = jnp.broadcast_to(cnt_ref[tb, row, :], (nk, LANES)).astype(bf16)
        prow = slice(hd * (nk // 2), (hd + 1) * (nk // 2))
        rk = pltpu.bitcast(rk_ref[tb, prow, :], bf16)
        bn = pltpu.bitcast(bn_ref[tb, prow, :], bf16)
        term = a * jnp.where(rk < c, bn, jnp.zeros((), bf16))
        gate = term if gate is None else gate + term
    p_ref[ii * nk:(ii + 1) * nk, lanes] = gate * act_ref[ii * nk:(ii + 1) * nk, lanes]


def _mxu_product(rhs_tile, lhs_tile, between):
    nt = PEER_SUB // MXU_TILE
    for q in range(2):
        pltpu.matmul_push_rhs(rhs_tile(0, q), staging_register=0, mxu_index=q)
    for kt in range(nt):
        reg = kt % 2
        for mc in range(nt):
            lhs = lhs_tile(mc, kt)
            for q in range(2):
                pltpu.matmul_acc_lhs(mc * (MXU_TILE // 4), lhs, mxu_index=q,
                                     load_staged_rhs=(reg if mc == 0 else None))
            if mc == 0 and kt + 1 < nt:
                for q in range(2):
                    pltpu.matmul_push_rhs(rhs_tile(kt + 1, q), staging_register=1 - reg, mxu_index=q)
            between()


def _peer_kernel(ut_ref, a_ref, cnt_ref, rk_ref, bn_ref, pu_ref, pvt_ref, o_ref, acc_ref, act0, act1, p0, p1,
                 *, ne, tp):
    g = pl.program_id(0)
    nblocks = pl.num_programs(0) - 2
    b1 = jnp.clip(g - 1, 0, nblocks - 1)
    b2 = jnp.clip(g - 2, 0, nblocks - 1)
    i1 = (b1 % ne) * (PEER_SUB // PEER_NKEYS)
    tabs = (a_ref, cnt_ref, rk_ref, bn_ref)

    @pl.when(g == 0)
    def _():
        for ref in (act0, act1, p0, p1):
            ref[...] = jnp.zeros(ref.shape, bf16)

    @pl.when(b2 % ne == 0)
    def _():
        acc_ref[...] = jnp.zeros(acc_ref.shape, f32)

    t256 = lambda i: slice(i * MXU_TILE, (i + 1) * MXU_TILE)

    def stage(act_w, act_r, p_w, p_r):
        blocks = iter([(ii, tb) for ii in range(PEER_SUB // PEER_NKEYS) for tb in range(tp // LANES)])

        def one_gate_block():
            ii, tb = next(blocks)
            _peer_gate_block(act_r, p_w, tabs, i1, ii, tb)

        nt = PEER_SUB // MXU_TILE
        _mxu_product(lambda kt, q: ut_ref[t256(kt), t256(q)], lambda mc, kt: pu_ref[t256(mc), t256(kt)],
                     one_gate_block)
        for mc in range(nt):
            for q in range(2):
                x = pltpu.matmul_pop(mc * (MXU_TILE // 4), (MXU_TILE, MXU_TILE), f32, q)
                act_w[t256(mc), t256(q)] = _gelu_bf16(x)
        _mxu_product(lambda kt, q: p_r[t256(kt), t256(q)], lambda mc, kt: pvt_ref[t256(mc), t256(kt)],
                     one_gate_block)
        for mc in range(nt):
            for q in range(2):
                acc_ref[t256(mc), t256(q)] += pltpu.matmul_pop(mc * (MXU_TILE // 4), (MXU_TILE, MXU_TILE), f32, q)

    @pl.when(g % 2 == 0)
    def _():
        stage(act0, act1, p1, p0)

    @pl.when(g % 2 == 1)
    def _():
        stage(act1, act0, p0, p1)

    @pl.when(jnp.logical_and(g >= 2, b2 % ne == ne - 1))
    def _():
        o_ref[...] = acc_ref[...].T


def _peer(ut, tabs, pu, pvt):
    t = ut.shape[1]
    tp = 2 * MXU_TILE
    nh, nk = PEER_HEADS, PEER_NKEYS
    assert D_MODEL == PEER_SUB and (PEER_SUB // nk) * (tp // LANES) == 2 * (PEER_SUB // MXU_TILE) ** 2
    ne = pu.shape[0] // PEER_SUB
    nblocks = (t // tp) * ne
    blk = lambda lag: (lambda g: jnp.clip(g - lag, 0, nblocks - 1))
    b0, b1, b2 = blk(0), blk(1), blk(2)
    tab = pl.BlockSpec((tp // LANES, nh * nk, LANES), lambda g: (b1(g) // ne, 0, 0))
    ptab = pl.BlockSpec((tp // LANES, nh * nk // 2, LANES), lambda g: (b1(g) // ne, 0, 0))
    return pl.pallas_call(
        functools.partial(_peer_kernel, ne=ne, tp=tp),
        grid=(nblocks + 2,),
        in_specs=[pl.BlockSpec((D_MODEL, tp), lambda g: (0, b0(g) // ne)), tab, tab, ptab, ptab,
                  pl.BlockSpec((PEER_SUB, D_MODEL), lambda g: (b0(g) % ne, 0)),
                  pl.BlockSpec((D_MODEL, PEER_SUB), lambda g: (0, b2(g) % ne))],
        out_specs=pl.BlockSpec((tp, D_MODEL), lambda g: (b2(g) // ne, 0)),
        out_shape=jax.ShapeDtypeStruct((t, D_MODEL), f32),
        scratch_shapes=[pltpu.VMEM((D_MODEL, tp), f32)] + [pltpu.VMEM((PEER_SUB, tp), bf16)] * 4,
        compiler_params=_cparams("arbitrary"),
        name="peer",
    )(ut, *tabs, pu, pvt)


def _pe_kernel(h_ref, po_ref, p_ref, g_ref, wpg_ref, wpe_ref, gf_ref, o_ref, *, final):
    h = h_ref[...] + po_ref[...]
    gate = jax.nn.sigmoid(_mm(_rms(h, g_ref[...]).astype(bf16), wpg_ref[...]))
    h = h + gate * _mm(p_ref[...].astype(bf16), wpe_ref[...])
    if final:
        h = _rms(h, gf_ref[...])
    o_ref[...] = h


def _pe(h, po, p, g, wpg, wpe, gf, final, tm=512):
    t = h.shape[0]
    tok = lambda w: pl.BlockSpec((tm, w), lambda i: (i, 0))
    return pl.pallas_call(
        functools.partial(_pe_kernel, final=final),
        grid=(t // tm,),
        in_specs=[tok(D_MODEL), tok(D_MODEL), tok(PE_DIM), _const_spec((1, D_MODEL)),
                  _const_spec((D_MODEL, D_MODEL)), _const_spec((PE_DIM, D_MODEL)), _const_spec((1, D_MODEL))],
        out_specs=tok(D_MODEL),
        out_shape=jax.ShapeDtypeStruct((t, D_MODEL), f32),
        compiler_params=_cparams("parallel"),
        name="pe_embed",
    )(h, po, p, g, wpg, wpe, gf)


def _cols(w, name):
    a, b = _PROJ_OFF[name]
    return w[..., a:b]


def _proj_groups(w):
    pad = jnp.zeros(w.shape[:-1] + (ZA_W - 800,), w.dtype)
    ga = jnp.concatenate([_cols(w, n) for n in ('a_q', 'a_k', 'a_v', 'a_g', 'a_lr')] + [pad], axis=-1)
    gates = [jnp.repeat(_cols(w, n), HEAD_W, axis=-1) for n in ('b_i', 'b_f')]
    gb = jnp.concatenate([_cols(w, n) for n in ('b_q', 'b_k', 'b_v', 'b_o')] + gates, axis=-1)
    gc = jnp.concatenate([_cols(w, n) for n in ('c_q', 'c_k', 'c_v')], axis=-1)
    gd = jnp.concatenate([_cols(w, n) for n in ('d_q', 'd_f', 'd_i', 'd_g')], axis=-1)
    return ga, gb, gc, gd


def _rope_tables(s):
    half = HEAD_W // 2
    inv = ROPE_THETA ** (-jnp.arange(half, dtype=f32) / half)
    ang = jnp.arange(s, dtype=f32)[:, None] * inv[None, :]
    cos, sin = jnp.cos(ang), jnp.sin(ang)
    zero = jnp.zeros_like(sin)
    tile = lambda a, b: jnp.tile(jnp.concatenate([a, b], axis=-1), (1, HEADS))
    return tile(cos, cos), tile(-sin, zero), tile(zero, sin)


def _row_count_matrix():
    m = np.zeros((PEER_TOPK, _NCELL_PAD), np.float32)
    for ci, (a, _) in enumerate(_CELLS):
        m[a, ci] = 1.0
    return jnp.asarray(m, bf16)


def _prepare(g_mix, w_in, b_in, w_lr2, b_lr2, conv_w, conv_b, lb_param, g_a, g_b, g_d, w_gate, w_br, w_out,
             g_ffn, w_pq, sub_keys, peer_u, peer_v, g_pe, w_pe, w_pg, g_final):
    layers = []
    row = lambda a: a.reshape(1, -1)
    for l in range(DEPTH):
        wlr = jnp.zeros((LANES, 256), f32)
        for n in range(2):
            wlr = wlr.at[n * GLA_RANK:(n + 1) * GLA_RANK, n * 128:(n + 1) * 128].set(w_lr2[l, n])
        layers.append(dict(
            g_mix=row(g_mix[l]),
            w_proj=[w.astype(bf16) for w in _proj_groups(w_in[l])],
            b_proj=list(_proj_groups(row(b_in[l]))),
            wlr=wlr.astype(bf16), blr=row(b_lr2[l]),
            conv_w=conv_w[l], conv_b=row(conv_b[l]),
            g_a=row(g_a[l]), g_b=row(g_b[l]), g_d=row(g_d[l]),
            w_gate=w_gate[l].astype(bf16), w_br=w_br[l].astype(bf16), w_out=w_out[l].astype(bf16),
            g_ffn=row(g_ffn[l]), w_pq=w_pq[l].astype(bf16),
            keys=sub_keys[l].reshape(2 * PEER_HEADS, PEER_NKEYS, -1).astype(bf16),
            peer_u=peer_u[l].astype(bf16), peer_vt=peer_v[l].astype(bf16).T,
            g_pe=row(g_pe[l]), w_pe=w_pe[l].astype(bf16), w_pg=w_pg[l].astype(bf16),
        ))
    shared = dict(
        lbp=jnp.transpose(lb_param, (1, 0, 2)).reshape(DEPTH, -1),
        g_final=row(g_final), sa=_row_count_matrix(),
    )
    return layers, shared


def _trunk(x, p, layers, shared):
    nb, s, d = x.shape
    t = nb * s
    h = x.reshape(t, d)
    cos, sa, sb = _rope_tables(s)
    for l, w in enumerate(layers):
        za, zb, zc, zd = _proj(h, w['g_mix'], w['w_proj'], w['b_proj'])
        ya = _gla(za, w['wlr'], w['blr'], w['g_a'], nb, s)
        yb = _mlstm(zb, w['conv_w'], w['conv_b'], w['g_b'], nb, s)
        yc = _dil(zc, cos, sa, sb, nb, s)
        yd = _hgrn(zd, shared['lbp'], w['g_d'], nb, s, l)
        ys = [y.reshape(t, BRANCH_W) for y in (ya, yb, yc, yd)]
        h1 = _merge(h, ys, w['g_mix'], w['w_gate'], w['w_br'], w['w_out'])
        ut, *tabs = _route(h1, w['g_ffn'], w['w_pq'], w['keys'], shared['sa'])
        po = _peer(ut, tabs, w['peer_u'], w['peer_vt'])
        h = _pe(h1, po, p[l].reshape(t, PE_DIM), w['g_pe'], w['w_pg'], w['w_pe'], shared['g_final'],
                final=(l == DEPTH - 1))
    return h.reshape(nb, s, d)


def kernel(x_prompt, x_sample, p_prompt, p_sample, g_mix, w_in, b_in, w_lr2, b_lr2, conv_w, conv_b, lb_param, g_a, g_b, g_d, w_gate, w_br, w_out, g_ffn, w_pq, sub_keys, peer_u, peer_v, g_pe, w_pe, w_pg, g_final):
    layers, shared = _prepare(g_mix, w_in, b_in, w_lr2, b_lr2, conv_w, conv_b, lb_param, g_a, g_b, g_d, w_gate,
                              w_br, w_out, g_ffn, w_pq, sub_keys, peer_u, peer_v, g_pe, w_pe, w_pg, g_final)
    return (_trunk(x_prompt, p_prompt, layers, shared), _trunk(x_sample, p_sample, layers, shared))
```

```python
import functools
import math

import numpy as np
import jax
import jax.numpy as jnp
from jax import lax
from jax.experimental import pallas as pl
from jax.experimental.pallas import tpu as pltpu

f32 = jnp.float32
bf16 = jnp.bfloat16
i32 = jnp.int32

D_MODEL = 1024
DEPTH = 2
PE_DIM = 256
EPS = 1e-6
NEG_BIG = -1e30
HEADS = 4
HEAD_W = 64
BRANCH_W = HEADS * HEAD_W
GLA_DK = 32
GLA_RANK = 16
GLA_NORMALIZER = 16.0
GLA_CHUNK = 32
MLSTM_CHUNK = 64
HGRN_DF = 64
HGRN_CHUNK = 32
DIL_PATTERNS = ((128, 1), (512, 4), (2048, 16))
ROPE_THETA = 10000.0
PEER_HEADS = 8
PEER_NKEYS = 128
PEER_TOPK = 16
LANES = 128
PEER_SUB = 1024
MXU_TILE = 256
EXP_CLAMP = 80.0
GL_UNROLL = 4
MLSTM_UNROLL = 2

_PROJ_LAYOUT = (
    ('a_q', 128), ('a_k', 128), ('a_v', 256), ('a_g', 256), ('a_lr', 32),
    ('b_q', 256), ('b_k', 256), ('b_v', 256), ('b_i', 8), ('b_f', 8), ('b_o', 256),
    ('c_q', 256), ('c_k', 256), ('c_v', 256),
    ('d_q', 256), ('d_f', 512), ('d_i', 256), ('d_g', 256),
)
_PROJ_OFF = {}
_o = 0
for _n, _s in _PROJ_LAYOUT:
    _PROJ_OFF[_n] = (_o, _o + _s)
    _o += _s

ZA_W = 896
ZB_W = 2048
ZC_W = 768
ZD_W = 1280

VMEM_LIMIT = 56 * 1024 * 1024


def _cparams(*sem):
    return pltpu.CompilerParams(dimension_semantics=sem, vmem_limit_bytes=VMEM_LIMIT)


def _const_spec(shape):
    nd = len(shape)
    return pl.BlockSpec(shape, lambda *_: (0,) * nd, pipeline_mode=pl.Buffered(1))


def _mm(a, b):
    return jnp.dot(a, b, preferred_element_type=f32)


def _mm_nt(a, b):
    return lax.dot_general(a, b, (((1,), (1,)), ((), ())), preferred_element_type=f32)


def _mm_tn(a, b):
    return lax.dot_general(a, b, (((0,), (0,)), ((), ())), preferred_element_type=f32)


def _split3(x):
    hi = x.astype(bf16)
    r1 = x - hi.astype(f32)
    mid = r1.astype(bf16)
    lo = (r1 - mid.astype(f32)).astype(bf16)
    return hi, mid, lo


def _lmul3(m, x):
    w = x.shape[1]
    y = _mm(m, jnp.concatenate(_split3(x), axis=1))
    return y[:, 0:w] + y[:, w:2 * w] + y[:, 2 * w:3 * w]


def _rmul3(x, m):
    r = x.shape[0]
    y = _mm(jnp.concatenate(_split3(x), axis=0), m)
    return y[0:r] + y[r:2 * r] + y[2 * r:3 * r]


def _rms(x, g):
    return x * lax.rsqrt(jnp.mean(x * x, axis=-1, keepdims=True) + EPS) * g


def _iota(shape, dim):
    return lax.broadcasted_iota(i32, shape, dim)


def _tri(c, reverse):
    r, k = _iota((c, c), 0), _iota((c, c), 1)
    return jnp.where((k >= r) if reverse else (k <= r), 1.0, 0.0).astype(bf16)


def _block_avg(w, blk):
    same = (_iota((w, w), 0) // blk) == (_iota((w, w), 1) // blk)
    return jnp.where(same, 1.0 / blk, 0.0).astype(bf16)


def _rows(n, blk, fn):
    def body(i, c):
        fn(pl.multiple_of(i * blk, blk), i)
        return c
    lax.fori_loop(0, n // blk, body, 0)


def _head_rmsnorm(o, g, mavg):
    ms = _rmul3(o * o, mavg)
    return o * lax.rsqrt(ms + EPS) * g


def _proj_kernel(h_ref, g_ref, wa, wb, wc, wd, ba, bb, bc, bd, za, zb, zc, zd):
    u = _rms(h_ref[...], g_ref[...]).astype(bf16)
    for w, b, z in ((wa, ba, za), (wb, bb, zb), (wc, bc, zc), (wd, bd, zd)):
        z[...] = _mm(u, w[...]) + b[...]


def _proj(h, g, ws, bs, tm=512):
    t = h.shape[0]
    widths = (ZA_W, ZB_W, ZC_W, ZD_W)
    tok = lambda w: pl.BlockSpec((tm, w), lambda i: (i, 0))
    return pl.pallas_call(
        _proj_kernel,
        grid=(t // tm,),
        in_specs=[tok(D_MODEL), _const_spec((1, D_MODEL))]
        + [_const_spec((D_MODEL, w)) for w in widths] + [_const_spec((1, w)) for w in widths],
        out_specs=[tok(w) for w in widths],
        out_shape=[jax.ShapeDtypeStruct((t, w), f32) for w in widths],
        compiler_params=_cparams("parallel"),
        name="proj",
    )(h, g, *ws, *bs)


def _gl_bidir(loads, acc_refs, st_refs, *, s, c, dk):
    h, dv = HEADS, HEAD_W
    kw, vw, hc = h * dk, h * dv, h * c
    n, un = s // c, GL_UNROLL
    khead = _iota((c, kw), 1) // dk
    vhead = _iota((c, vw), 1) // dv
    sidx, trow = _iota((c, hc), 1) % c, _iota((c, hc), 0)
    bd = (_iota((vw, kw), 0) // dv) == (_iota((vw, kw), 1) // dk)
    tris = (_tri(c, False), _tri(c, True))
    causals = (sidx <= trow, sidx >= trow)
    for st_ref in st_refs:
        st_ref[...] = jnp.zeros((vw, kw), f32)

    streams = [(d, u) for u in range(un) for d in range(2)]

    def body(ci, carry):
        idx = [(ci * un + u) if d == 0 else (n - 1 - ci * un - u) for d, u in streams]
        sts = [pl.multiple_of(i * c, c) for i in idx]
        ins = [loads[d](st) for (d, _), st in zip(streams, sts)]
        qs, ks, vs = [x[0] for x in ins], [x[1] for x in ins], [x[2] for x in ins]
        bs = [_lmul3(tris[d], x[3]) for (d, _), x in zip(streams, ins)]
        lasts = [b[0:1] if d == 1 else b[c - 1:c] for (d, _), b in zip(streams, bs)]
        mids = [b[c // 2:c // 2 + 1] for b in bs]
        qts = [(q * jnp.exp(jnp.minimum(b - m, EXP_CLAMP))).astype(bf16) for q, b, m in zip(qs, bs, mids)]
        kts = [k * jnp.exp(jnp.minimum(m - b, EXP_CLAMP)) for k, b, m in zip(ks, bs, mids)]
        kexps = [jnp.concatenate([jnp.where(khead == j, kt, 0.0).astype(bf16) for j in range(h)], axis=0)
                 for kt in kts]
        vexps = [jnp.concatenate([jnp.where(vhead == j, v, 0.0).astype(bf16) for j in range(h)], axis=0)
                 for v in vs]
        atts = [jnp.where(causals[d], _mm_nt(qt, kexp), 0.0).astype(bf16)
                for (d, _), qt, kexp in zip(streams, qts, kexps)]
        os_ = [_mm(att, vexp) for att, vexp in zip(atts, vexps)]
        qis = [(q * jnp.exp(b)).astype(bf16) for q, b in zip(qs, bs)]
        kds = [(k * jnp.exp(l - b)).astype(bf16) for k, l, b in zip(ks, lasts, bs)]
        upds = [jnp.where(bd, _mm_tn(v.astype(bf16), kd), 0.0) for v, kd in zip(vs, kds)]
        decs = [jnp.exp(l) for l in lasts]
        states = [st_ref[...] for st_ref in st_refs]
        for i, (d, _) in enumerate(streams):
            acc_refs[d][pl.ds(sts[i], c), :] = os_[i] + _mm_nt(qis[i], states[d].astype(bf16))
            states[d] = states[d] * decs[i] + upds[i]
        for d in range(2):
            st_refs[d][...] = states[d]
        return carry

    lax.fori_loop(0, n // un, body, 0)


def _gla_kernel(z_ref, wlr_ref, blr_ref, g_ref, o_ref, gk_ref, accf_ref, accb_ref, stf_ref, stb_ref, *, s):
    rb = 256

    def gates(st, i):
        lr = z_ref[pl.ds(st, rb), 768:896].astype(bf16)
        x = _mm(lr, wlr_ref[...]) + blr_ref[...]
        gk_ref[pl.ds(st, rb), :] = jax.nn.log_sigmoid(x) * (1.0 / GLA_NORMALIZER)
    _rows(s, rb, gates)

    scale = GLA_DK ** -0.5

    def load(st, d):
        q = z_ref[pl.ds(st, GLA_CHUNK), 0:128] * scale
        k = z_ref[pl.ds(st, GLA_CHUNK), 128:256]
        v = z_ref[pl.ds(st, GLA_CHUNK), 256:512]
        g = gk_ref[pl.ds(st, GLA_CHUNK), d * 128:(d + 1) * 128]
        return q, k, v, g
    _gl_bidir([functools.partial(load, d=d) for d in range(2)], (accf_ref, accb_ref), (stf_ref, stb_ref),
              s=s, c=GLA_CHUNK, dk=GLA_DK)

    mavg = _block_avg(BRANCH_W, HEAD_W)

    def out(st, i):
        y = _head_rmsnorm(accf_ref[pl.ds(st, rb), :] + accb_ref[pl.ds(st, rb), :], g_ref[...], mavg)
        gate = z_ref[pl.ds(st, rb), 512:768]
        o_ref[pl.ds(st, rb), :] = y * (gate * jax.nn.sigmoid(gate))
    _rows(s, rb, out)


def _gla(za, wlr, blr, g, nb, s):
    return pl.pallas_call(
        functools.partial(_gla_kernel, s=s),
        grid=(nb,),
        in_specs=[pl.BlockSpec((None, s, ZA_W), lambda b: (b, 0, 0)),
                  _const_spec((LANES, 256)), _const_spec((1, 256)), _const_spec((1, BRANCH_W))],
        out_specs=pl.BlockSpec((None, s, BRANCH_W), lambda b: (b, 0, 0)),
        out_shape=jax.ShapeDtypeStruct((nb, s, BRANCH_W), f32),
        scratch_shapes=[pltpu.VMEM((s, 256), f32)] + [pltpu.VMEM((s, BRANCH_W), f32)] * 2
        + [pltpu.VMEM((BRANCH_W, HEADS * GLA_DK), f32)] * 2,
        compiler_params=_cparams("parallel"),
        name="gla",
    )(za.reshape(nb, s, ZA_W), wlr, blr, g)


def _hgrn_kernel(z_ref, lbp_ref, g_ref, o_ref, q_ref, lf_ref, kd_ref, accf_ref, accb_ref, stf_ref, stb_ref,
                 *, s, layer):
    rb = 256
    x = lbp_ref[...]
    e = jnp.exp(x - jnp.max(x, axis=0, keepdims=True))
    p = e / jnp.sum(e, axis=0, keepdims=True)
    lb = p[0:1]
    for j in range(1, layer + 1):
        lb = lb + p[j:j + 1]
    lb = lb - p[0:1]

    def prep(st, i):
        qz = z_ref[pl.ds(st, rb), 0:256]
        q_ref[pl.ds(st, rb), :] = qz * jax.nn.sigmoid(qz)
        fz = z_ref[pl.ds(st, rb), 256:768]
        lf_ref[pl.ds(st, rb), :] = jnp.log(lb + (1.0 - lb) * jax.nn.sigmoid(fz))
        kd_ref[pl.ds(st, rb), :] = (1.0 - lb) * jax.nn.sigmoid(-fz)
    _rows(s, rb, prep)

    kw = HEADS * HGRN_DF

    def load(st, d):
        q = q_ref[pl.ds(st, HGRN_CHUNK), :]
        k = kd_ref[pl.ds(st, HGRN_CHUNK), d * kw:(d + 1) * kw]
        v = z_ref[pl.ds(st, HGRN_CHUNK), 768:1024]
        g = lf_ref[pl.ds(st, HGRN_CHUNK), d * kw:(d + 1) * kw]
        return q, k, v, g
    _gl_bidir([functools.partial(load, d=d) for d in range(2)], (accf_ref, accb_ref), (stf_ref, stb_ref),
              s=s, c=HGRN_CHUNK, dk=HGRN_DF)

    mavg = _block_avg(BRANCH_W, HEAD_W)

    def out(st, i):
        y = _head_rmsnorm(accf_ref[pl.ds(st, rb), :] + accb_ref[pl.ds(st, rb), :], g_ref[...], mavg)
        gate = z_ref[pl.ds(st, rb), 1024:1280]
        o_ref[pl.ds(st, rb), :] = y * (gate * jax.nn.sigmoid(gate))
    _rows(s, rb, out)


def _hgrn(zd, lbp, g, nb, s, layer):
    kw = HEADS * HGRN_DF
    return pl.pallas_call(
        functools.partial(_hgrn_kernel, s=s, layer=layer),
        grid=(nb,),
        in_specs=[pl.BlockSpec((None, s, ZD_W), lambda b: (b, 0, 0)),
                  _const_spec((DEPTH, 2 * kw)), _const_spec((1, BRANCH_W))],
        out_specs=pl.BlockSpec((None, s, BRANCH_W), lambda b: (b, 0, 0)),
        out_shape=jax.ShapeDtypeStruct((nb, s, BRANCH_W), f32),
        scratch_shapes=[pltpu.VMEM((s, kw), f32), pltpu.VMEM((s, 2 * kw), f32), pltpu.VMEM((s, 2 * kw), f32)]
        + [pltpu.VMEM((s, BRANCH_W), f32)] * 2 + [pltpu.VMEM((BRANCH_W, kw), f32)] * 2,
        compiler_params=_cparams("parallel"),
        name="hgrn",
    )(zd.reshape(nb, s, ZD_W), lbp, g)


def _mlstm_bidir(z_ref, q_ref, k_ref, acc_refs, st_refs, nv_refs, m_refs, *, s):
    c, w = MLSTM_CHUNK, BRANCH_W
    n, un = s // c, MLSTM_UNROLL
    ones = jnp.ones((c, c), bf16)
    lane, row = _iota((c, w), 1), _iota((c, w), 0)
    head, sidx = lane // HEAD_W, lane % HEAD_W
    diag = sidx == row
    same = (_iota((w, w), 0) // HEAD_W) == (_iota((w, w), 1) // HEAD_W)
    bones = jnp.where(same, 1.0, 0.0).astype(bf16)
    tris = (_tri(c, False), _tri(c, True))
    causals = (sidx <= row, sidx >= row)
    for d in range(2):
        st_refs[d][...] = jnp.zeros((w, w), f32)
        nv_refs[d][...] = jnp.zeros((1, w), f32)
        m_refs[d][...] = jnp.zeros((1, w), f32)

    streams = [(d, u) for u in range(un) for d in range(2)]

    def seg_max(x):
        out = jnp.full((c, w), NEG_BIG, f32)
        for j in range(HEADS):
            mj = jnp.max(jnp.where(head == j, x, NEG_BIG), axis=-1, keepdims=True)
            out = jnp.where(head == j, mj, out)
        return out

    def body(ci, carry):
        idx = [(ci * un + u) if d == 0 else (n - 1 - ci * un - u) for d, u in streams]
        sts = [pl.multiple_of(i * c, c) for i in idx]
        qs = [q_ref[pl.ds(st, c), :] for st in sts]
        ks = [k_ref[pl.ds(st, c), :] for st in sts]
        vs = [z_ref[pl.ds(st, c), 512:768] for st in sts]
        igs = [z_ref[pl.ds(st, c), 1024 + d * w:1024 + (d + 1) * w] for (d, _), st in zip(streams, sts)]
        lfs = [jax.nn.log_sigmoid(z_ref[pl.ds(st, c), 1536 + d * w:1536 + (d + 1) * w])
               for (d, _), st in zip(streams, sts)]
        bs = [_lmul3(tris[d], lf) for (d, _), lf in zip(streams, lfs)]
        rs = [_lmul3(ones, jnp.where(diag, ig - b, 0.0)) for ig, b in zip(igs, bs)]
        log_ds = [jnp.where(causals[d], b + r, NEG_BIG) for (d, _), b, r in zip(streams, bs, rs)]
        mrows = [seg_max(x) for x in log_ds]
        lasts = [b[0:1] if d == 1 else b[c - 1:c] for (d, _), b in zip(streams, bs)]
        log_ws = [l - b + ig for l, b, ig in zip(lasts, bs, igs)]
        mws = [jnp.max(x, axis=0, keepdims=True) for x in log_ws]
        qbs = [q.astype(bf16) for q in qs]
        kexps = [jnp.concatenate([jnp.where(head == j, k, 0.0).astype(bf16) for j in range(HEADS)], axis=0)
                 for k in ks]
        vexps = [jnp.concatenate([jnp.where(head == j, v, 0.0).astype(bf16) for j in range(HEADS)], axis=0)
                 for v in vs]
        qks = [_mm_nt(qb, kexp) for qb, kexp in zip(qbs, kexps)]
        ms = [m_ref[...] for m_ref in m_refs]
        m_ts, w_prevs, w_ss, decs = [], [], [], []
        for i, (d, _) in enumerate(streams):
            log_prev = bs[i] + ms[d]
            m_t = jnp.maximum(mrows[i], log_prev)
            m_new = jnp.maximum(lasts[i] + ms[d], mws[i])
            m_ts.append(m_t)
            w_prevs.append(jnp.exp(log_prev - m_t))
            w_ss.append(jnp.exp(log_ws[i] - m_new))
            decs.append(jnp.exp(lasts[i] + ms[d] - m_new))
            ms[d] = m_new
        scs = [qk * jnp.exp(log_d - m_t) for qk, log_d, m_t in zip(qks, log_ds, m_ts)]
        nums = [_mm(sc.astype(bf16), vexp) for sc, vexp in zip(scs, vexps)]
        dens = [_rmul3(sc, bones) for sc in scs]
        upds = [jnp.where(same, _mm_tn((v * w_s).astype(bf16), k.astype(bf16)), 0.0)
                for v, w_s, k in zip(vs, w_ss, ks)]
        nvus = [jnp.sum(w_s * k, axis=0, keepdims=True) for w_s, k in zip(w_ss, ks)]
        states = [st_ref[...] for st_ref in st_refs]
        nvs = [nv_ref[...] for nv_ref in nv_refs]
        for i, (d, _) in enumerate(streams):
            num = w_prevs[i] * _mm_nt(qbs[i], states[d].astype(bf16)) + nums[i]
            den = w_prevs[i] * _rmul3(qs[i] * nvs[d], bones) + dens[i]
            acc_refs[d][pl.ds(sts[i], c), :] = num / jnp.maximum(jnp.abs(den), jnp.exp(-m_ts[i]))
            states[d] = decs[i] * states[d] + upds[i]
            nvs[d] = decs[i] * nvs[d] + nvus[i]
        for d in range(2):
            st_refs[d][...] = states[d]
            nv_refs[d][...] = nvs[d]
            m_refs[d][...] = ms[d]
        return carry

    lax.fori_loop(0, n // un, body, 0)


def _mlstm_kernel(z_ref, cw_ref, cb_ref, g_ref, o_ref, q_ref, k_ref, accf_ref, accb_ref, stf_ref, stb_ref,
                  nvf_ref, nvb_ref, mf_ref, mb_ref, *, s):
    rb = 256
    nblk = s // rb
    w = BRANCH_W

    def conv(st, i):
        x = z_ref[pl.ds(st, rb), 0:2 * w]
        pst = pl.multiple_of(jnp.maximum(st - 8, 0), 8)
        nst = pl.multiple_of(jnp.minimum(st + rb, s - 8), 8)
        prev = z_ref[pl.ds(pst, 8), 0:2 * w][7:8]
        nxt = z_ref[pl.ds(nst, 8), 0:2 * w][0:1]
        prev = jnp.where(i == 0, 0.0, prev)
        nxt = jnp.where(i == nblk - 1, 0.0, nxt)
        row = _iota((rb, 2 * w), 0)
        xm = jnp.where(row == 0, prev, pltpu.roll(x, 1, 0))
        xp = jnp.where(row == rb - 1, nxt, pltpu.roll(x, rb - 1, 0))
        y = cb_ref[...] + xm * cw_ref[0:1, :]
        y = y + x * cw_ref[1:2, :]
        y = y + xp * cw_ref[2:3, :]
        y = y * jax.nn.sigmoid(y)
        q_ref[pl.ds(st, rb), :] = y[:, 0:w]
        k_ref[pl.ds(st, rb), :] = y[:, w:2 * w] * (HEAD_W ** -0.5)
    _rows(s, rb, conv)

    _mlstm_bidir(z_ref, q_ref, k_ref, (accf_ref, accb_ref), (stf_ref, stb_ref), (nvf_ref, nvb_ref),
                 (mf_ref, mb_ref), s=s)

    mavg = _block_avg(w, HEAD_W)

    def out(st, i):
        y = _head_rmsnorm(accf_ref[pl.ds(st, rb), :] + accb_ref[pl.ds(st, rb), :], g_ref[...], mavg)
        o_ref[pl.ds(st, rb), :] = jax.nn.sigmoid(z_ref[pl.ds(st, rb), 768:1024]) * y
    _rows(s, rb, out)


def _mlstm(zb, cw, cb, g, nb, s):
    w = BRANCH_W
    return pl.pallas_call(
        functools.partial(_mlstm_kernel, s=s),
        grid=(nb,),
        in_specs=[pl.BlockSpec((None, s, ZB_W), lambda b: (b, 0, 0)),
                  _const_spec((3, 2 * w)), _const_spec((1, 2 * w)), _const_spec((1, w))],
        out_specs=pl.BlockSpec((None, s, w), lambda b: (b, 0, 0)),
        out_shape=jax.ShapeDtypeStruct((nb, s, w), f32),
        scratch_shapes=[pltpu.VMEM((s, w), f32)] * 4 + [pltpu.VMEM((w, w), f32)] * 2 + [pltpu.VMEM((1, w), f32)] * 4,
        compiler_params=_cparams("parallel"),
        name="mlstm",
    )(zb.reshape(nb, s, ZB_W), cw, cb, g)


def _rope(x, cos, sa, sb):
    w = x.shape[-1]
    half = HEAD_W // 2
    return x * cos + pltpu.roll(x, w - half, 1) * sa + pltpu.roll(x, half, 1) * sb


def _dil_bias(s):
    rel = jnp.arange(s, dtype=i32)[None, :] - jnp.arange(s, dtype=i32)[:, None]
    dist = jnp.abs(rel)
    cnt = jnp.zeros((s, s), f32)
    for window, dil in DIL_PATTERNS:
        ok = dist <= (window // (2 * dil)) * dil
        if dil > 1:
            ok = jnp.logical_and(ok, (rel & (dil - 1)) == 0)
        cnt = cnt + jnp.where(ok, 1.0, 0.0)
    return jnp.where(cnt > 0.0, jnp.log(jnp.maximum(cnt, 1.0)), NEG_BIG)


def _dil_kernel(z_ref, cos_ref, sa_ref, sb_ref, bias_ref, o_ref, kr_ref, vm_ref, *, s, tq):
    w = BRANCH_W
    qi = pl.program_id(1)
    rb = 256

    @pl.when(qi == 0)
    def _():
        head = _iota((rb, w), 1) // HEAD_W

        def prep(st, i):
            x = z_ref[pl.ds(st, rb), w:2 * w]
            kr_ref[pl.ds(st, rb), :] = _rope(x, cos_ref[pl.ds(st, rb), :], sa_ref[pl.ds(st, rb), :],
                                             sb_ref[pl.ds(st, rb), :]).astype(bf16)
            v = z_ref[pl.ds(st, rb), 2 * w:3 * w]
            for j in range(HEADS):
                vm_ref[j, pl.ds(st, rb), :] = jnp.where(head == j, v, 0.0).astype(bf16)
        _rows(s, rb, prep)

    q0 = pl.multiple_of(qi * tq, tq)
    qr = _rope(z_ref[pl.ds(q0, tq), 0:w], cos_ref[pl.ds(q0, tq), :], sa_ref[pl.ds(q0, tq), :],
               sb_ref[pl.ds(q0, tq), :]) * (HEAD_W ** -0.5)
    qhead = _iota((tq, w), 1) // HEAD_W
    acc = jnp.zeros((tq, w), f32)
    for j in range(HEADS):
        qh = jnp.where(qhead == j, qr, 0.0).astype(bf16)
        sc = _mm_nt(qh, kr_ref[...]) + bias_ref[...]
        m = jnp.max(sc, axis=-1, keepdims=True)
        p = jnp.exp(sc - m)
        l = jnp.sum(p, axis=-1, keepdims=True)
        acc = acc + _mm(p.astype(bf16), vm_ref[j]) / l
    o_ref[...] = acc


def _dil(zc, cos, sa, sb, bias, nb, s, tq=256):
    w = BRANCH_W
    for _, dil in DIL_PATTERNS:
        assert dil & (dil - 1) == 0 and s % dil == 0
    return pl.pallas_call(
        functools.partial(_dil_kernel, s=s, tq=tq),
        grid=(nb, s // tq),
        in_specs=[pl.BlockSpec((None, s, ZC_W), lambda b, i: (b, 0, 0)),
                  _const_spec((s, w)), _const_spec((s, w)), _const_spec((s, w)),
                  pl.BlockSpec((tq, s), lambda b, i: (i, 0))],
        out_specs=pl.BlockSpec((None, tq, w), lambda b, i: (b, i, 0)),
        out_shape=jax.ShapeDtypeStruct((nb, s, w), f32),
        scratch_shapes=[pltpu.VMEM((s, w), bf16), pltpu.VMEM((HEADS, s, w), bf16)],
        compiler_params=_cparams("parallel", "arbitrary"),
        name="dilattn",
    )(zc.reshape(nb, s, ZC_W), cos, sa, sb, bias)


def _merge_kernel(h_ref, ya, yb, yc, yd, g_ref, wg_ref, wbr_ref, wo_ref, o_ref):
    x = h_ref[...]
    u = _rms(x, g_ref[...]).astype(bf16)
    merged = None
    for n, y in enumerate((ya, yb, yc, yd)):
        gate = jax.nn.sigmoid(_mm(u, wg_ref[n]))
        term = gate * _mm(y[...].astype(bf16), wbr_ref[n])
        merged = term if merged is None else merged + term
    o_ref[...] = x + _mm(merged.astype(bf16), wo_ref[...])


def _merge(h, ys, g, wg, wbr, wo, tm=512):
    t = h.shape[0]
    tok = lambda w: pl.BlockSpec((tm, w), lambda i: (i, 0))
    return pl.pallas_call(
        _merge_kernel,
        grid=(t // tm,),
        in_specs=[tok(D_MODEL)] + [tok(BRANCH_W)] * 4
        + [_const_spec((1, D_MODEL)), _const_spec((4, D_MODEL, D_MODEL)),
           _const_spec((4, BRANCH_W, D_MODEL)), _const_spec((D_MODEL, D_MODEL))],
        out_specs=tok(D_MODEL),
        out_shape=jax.ShapeDtypeStruct((t, D_MODEL), f32),
        compiler_params=_cparams("parallel"),
        name="merge",
    )(h, *ys, g, wg, wbr, wo)


def _stair_cells():
    return [(a, b) for a in range(PEER_TOPK) for b in range(PEER_TOPK) if (a + 1) * (b + 1) <= PEER_TOPK]


_CELLS = _stair_cells()
_NCELL = len(_CELLS)
_NCELL_PAD = -(-_NCELL // 8) * 8


def _extract_top(works, rowf, nrows):
    works = list(works)
    tops = [[] for _ in works]
    ranks = [jnp.full(w.shape, float(PEER_TOPK), f32) for w in works]
    for r in range(PEER_TOPK):
        mxs = [jnp.max(w, axis=0, keepdims=True) for w in works]
        cands = [jnp.where(w == mx, rowf, float(nrows)) for w, mx in zip(works, mxs)]
        hits = [cand == jnp.min(cand, axis=0, keepdims=True) for cand in cands]
        works = [jnp.where(hit, -jnp.inf, w) for hit, w in zip(hits, works)]
        ranks = [jnp.where(hit, float(r), rank) for hit, rank in zip(hits, ranks)]
        for t, mx in zip(tops, mxs):
            t.append(mx)
    return tops, ranks


def _distinct_maxima(xs):
    tops = [[] for _ in xs]
    mxs = [None] * len(xs)
    for _ in range(PEER_TOPK):
        curs = [x if mx is None else jnp.where(x < mx, x, -jnp.inf) for x, mx in zip(xs, mxs)]
        mxs = [jnp.max(cur, axis=0, keepdims=True) for cur in curs]
        for t, mx in zip(tops, mxs):
            t.append(mx)
    return tops


def _count(mask):
    return jnp.sum(jnp.where(mask, 1.0, 0.0), axis=0, keepdims=True)


def _route_kernel(h_ref, g_ref, wpq_ref, keys_ref, sa_ref, ut_ref, a_ref, cnt_ref, rk_ref, bn_ref,
                  q_ref, m_ref, *, tm):
    uf = _rms(h_ref[...], g_ref[...])
    ut_ref[...] = uf.T.astype(bf16)
    q_ref[...] = _mm(uf.astype(bf16), wpq_ref[...]).astype(bf16)
    nk = PEER_NKEYS
    topk = float(PEER_TOPK)

    def scores(hd, toks):
        return [_mm_nt(keys_ref[hd * 2 + p], q_ref[toks, pl.ds(pl.multiple_of((hd * 2 + p) * nk, nk), nk)])
                for p in range(2)]

    def write_tables(hd, amat, cnt, rank2, bnorm):
        rows = pl.ds(pl.multiple_of(hd * nk, nk), nk)
        prow = pl.ds(pl.multiple_of(hd * (nk // 2), nk // 2), nk // 2)
        rkp = pltpu.bitcast(rank2.astype(bf16), jnp.uint32)
        bnp = pltpu.bitcast(bnorm.astype(bf16), jnp.uint32)
        for tb in range(tm // LANES):
            lanes = slice(tb * LANES, (tb + 1) * LANES)
            a_ref[tb, rows, :] = amat[:, lanes]
            cnt_ref[tb, rows, :] = cnt[:, lanes]
            rk_ref[tb, prow, :] = rkp[:, lanes]
            bn_ref[tb, prow, :] = bnp[:, lanes]

    def stair_sums(tops):
        m_ref[...] = jnp.full((_NCELL_PAD, tm), -jnp.inf, f32)
        for ci, (a, b) in enumerate(_CELLS):
            m_ref[ci:ci + 1, :] = tops[0][a] + tops[1][b]
        return m_ref[...]

    def fast_tile(hd):
        scs = scores(hd, slice(0, tm))
        tops = _distinct_maxima(scs)
        rank2 = jnp.zeros((nk, tm), f32)
        for t in tops[1]:
            rank2 = rank2 + jnp.where(scs[1] < t, 1.0, 0.0)
        tied = jnp.where(_count(scs[0] >= tops[0][-1]) == topk, 0.0, 1.0)
        tied = jnp.where(_count(rank2 < topk) == topk, tied, 1.0)
        cand = stair_sums(tops)
        (ctops,) = _distinct_maxima([cand])
        chosen = cand >= ctops[-1]
        tied = jnp.where(_count(chosen) == topk, tied, 1.0)
        zsum = jnp.sum(jnp.where(chosen, jnp.exp(cand - cand[0:1]), 0.0), axis=0, keepdims=True)
        rowcnt = _mm(sa_ref[...], jnp.where(chosen, 1.0, 0.0).astype(bf16))
        cnt = jnp.zeros((nk, tm), f32)
        for a in range(PEER_TOPK):
            cnt = jnp.where(scs[0] == tops[0][a], rowcnt[a:a + 1], cnt)
        write_tables(hd, jnp.exp(scs[0] - tops[0][0]), cnt, rank2, jnp.exp(scs[1] - tops[1][0]) / zsum)
        return tied

    def exact_tile(hd):
        rowf = _iota((nk, tm), 0).astype(f32)
        rowcf = _iota((_NCELL_PAD, tm), 0).astype(f32)
        scs = scores(hd, slice(0, tm))
        tops, ranks = _extract_top(scs, rowf, nk)
        cand = stair_sums(tops)
        _, (crank,) = _extract_top([cand], rowcf, _NCELL_PAD)
        chosen = crank < topk
        zsum = jnp.sum(jnp.where(chosen, jnp.exp(cand - cand[0:1]), 0.0), axis=0, keepdims=True)
        rowcnt = _mm(sa_ref[...], jnp.where(chosen, 1.0, 0.0).astype(bf16))
        cnt = jnp.zeros((nk, tm), f32)
        for a in range(PEER_TOPK):
            cnt = jnp.where(ranks[0] == float(a), rowcnt[a:a + 1], cnt)
        write_tables(hd, jnp.exp(scs[0] - tops[0][0]), cnt, ranks[1], jnp.exp(scs[1] - tops[1][0]) / zsum)

    def per_head(hd, carry):
        tied = fast_tile(hd)

        @pl.when(jnp.max(tied) > 0.0)
        def _():
            exact_tile(hd)
        return carry

    lax.fori_loop(0, PEER_HEADS, per_head, 0)


def _route(h, g, wpq, keys, sa, tm=256):
    t = h.shape[0]
    nh, nk = PEER_HEADS, PEER_NKEYS
    tab = pl.BlockSpec((tm // LANES, nh * nk, LANES), lambda i: (i, 0, 0))
    ptab = pl.BlockSpec((tm // LANES, nh * nk // 2, LANES), lambda i: (i, 0, 0))
    return pl.pallas_call(
        functools.partial(_route_kernel, tm=tm),
        grid=(t // tm,),
        in_specs=[pl.BlockSpec((tm, D_MODEL), lambda i: (i, 0)), _const_spec((1, D_MODEL)),
                  _const_spec((D_MODEL, 2 * nh * nk)), _const_spec((2 * nh, nk, nk)),
                  _const_spec((PEER_TOPK, _NCELL_PAD))],
        out_specs=[pl.BlockSpec((D_MODEL, tm), lambda i: (0, i)), tab, tab, ptab, ptab],
        out_shape=[jax.ShapeDtypeStruct((D_MODEL, t), bf16)] + [jax.ShapeDtypeStruct((t // LANES, nh * nk, LANES), f32)] * 2
        + [jax.ShapeDtypeStruct((t // LANES, nh * nk // 2, LANES), jnp.uint32)] * 2,
        scratch_shapes=[pltpu.VMEM((tm, 2 * nh * nk), bf16), pltpu.VMEM((_NCELL_PAD, tm), f32)],
        compiler_params=_cparams("parallel"),
        name="route",
    )(h, g, wpq, keys, sa)


def _gelu_bf16(x):
    return (0.5 * x * (1.0 + lax.erf(x * math.sqrt(0.5)))).astype(bf16)


def _peer_gate_block(act_ref, p_ref, tabs, i0, ii, tb):
    a_ref, cnt_ref, rk_ref, bn_ref = tabs
    nk = PEER_NKEYS
    lanes = slice(tb * LANES, (tb + 1) * LANES)
    gate = None
    for hd in range(PEER_HEADS):
        row = pl.ds(hd * nk + i0 + ii, 1)
        a = jnp.broadcast_to(a_ref[tb, row, :], (nk, LANES)).astype(bf16)
        c = jnp.broadcast_to(cnt_ref[tb, row, :], (nk, LANES)).astype(bf16)
        prow = slice(hd * (nk // 2), (hd + 1) * (nk // 2))
        rk = pltpu.bitcast(rk_ref[tb, prow, :], bf16)
        bn = pltpu.bitcast(bn_ref[tb, prow, :], bf16)
        term = a * jnp.where(rk < c, bn, jnp.zeros((), bf16))
        gate = term if gate is None else gate + term
    p_ref[ii * nk:(ii + 1) * nk, lanes] = gate * act_ref[ii * nk:(ii + 1) * nk, lanes]


def _mxu_product(rhs_tile, lhs_tile, between):
    nt = PEER_SUB // MXU_TILE
    for q in range(2):
        pltpu.matmul_push_rhs(rhs_tile(0, q), staging_register=0, mxu_index=q)
    for kt in range(nt):
        reg = kt % 2
        for mc in range(nt):
            lhs = lhs_tile(mc, kt)
            for q in range(2):
                pltpu.matmul_acc_lhs(mc * (MXU_TILE // 4), lhs, mxu_index=q,
                                     load_staged_rhs=(reg if mc == 0 else None))
            if mc == 0 and kt + 1 < nt:
                for q in range(2):
                    pltpu.matmul_push_rhs(rhs_tile(kt + 1, q), staging_register=1 - reg, mxu_index=q)
            between()


def _peer_kernel(ut_ref, a_ref, cnt_ref, rk_ref, bn_ref, pu_ref, pvt_ref, o_ref, acc_ref, act0, act1, p0, p1,
                 *, ne, tp):
    g = pl.program_id(0)
    nblocks = pl.num_programs(0) - 2
    b1 = jnp.clip(g - 1, 0, nblocks - 1)
    b2 = jnp.clip(g - 2, 0, nblocks - 1)
    i1 = (b1 % ne) * (PEER_SUB // PEER_NKEYS)
    tabs = (a_ref, cnt_ref, rk_ref, bn_ref)

    @pl.when(g == 0)
    def _():
        for ref in (act0, act1, p0, p1):
            ref[...] = jnp.zeros(ref.shape, bf16)

    @pl.when(b2 % ne == 0)
    def _():
        acc_ref[...] = jnp.zeros(acc_ref.shape, f32)

    t256 = lambda i: slice(i * MXU_TILE, (i + 1) * MXU_TILE)

    def stage(act_w, act_r, p_w, p_r):
        blocks = iter([(ii, tb) for ii in range(PEER_SUB // PEER_NKEYS) for tb in range(tp // LANES)])

        def one_gate_block():
            ii, tb = next(blocks)
            _peer_gate_block(act_r, p_w, tabs, i1, ii, tb)

        nt = PEER_SUB // MXU_TILE
        _mxu_product(lambda kt, q: ut_ref[t256(kt), t256(q)], lambda mc, kt: pu_ref[t256(mc), t256(kt)],
                     one_gate_block)
        for mc in range(nt):
            for q in range(2):
                x = pltpu.matmul_pop(mc * (MXU_TILE // 4), (MXU_TILE, MXU_TILE), f32, q)
                act_w[t256(mc), t256(q)] = _gelu_bf16(x)
        _mxu_product(lambda kt, q: p_r[t256(kt), t256(q)], lambda mc, kt: pvt_ref[t256(mc), t256(kt)],
                     one_gate_block)
        for mc in range(nt):
            for q in range(2):
                acc_ref[t256(mc), t256(q)] += pltpu.matmul_pop(mc * (MXU_TILE // 4), (MXU_TILE, MXU_TILE), f32, q)

    @pl.when(g % 2 == 0)
    def _():
        stage(act0, act1, p1, p0)

    @pl.when(g % 2 == 1)
    def _():
        stage(act1, act0, p0, p1)

    @pl.when(jnp.logical_and(g >= 2, b2 % ne == ne - 1))
    def _():
        o_ref[...] = acc_ref[...].T


def _peer(ut, tabs, pu, pvt):
    t = ut.shape[1]
    tp = 2 * MXU_TILE
    nh, nk = PEER_HEADS, PEER_NKEYS
    assert D_MODEL == PEER_SUB and (PEER_SUB // nk) * (tp // LANES) == 2 * (PEER_SUB // MXU_TILE) ** 2
    ne = pu.shape[0] // PEER_SUB
    nblocks = (t // tp) * ne
    blk = lambda lag: (lambda g: jnp.clip(g - lag, 0, nblocks - 1))
    b0, b1, b2 = blk(0), blk(1), blk(2)
    tab = pl.BlockSpec((tp // LANES, nh * nk, LANES), lambda g: (b1(g) // ne, 0, 0))
    ptab = pl.BlockSpec((tp // LANES, nh * nk // 2, LANES), lambda g: (b1(g) // ne, 0, 0))
    return pl.pallas_call(
        functools.partial(_peer_kernel, ne=ne, tp=tp),
        grid=(nblocks + 2,),
        in_specs=[pl.BlockSpec((D_MODEL, tp), lambda g: (0, b0(g) // ne)), tab, tab, ptab, ptab,
                  pl.BlockSpec((PEER_SUB, D_MODEL), lambda g: (b0(g) % ne, 0)),
                  pl.BlockSpec((D_MODEL, PEER_SUB), lambda g: (0, b2(g) % ne))],
        out_specs=pl.BlockSpec((tp, D_MODEL), lambda g: (b2(g) // ne, 0)),
        out_shape=jax.ShapeDtypeStruct((t, D_MODEL), f32),
        scratch_shapes=[pltpu.VMEM((D_MODEL, tp), f32)] + [pltpu.VMEM((PEER_SUB, tp), bf16)] * 4,
        compiler_params=_cparams("arbitrary"),
        name="peer",
    )(ut, *tabs, pu, pvt)


def _pe_kernel(h_ref, po_ref, p_ref, g_ref, wpg_ref, wpe_ref, gf_ref, o_ref, *, final):
    h = h_ref[...] + po_ref[...]
    gate = jax.nn.sigmoid(_mm(_rms(h, g_ref[...]).astype(bf16), wpg_ref[...]))
    h = h + gate * _mm(p_ref[...].astype(bf16), wpe_ref[...])
    if final:
        h = _rms(h, gf_ref[...])
    o_ref[...] = h


def _pe(h, po, p, g, wpg, wpe, gf, final, tm=512):
    t = h.shape[0]
    tok = lambda w: pl.BlockSpec((tm, w), lambda i: (i, 0))
    return pl.pallas_call(
        functools.partial(_pe_kernel, final=final),
        grid=(t // tm,),
        in_specs=[tok(D_MODEL), tok(D_MODEL), tok(PE_DIM), _const_spec((1, D_MODEL)),
                  _const_spec((D_MODEL, D_MODEL)), _const_spec((PE_DIM, D_MODEL)), _const_spec((1, D_MODEL))],
        out_specs=tok(D_MODEL),
        out_shape=jax.ShapeDtypeStruct((t, D_MODEL), f32),
        compiler_params=_cparams("parallel"),
        name="pe_embed",
    )(h, po, p, g, wpg, wpe, gf)


def _cols(w, name):
    a, b = _PROJ_OFF[name]
    return w[..., a:b]


def _proj_groups(w):
    pad = jnp.zeros(w.shape[:-1] + (ZA_W - 800,), w.dtype)
    ga = jnp.concatenate([_cols(w, n) for n in ('a_q', 'a_k', 'a_v', 'a_g', 'a_lr')] + [pad], axis=-1)
    gates = [jnp.repeat(_cols(w, n), HEAD_W, axis=-1) for n in ('b_i', 'b_f')]
    gb = jnp.concatenate([_cols(w, n) for n in ('b_q', 'b_k', 'b_v', 'b_o')] + gates, axis=-1)
    gc = jnp.concatenate([_cols(w, n) for n in ('c_q', 'c_k', 'c_v')], axis=-1)
    gd = jnp.concatenate([_cols(w, n) for n in ('d_q', 'd_f', 'd_i', 'd_g')], axis=-1)
    return ga, gb, gc, gd


def _rope_tables(s):
    half = HEAD_W // 2
    inv = ROPE_THETA ** (-jnp.arange(half, dtype=f32) / half)
    ang = jnp.arange(s, dtype=f32)[:, None] * inv[None, :]
    cos, sin = jnp.cos(ang), jnp.sin(ang)
    zero = jnp.zeros_like(sin)
    tile = lambda a, b: jnp.tile(jnp.concatenate([a, b], axis=-1), (1, HEADS))
    return tile(cos, cos), tile(-sin, zero), tile(zero, sin)


def _row_count_matrix():
    m = np.zeros((PEER_TOPK, _NCELL_PAD), np.float32)
    for ci, (a, _) in enumerate(_CELLS):
        m[a, ci] = 1.0
    return jnp.asarray(m, bf16)


def _prepare(g_mix, w_in, b_in, w_lr2, b_lr2, conv_w, conv_b, lb_param, g_a, g_b, g_d, w_gate, w_br, w_out,
             g_ffn, w_pq, sub_keys, peer_u, peer_v, g_pe, w_pe, w_pg, g_final):
    layers = []
    row = lambda a: a.reshape(1, -1)
    for l in range(DEPTH):
        wlr = jnp.zeros((LANES, 256), f32)
        for n in range(2):
            wlr = wlr.at[n * GLA_RANK:(n + 1) * GLA_RANK, n * 128:(n + 1) * 128].set(w_lr2[l, n])
        layers.append(dict(
            g_mix=row(g_mix[l]),
            w_proj=[w.astype(bf16) for w in _proj_groups(w_in[l])],
            b_proj=list(_proj_groups(row(b_in[l]))),
            wlr=wlr.astype(bf16), blr=row(b_lr2[l]),
            conv_w=conv_w[l], conv_b=row(conv_b[l]),
            g_a=row(g_a[l]), g_b=row(g_b[l]), g_d=row(g_d[l]),
            w_gate=w_gate[l].astype(bf16), w_br=w_br[l].astype(bf16), w_out=w_out[l].astype(bf16),
            g_ffn=row(g_ffn[l]), w_pq=w_pq[l].astype(bf16),
            keys=sub_keys[l].reshape(2 * PEER_HEADS, PEER_NKEYS, -1).astype(bf16),
            peer_u=peer_u[l].astype(bf16), peer_vt=peer_v[l].astype(bf16).T,
            g_pe=row(g_pe[l]), w_pe=w_pe[l].astype(bf16), w_pg=w_pg[l].astype(bf16),
        ))
    shared = dict(
        lbp=jnp.transpose(lb_param, (1, 0, 2)).reshape(DEPTH, -1),
        g_final=row(g_final), sa=_row_count_matrix(),
    )
    return layers, shared


def _trunk(x, p, layers, shared):
    nb, s, d = x.shape
    t = nb * s
    h = x.reshape(t, d)
    cos, sa, sb = _rope_tables(s)
    dil_bias = _dil_bias(s)
    for l, w in enumerate(layers):
        za, zb, zc, zd = _proj(h, w['g_mix'], w['w_proj'], w['b_proj'])
        ya = _gla(za, w['wlr'], w['blr'], w['g_a'], nb, s)
        yb = _mlstm(zb, w['conv_w'], w['conv_b'], w['g_b'], nb, s)
        yc = _dil(zc, cos, sa, sb, dil_bias, nb, s)
        yd = _hgrn(zd, shared['lbp'], w['g_d'], nb, s, l)
        ys = [y.reshape(t, BRANCH_W) for y in (ya, yb, yc, yd)]
        h1 = _merge(h, ys, w['g_mix'], w['w_gate'], w['w_br'], w['w_out'])
        ut, *tabs = _route(h1, w['g_ffn'], w['w_pq'], w['keys'], shared['sa'])
        po = _peer(ut, tabs, w['peer_u'], w['peer_vt'])
        h = _pe(h1, po, p[l].reshape(t, PE_DIM), w['g_pe'], w['w_pg'], w['w_pe'], shared['g_final'],
                final=(l == DEPTH - 1))
    return h.reshape(nb, s, d)


def kernel(x_prompt, x_sample, p_prompt, p_sample, g_mix, w_in, b_in, w_lr2, b_lr2, conv_w, conv_b, lb_param, g_a, g_b, g_d, w_gate, w_br, w_out, g_ffn, w_pq, sub_keys, peer_u, peer_v, g_pe, w_pe, w_pg, g_final):
    layers, shared = _prepare(g_mix, w_in, b_in, w_lr2, b_lr2, conv_w, conv_b, lb_param, g_a, g_b, g_d, w_gate,
                              w_br, w_out, g_ffn, w_pq, sub_keys, peer_u, peer_v, g_pe, w_pe, w_pg, g_final)
    return (_trunk(x_prompt, p_prompt, layers, shared), _trunk(x_sample, p_sample, layers, shared))
```

```python
import functools
import math

import numpy as np
import jax
import jax.numpy as jnp
from jax import lax
from jax.experimental import pallas as pl
from jax.experimental.pallas import tpu as pltpu

f32 = jnp.float32
bf16 = jnp.bfloat16
i32 = jnp.int32

D_MODEL = 1024
DEPTH = 2
PE_DIM = 256
EPS = 1e-6
NEG_BIG = -1e30
HEADS = 4
HEAD_W = 64
BRANCH_W = HEADS * HEAD_W
GLA_DK = 32
GLA_RANK = 16
GLA_NORMALIZER = 16.0
GLA_CHUNK = 32
MLSTM_CHUNK = 64
HGRN_DF = 64
HGRN_CHUNK = 32
DIL_PATTERNS = ((128, 1), (512, 4), (2048, 16))
ROPE_THETA = 10000.0
PEER_HEADS = 8
PEER_NKEYS = 128
PEER_TOPK = 16
LANES = 128
PEER_SUB = 1024
MXU_TILE = 256
ROUTE_HEADS_TOGETHER = 2
EXP_CLAMP = 80.0
GL_UNROLL = 4
MLSTM_UNROLL = 2

_PROJ_LAYOUT = (
    ('a_q', 128), ('a_k', 128), ('a_v', 256), ('a_g', 256), ('a_lr', 32),
    ('b_q', 256), ('b_k', 256), ('b_v', 256), ('b_i', 8), ('b_f', 8), ('b_o', 256),
    ('c_q', 256), ('c_k', 256), ('c_v', 256),
    ('d_q', 256), ('d_f', 512), ('d_i', 256), ('d_g', 256),
)
_PROJ_OFF = {}
_o = 0
for _n, _s in _PROJ_LAYOUT:
    _PROJ_OFF[_n] = (_o, _o + _s)
    _o += _s

ZA_W = 896
ZB_W = 2048
ZC_W = 768
ZD_W = 1280

VMEM_LIMIT = 56 * 1024 * 1024


def _cparams(*sem):
    return pltpu.CompilerParams(dimension_semantics=sem, vmem_limit_bytes=VMEM_LIMIT)


def _const_spec(shape):
    nd = len(shape)
    return pl.BlockSpec(shape, lambda *_: (0,) * nd, pipeline_mode=pl.Buffered(1))


def _mm(a, b):
    return jnp.dot(a, b, preferred_element_type=f32)


def _mm_nt(a, b):
    return lax.dot_general(a, b, (((1,), (1,)), ((), ())), preferred_element_type=f32)


def _mm_tn(a, b):
    return lax.dot_general(a, b, (((0,), (0,)), ((), ())), preferred_element_type=f32)


def _split3(x):
    hi = x.astype(bf16)
    r1 = x - hi.astype(f32)
    mid = r1.astype(bf16)
    lo = (r1 - mid.astype(f32)).astype(bf16)
    return hi, mid, lo


def _lmul3(m, x):
    w = x.shape[1]
    y = _mm(m, jnp.concatenate(_split3(x), axis=1))
    return y[:, 0:w] + y[:, w:2 * w] + y[:, 2 * w:3 * w]


def _rmul3(x, m):
    r = x.shape[0]
    y = _mm(jnp.concatenate(_split3(x), axis=0), m)
    return y[0:r] + y[r:2 * r] + y[2 * r:3 * r]


def _rms(x, g):
    return x * lax.rsqrt(jnp.mean(x * x, axis=-1, keepdims=True) + EPS) * g


def _iota(shape, dim):
    return lax.broadcasted_iota(i32, shape, dim)


def _tri(c, reverse):
    r, k = _iota((c, c), 0), _iota((c, c), 1)
    return jnp.where((k >= r) if reverse else (k <= r), 1.0, 0.0).astype(bf16)


def _block_avg(w, blk):
    same = (_iota((w, w), 0) // blk) == (_iota((w, w), 1) // blk)
    return jnp.where(same, 1.0 / blk, 0.0).astype(bf16)


def _rows(n, blk, fn):
    def body(i, c):
        fn(pl.multiple_of(i * blk, blk), i)
        return c
    lax.fori_loop(0, n // blk, body, 0)


def _head_rmsnorm(o, g, mavg):
    ms = _rmul3(o * o, mavg)
    return o * lax.rsqrt(ms + EPS) * g


def _proj_kernel(h_ref, g_ref, wa, wb, wc, wd, ba, bb, bc, bd, za, zb, zc, zd):
    u = _rms(h_ref[...], g_ref[...]).astype(bf16)
    for w, b, z in ((wa, ba, za), (wb, bb, zb), (wc, bc, zc), (wd, bd, zd)):
        z[...] = _mm(u, w[...]) + b[...]


def _proj(h, g, ws, bs, tm=512):
    t = h.shape[0]
    widths = (ZA_W, ZB_W, ZC_W, ZD_W)
    tok = lambda w: pl.BlockSpec((tm, w), lambda i: (i, 0))
    return pl.pallas_call(
        _proj_kernel,
        grid=(t // tm,),
        in_specs=[tok(D_MODEL), _const_spec((1, D_MODEL))]
        + [_const_spec((D_MODEL, w)) for w in widths] + [_const_spec((1, w)) for w in widths],
        out_specs=[tok(w) for w in widths],
        out_shape=[jax.ShapeDtypeStruct((t, w), f32) for w in widths],
        compiler_params=_cparams("parallel"),
        name="proj",
    )(h, g, *ws, *bs)


def _gl_bidir(loads, acc_refs, st_refs, *, s, c, dk):
    h, dv = HEADS, HEAD_W
    kw, vw, hc = h * dk, h * dv, h * c
    n, un = s // c, GL_UNROLL
    khead = _iota((c, kw), 1) // dk
    vhead = _iota((c, vw), 1) // dv
    sidx, trow = _iota((c, hc), 1) % c, _iota((c, hc), 0)
    bd = (_iota((vw, kw), 0) // dv) == (_iota((vw, kw), 1) // dk)
    tris = (_tri(c, False), _tri(c, True))
    causals = (sidx <= trow, sidx >= trow)
    for st_ref in st_refs:
        st_ref[...] = jnp.zeros((vw, kw), f32)

    streams = [(d, u) for u in range(un) for d in range(2)]

    def body(ci, carry):
        idx = [(ci * un + u) if d == 0 else (n - 1 - ci * un - u) for d, u in streams]
        sts = [pl.multiple_of(i * c, c) for i in idx]
        ins = [loads[d](st) for (d, _), st in zip(streams, sts)]
        qs, ks, vs = [x[0] for x in ins], [x[1] for x in ins], [x[2] for x in ins]
        bs = [_lmul3(tris[d], x[3]) for (d, _), x in zip(streams, ins)]
        lasts = [b[0:1] if d == 1 else b[c - 1:c] for (d, _), b in zip(streams, bs)]
        mids = [b[c // 2:c // 2 + 1] for b in bs]
        qts = [(q * jnp.exp(jnp.minimum(b - m, EXP_CLAMP))).astype(bf16) for q, b, m in zip(qs, bs, mids)]
        kts = [k * jnp.exp(jnp.minimum(m - b, EXP_CLAMP)) for k, b, m in zip(ks, bs, mids)]
        kexps = [jnp.concatenate([jnp.where(khead == j, kt, 0.0).astype(bf16) for j in range(h)], axis=0)
                 for kt in kts]
        vexps = [jnp.concatenate([jnp.where(vhead == j, v, 0.0).astype(bf16) for j in range(h)], axis=0)
                 for v in vs]
        atts = [jnp.where(causals[d], _mm_nt(qt, kexp), 0.0).astype(bf16)
                for (d, _), qt, kexp in zip(streams, qts, kexps)]
        os_ = [_mm(att, vexp) for att, vexp in zip(atts, vexps)]
        qis = [(q * jnp.exp(b)).astype(bf16) for q, b in zip(qs, bs)]
        kds = [(k * jnp.exp(l - b)).astype(bf16) for k, l, b in zip(ks, lasts, bs)]
        upds = [jnp.where(bd, _mm_tn(v.astype(bf16), kd), 0.0) for v, kd in zip(vs, kds)]
        decs = [jnp.exp(l) for l in lasts]
        states = [st_ref[...] for st_ref in st_refs]
        for i, (d, _) in enumerate(streams):
            acc_refs[d][pl.ds(sts[i], c), :] = os_[i] + _mm_nt(qis[i], states[d].astype(bf16))
            states[d] = states[d] * decs[i] + upds[i]
        for d in range(2):
            st_refs[d][...] = states[d]
        return carry

    lax.fori_loop(0, n // un, body, 0)


def _gla_kernel(z_ref, wlr_ref, blr_ref, g_ref, o_ref, gk_ref, accf_ref, accb_ref, stf_ref, stb_ref, *, s):
    rb = 256

    def gates(st, i):
        lr = z_ref[pl.ds(st, rb), 768:896].astype(bf16)
        x = _mm(lr, wlr_ref[...]) + blr_ref[...]
        gk_ref[pl.ds(st, rb), :] = jax.nn.log_sigmoid(x) * (1.0 / GLA_NORMALIZER)
    _rows(s, rb, gates)

    scale = GLA_DK ** -0.5

    def load(st, d):
        q = z_ref[pl.ds(st, GLA_CHUNK), 0:128] * scale
        k = z_ref[pl.ds(st, GLA_CHUNK), 128:256]
        v = z_ref[pl.ds(st, GLA_CHUNK), 256:512]
        g = gk_ref[pl.ds(st, GLA_CHUNK), d * 128:(d + 1) * 128]
        return q, k, v, g
    _gl_bidir([functools.partial(load, d=d) for d in range(2)], (accf_ref, accb_ref), (stf_ref, stb_ref),
              s=s, c=GLA_CHUNK, dk=GLA_DK)

    mavg = _block_avg(BRANCH_W, HEAD_W)

    def out(st, i):
        y = _head_rmsnorm(accf_ref[pl.ds(st, rb), :] + accb_ref[pl.ds(st, rb), :], g_ref[...], mavg)
        gate = z_ref[pl.ds(st, rb), 512:768]
        o_ref[pl.ds(st, rb), :] = y * (gate * jax.nn.sigmoid(gate))
    _rows(s, rb, out)


def _gla(za, wlr, blr, g, nb, s):
    return pl.pallas_call(
        functools.partial(_gla_kernel, s=s),
        grid=(nb,),
        in_specs=[pl.BlockSpec((None, s, ZA_W), lambda b: (b, 0, 0)),
                  _const_spec((LANES, 256)), _const_spec((1, 256)), _const_spec((1, BRANCH_W))],
        out_specs=pl.BlockSpec((None, s, BRANCH_W), lambda b: (b, 0, 0)),
        out_shape=jax.ShapeDtypeStruct((nb, s, BRANCH_W), f32),
        scratch_shapes=[pltpu.VMEM((s, 256), f32)] + [pltpu.VMEM((s, BRANCH_W), f32)] * 2
        + [pltpu.VMEM((BRANCH_W, HEADS * GLA_DK), f32)] * 2,
        compiler_params=_cparams("parallel"),
        name="gla",
    )(za.reshape(nb, s, ZA_W), wlr, blr, g)


def _hgrn_kernel(z_ref, lbp_ref, g_ref, o_ref, q_ref, lf_ref, kd_ref, accf_ref, accb_ref, stf_ref, stb_ref,
                 *, s, layer):
    rb = 256
    x = lbp_ref[...]
    e = jnp.exp(x - jnp.max(x, axis=0, keepdims=True))
    p = e / jnp.sum(e, axis=0, keepdims=True)
    lb = p[0:1]
    for j in range(1, layer + 1):
        lb = lb + p[j:j + 1]
    lb = lb - p[0:1]

    def prep(st, i):
        qz = z_ref[pl.ds(st, rb), 0:256]
        q_ref[pl.ds(st, rb), :] = qz * jax.nn.sigmoid(qz)
        fz = z_ref[pl.ds(st, rb), 256:768]
        lf_ref[pl.ds(st, rb), :] = jnp.log(lb + (1.0 - lb) * jax.nn.sigmoid(fz))
        kd_ref[pl.ds(st, rb), :] = (1.0 - lb) * jax.nn.sigmoid(-fz)
    _rows(s, rb, prep)

    kw = HEADS * HGRN_DF

    def load(st, d):
        q = q_ref[pl.ds(st, HGRN_CHUNK), :]
        k = kd_ref[pl.ds(st, HGRN_CHUNK), d * kw:(d + 1) * kw]
        v = z_ref[pl.ds(st, HGRN_CHUNK), 768:1024]
        g = lf_ref[pl.ds(st, HGRN_CHUNK), d * kw:(d + 1) * kw]
        return q, k, v, g
    _gl_bidir([functools.partial(load, d=d) for d in range(2)], (accf_ref, accb_ref), (stf_ref, stb_ref),
              s=s, c=HGRN_CHUNK, dk=HGRN_DF)

    mavg = _block_avg(BRANCH_W, HEAD_W)

    def out(st, i):
        y = _head_rmsnorm(accf_ref[pl.ds(st, rb), :] + accb_ref[pl.ds(st, rb), :], g_ref[...], mavg)
        gate = z_ref[pl.ds(st, rb), 1024:1280]
        o_ref[pl.ds(st, rb), :] = y * (gate * jax.nn.sigmoid(gate))
    _rows(s, rb, out)


def _hgrn(zd, lbp, g, nb, s, layer):
    kw = HEADS * HGRN_DF
    return pl.pallas_call(
        functools.partial(_hgrn_kernel, s=s, layer=layer),
        grid=(nb,),
        in_specs=[pl.BlockSpec((None, s, ZD_W), lambda b: (b, 0, 0)),
                  _const_spec((DEPTH, 2 * kw)), _const_spec((1, BRANCH_W))],
        out_specs=pl.BlockSpec((None, s, BRANCH_W), lambda b: (b, 0, 0)),
        out_shape=jax.ShapeDtypeStruct((nb, s, BRANCH_W), f32),
        scratch_shapes=[pltpu.VMEM((s, kw), f32), pltpu.VMEM((s, 2 * kw), f32), pltpu.VMEM((s, 2 * kw), f32)]
        + [pltpu.VMEM((s, BRANCH_W), f32)] * 2 + [pltpu.VMEM((BRANCH_W, kw), f32)] * 2,
        compiler_params=_cparams("parallel"),
        name="hgrn",
    )(zd.reshape(nb, s, ZD_W), lbp, g)


def _mlstm_bidir(z_ref, q_ref, k_ref, acc_refs, st_refs, nv_refs, m_refs, *, s):
    c, w = MLSTM_CHUNK, BRANCH_W
    n, un = s // c, MLSTM_UNROLL
    ones = jnp.ones((c, c), bf16)
    lane, row = _iota((c, w), 1), _iota((c, w), 0)
    head, sidx = lane // HEAD_W, lane % HEAD_W
    diag = sidx == row
    same = (_iota((w, w), 0) // HEAD_W) == (_iota((w, w), 1) // HEAD_W)
    bones = jnp.where(same, 1.0, 0.0).astype(bf16)
    tris = (_tri(c, False), _tri(c, True))
    causals = (sidx <= row, sidx >= row)
    for d in range(2):
        st_refs[d][...] = jnp.zeros((w, w), f32)
        nv_refs[d][...] = jnp.zeros((1, w), f32)
        m_refs[d][...] = jnp.zeros((1, w), f32)

    streams = [(d, u) for u in range(un) for d in range(2)]

    def seg_max(x):
        out = jnp.full((c, w), NEG_BIG, f32)
        for j in range(HEADS):
            mj = jnp.max(jnp.where(head == j, x, NEG_BIG), axis=-1, keepdims=True)
            out = jnp.where(head == j, mj, out)
        return out

    def body(ci, carry):
        idx = [(ci * un + u) if d == 0 else (n - 1 - ci * un - u) for d, u in streams]
        sts = [pl.multiple_of(i * c, c) for i in idx]
        qs = [q_ref[pl.ds(st, c), :] for st in sts]
        ks = [k_ref[pl.ds(st, c), :] for st in sts]
        vs = [z_ref[pl.ds(st, c), 512:768] for st in sts]
        igs = [z_ref[pl.ds(st, c), 1024 + d * w:1024 + (d + 1) * w] for (d, _), st in zip(streams, sts)]
        lfs = [jax.nn.log_sigmoid(z_ref[pl.ds(st, c), 1536 + d * w:1536 + (d + 1) * w])
               for (d, _), st in zip(streams, sts)]
        bs = [_lmul3(tris[d], lf) for (d, _), lf in zip(streams, lfs)]
        rs = [_lmul3(ones, jnp.where(diag, ig - b, 0.0)) for ig, b in zip(igs, bs)]
        log_ds = [jnp.where(causals[d], b + r, NEG_BIG) for (d, _), b, r in zip(streams, bs, rs)]
        mrows = [seg_max(x) for x in log_ds]
        lasts = [b[0:1] if d == 1 else b[c - 1:c] for (d, _), b in zip(streams, bs)]
        log_ws = [l - b + ig for l, b, ig in zip(lasts, bs, igs)]
        mws = [jnp.max(x, axis=0, keepdims=True) for x in log_ws]
        qbs = [q.astype(bf16) for q in qs]
        kexps = [jnp.concatenate([jnp.where(head == j, k, 0.0).astype(bf16) for j in range(HEADS)], axis=0)
                 for k in ks]
        vexps = [jnp.concatenate([jnp.where(head == j, v, 0.0).astype(bf16) for j in range(HEADS)], axis=0)
                 for v in vs]
        qks = [_mm_nt(qb, kexp) for qb, kexp in zip(qbs, kexps)]
        ms = [m_ref[...] for m_ref in m_refs]
        m_ts, w_prevs, w_ss, decs = [], [], [], []
        for i, (d, _) in enumerate(streams):
            log_prev = bs[i] + ms[d]
            m_t = jnp.maximum(mrows[i], log_prev)
            m_new = jnp.maximum(lasts[i] + ms[d], mws[i])
            m_ts.append(m_t)
            w_prevs.append(jnp.exp(log_prev - m_t))
            w_ss.append(jnp.exp(log_ws[i] - m_new))
            decs.append(jnp.exp(lasts[i] + ms[d] - m_new))
            ms[d] = m_new
        scs = [qk * jnp.exp(log_d - m_t) for qk, log_d, m_t in zip(qks, log_ds, m_ts)]
        nums = [_mm(sc.astype(bf16), vexp) for sc, vexp in zip(scs, vexps)]
        dens = [_rmul3(sc, bones) for sc in scs]
        upds = [jnp.where(same, _mm_tn((v * w_s).astype(bf16), k.astype(bf16)), 0.0)
                for v, w_s, k in zip(vs, w_ss, ks)]
        nvus = [jnp.sum(w_s * k, axis=0, keepdims=True) for w_s, k in zip(w_ss, ks)]
        states = [st_ref[...] for st_ref in st_refs]
        nvs = [nv_ref[...] for nv_ref in nv_refs]
        for i, (d, _) in enumerate(streams):
            num = w_prevs[i] * _mm_nt(qbs[i], states[d].astype(bf16)) + nums[i]
            den = w_prevs[i] * _rmul3(qs[i] * nvs[d], bones) + dens[i]
            acc_refs[d][pl.ds(sts[i], c), :] = num / jnp.maximum(jnp.abs(den), jnp.exp(-m_ts[i]))
            states[d] = decs[i] * states[d] + upds[i]
            nvs[d] = decs[i] * nvs[d] + nvus[i]
        for d in range(2):
            st_refs[d][...] = states[d]
            nv_refs[d][...] = nvs[d]
            m_refs[d][...] = ms[d]
        return carry

    lax.fori_loop(0, n // un, body, 0)


def _mlstm_kernel(z_ref, cw_ref, cb_ref, g_ref, o_ref, q_ref, k_ref, accf_ref, accb_ref, stf_ref, stb_ref,
                  nvf_ref, nvb_ref, mf_ref, mb_ref, *, s):
    rb = 256
    nblk = s // rb
    w = BRANCH_W

    def conv(st, i):
        x = z_ref[pl.ds(st, rb), 0:2 * w]
        pst = pl.multiple_of(jnp.maximum(st - 8, 0), 8)
        nst = pl.multiple_of(jnp.minimum(st + rb, s - 8), 8)
        prev = z_ref[pl.ds(pst, 8), 0:2 * w][7:8]
        nxt = z_ref[pl.ds(nst, 8), 0:2 * w][0:1]
        prev = jnp.where(i == 0, 0.0, prev)
        nxt = jnp.where(i == nblk - 1, 0.0, nxt)
        row = _iota((rb, 2 * w), 0)
        xm = jnp.where(row == 0, prev, pltpu.roll(x, 1, 0))
        xp = jnp.where(row == rb - 1, nxt, pltpu.roll(x, rb - 1, 0))
        y = cb_ref[...] + xm * cw_ref[0:1, :]
        y = y + x * cw_ref[1:2, :]
        y = y + xp * cw_ref[2:3, :]
        y = y * jax.nn.sigmoid(y)
        q_ref[pl.ds(st, rb), :] = y[:, 0:w]
        k_ref[pl.ds(st, rb), :] = y[:, w:2 * w] * (HEAD_W ** -0.5)
    _rows(s, rb, conv)

    _mlstm_bidir(z_ref, q_ref, k_ref, (accf_ref, accb_ref), (stf_ref, stb_ref), (nvf_ref, nvb_ref),
                 (mf_ref, mb_ref), s=s)

    mavg = _block_avg(w, HEAD_W)

    def out(st, i):
        y = _head_rmsnorm(accf_ref[pl.ds(st, rb), :] + accb_ref[pl.ds(st, rb), :], g_ref[...], mavg)
        o_ref[pl.ds(st, rb), :] = jax.nn.sigmoid(z_ref[pl.ds(st, rb), 768:1024]) * y
    _rows(s, rb, out)


def _mlstm(zb, cw, cb, g, nb, s):
    w = BRANCH_W
    return pl.pallas_call(
        functools.partial(_mlstm_kernel, s=s),
        grid=(nb,),
        in_specs=[pl.BlockSpec((None, s, ZB_W), lambda b: (b, 0, 0)),
                  _const_spec((3, 2 * w)), _const_spec((1, 2 * w)), _const_spec((1, w))],
        out_specs=pl.BlockSpec((None, s, w), lambda b: (b, 0, 0)),
        out_shape=jax.ShapeDtypeStruct((nb, s, w), f32),
        scratch_shapes=[pltpu.VMEM((s, w), f32)] * 4 + [pltpu.VMEM((w, w), f32)] * 2 + [pltpu.VMEM((1, w), f32)] * 4,
        compiler_params=_cparams("parallel"),
        name="mlstm",
    )(zb.reshape(nb, s, ZB_W), cw, cb, g)


def _rope(x, cos, sa, sb):
    w = x.shape[-1]
    half = HEAD_W // 2
    return x * cos + pltpu.roll(x, w - half, 1) * sa + pltpu.roll(x, half, 1) * sb


def _dil_bias(s):
    rel = jnp.arange(s, dtype=i32)[None, :] - jnp.arange(s, dtype=i32)[:, None]
    dist = jnp.abs(rel)
    cnt = jnp.zeros((s, s), f32)
    for window, dil in DIL_PATTERNS:
        ok = dist <= (window // (2 * dil)) * dil
        if dil > 1:
            ok = jnp.logical_and(ok, (rel & (dil - 1)) == 0)
        cnt = cnt + jnp.where(ok, 1.0, 0.0)
    return jnp.where(cnt > 0.0, jnp.log(jnp.maximum(cnt, 1.0)), NEG_BIG)


def _dil_kernel(z_ref, cos_ref, sa_ref, sb_ref, bias_ref, o_ref, kr_ref, vm_ref, *, s, tq):
    w = BRANCH_W
    qi = pl.program_id(1)
    rb = 256

    @pl.when(qi == 0)
    def _():
        head = _iota((rb, w), 1) // HEAD_W

        def prep(st, i):
            x = z_ref[pl.ds(st, rb), w:2 * w]
            kr_ref[pl.ds(st, rb), :] = _rope(x, cos_ref[pl.ds(st, rb), :], sa_ref[pl.ds(st, rb), :],
                                             sb_ref[pl.ds(st, rb), :]).astype(bf16)
            v = z_ref[pl.ds(st, rb), 2 * w:3 * w]
            for j in range(HEADS):
                vm_ref[j, pl.ds(st, rb), :] = jnp.where(head == j, v, 0.0).astype(bf16)
        _rows(s, rb, prep)

    q0 = pl.multiple_of(qi * tq, tq)
    qr = _rope(z_ref[pl.ds(q0, tq), 0:w], cos_ref[pl.ds(q0, tq), :], sa_ref[pl.ds(q0, tq), :],
               sb_ref[pl.ds(q0, tq), :]) * (HEAD_W ** -0.5)
    qhead = _iota((tq, w), 1) // HEAD_W
    acc = jnp.zeros((tq, w), f32)
    for j in range(HEADS):
        qh = jnp.where(qhead == j, qr, 0.0).astype(bf16)
        sc = _mm_nt(qh, kr_ref[...]) + bias_ref[...]
        m = jnp.max(sc, axis=-1, keepdims=True)
        p = jnp.exp(sc - m)
        l = jnp.sum(p, axis=-1, keepdims=True)
        acc = acc + _mm(p.astype(bf16), vm_ref[j]) / l
    o_ref[...] = acc


def _dil(zc, cos, sa, sb, bias, nb, s, tq=256):
    w = BRANCH_W
    for _, dil in DIL_PATTERNS:
        assert dil & (dil - 1) == 0 and s % dil == 0
    return pl.pallas_call(
        functools.partial(_dil_kernel, s=s, tq=tq),
        grid=(nb, s // tq),
        in_specs=[pl.BlockSpec((None, s, ZC_W), lambda b, i: (b, 0, 0)),
                  _const_spec((s, w)), _const_spec((s, w)), _const_spec((s, w)),
                  pl.BlockSpec((tq, s), lambda b, i: (i, 0))],
        out_specs=pl.BlockSpec((None, tq, w), lambda b, i: (b, i, 0)),
        out_shape=jax.ShapeDtypeStruct((nb, s, w), f32),
        scratch_shapes=[pltpu.VMEM((s, w), bf16), pltpu.VMEM((HEADS, s, w), bf16)],
        compiler_params=_cparams("parallel", "arbitrary"),
        name="dilattn",
    )(zc.reshape(nb, s, ZC_W), cos, sa, sb, bias)


def _merge_kernel(h_ref, ya, yb, yc, yd, g_ref, wg_ref, wbr_ref, wo_ref, o_ref):
    x = h_ref[...]
    u = _rms(x, g_ref[...]).astype(bf16)
    merged = None
    for n, y in enumerate((ya, yb, yc, yd)):
        gate = jax.nn.sigmoid(_mm(u, wg_ref[n]))
        term = gate * _mm(y[...].astype(bf16), wbr_ref[n])
        merged = term if merged is None else merged + term
    o_ref[...] = x + _mm(merged.astype(bf16), wo_ref[...])


def _merge(h, ys, g, wg, wbr, wo, tm=512):
    t = h.shape[0]
    tok = lambda w: pl.BlockSpec((tm, w), lambda i: (i, 0))
    return pl.pallas_call(
        _merge_kernel,
        grid=(t // tm,),
        in_specs=[tok(D_MODEL)] + [tok(BRANCH_W)] * 4
        + [_const_spec((1, D_MODEL)), _const_spec((4, D_MODEL, D_MODEL)),
           _const_spec((4, BRANCH_W, D_MODEL)), _const_spec((D_MODEL, D_MODEL))],
        out_specs=tok(D_MODEL),
        out_shape=jax.ShapeDtypeStruct((t, D_MODEL), f32),
        compiler_params=_cparams("parallel"),
        name="merge",
    )(h, *ys, g, wg, wbr, wo)


def _stair_cells():
    return [(a, b) for a in range(PEER_TOPK) for b in range(PEER_TOPK) if (a + 1) * (b + 1) <= PEER_TOPK]


_CELLS = _stair_cells()
_NCELL = len(_CELLS)
_NCELL_PAD = -(-_NCELL // 8) * 8


def _extract_top(works, rowf, nrows):
    works = list(works)
    tops = [[] for _ in works]
    ranks = [jnp.full(w.shape, float(PEER_TOPK), f32) for w in works]
    for r in range(PEER_TOPK):
        mxs = [jnp.max(w, axis=0, keepdims=True) for w in works]
        cands = [jnp.where(w == mx, rowf, float(nrows)) for w, mx in zip(works, mxs)]
        hits = [cand == jnp.min(cand, axis=0, keepdims=True) for cand in cands]
        works = [jnp.where(hit, -jnp.inf, w) for hit, w in zip(hits, works)]
        ranks = [jnp.where(hit, float(r), rank) for hit, rank in zip(hits, ranks)]
        for t, mx in zip(tops, mxs):
            t.append(mx)
    return tops, ranks


def _distinct_maxima(xs):
    tops = [[] for _ in xs]
    mxs = [None] * len(xs)
    for _ in range(PEER_TOPK):
        curs = [x if mx is None else jnp.where(x < mx, x, -jnp.inf) for x, mx in zip(xs, mxs)]
        mxs = [jnp.max(cur, axis=0, keepdims=True) for cur in curs]
        for t, mx in zip(tops, mxs):
            t.append(mx)
    return tops


def _count(mask):
    return jnp.sum(jnp.where(mask, 1.0, 0.0), axis=0, keepdims=True)


def _route_kernel(h_ref, g_ref, wpq_ref, keys_ref, sa_ref, ut_ref, a_ref, cnt_ref, rk_ref, bn_ref,
                  q_ref, *m_refs, tm):
    uf = _rms(h_ref[...], g_ref[...])
    ut_ref[...] = uf.T.astype(bf16)
    q_ref[...] = _mm(uf.astype(bf16), wpq_ref[...]).astype(bf16)
    nk = PEER_NKEYS
    topk = float(PEER_TOPK)

    def scores(hd, toks):
        return [_mm_nt(keys_ref[hd * 2 + p], q_ref[toks, pl.ds(pl.multiple_of((hd * 2 + p) * nk, nk), nk)])
                for p in range(2)]

    def write_tables(hd, amat, cnt, rank2, bnorm):
        rows = pl.ds(pl.multiple_of(hd * nk, nk), nk)
        prow = pl.ds(pl.multiple_of(hd * (nk // 2), nk // 2), nk // 2)
        rkp = pltpu.bitcast(rank2.astype(bf16), jnp.uint32)
        bnp = pltpu.bitcast(bnorm.astype(bf16), jnp.uint32)
        for tb in range(tm // LANES):
            lanes = slice(tb * LANES, (tb + 1) * LANES)
            a_ref[tb, rows, :] = amat[:, lanes]
            cnt_ref[tb, rows, :] = cnt[:, lanes]
            rk_ref[tb, prow, :] = rkp[:, lanes]
            bn_ref[tb, prow, :] = bnp[:, lanes]

    def stair_sums(tops, m_ref):
        m_ref[...] = jnp.full((_NCELL_PAD, tm), -jnp.inf, f32)
        for ci, (a, b) in enumerate(_CELLS):
            m_ref[ci:ci + 1, :] = tops[0][a] + tops[1][b]
        return m_ref[...]

    def fast_tiles(hds):
        scs = [scores(hd, slice(0, tm)) for hd in hds]
        flat = _distinct_maxima([s for pair in scs for s in pair])
        tops = [flat[2 * i:2 * i + 2] for i in range(len(hds))]
        rank2s = [jnp.zeros((nk, tm), f32) for _ in hds]
        for r in range(PEER_TOPK):
            rank2s = [rank2 + jnp.where(sc[1] < top[1][r], 1.0, 0.0) for rank2, sc, top in zip(rank2s, scs, tops)]
        tied = jnp.zeros((1, tm), f32)
        for sc, top, rank2 in zip(scs, tops, rank2s):
            tied = jnp.where(_count(sc[0] >= top[0][-1]) == topk, tied, 1.0)
            tied = jnp.where(_count(rank2 < topk) == topk, tied, 1.0)
        cands = [stair_sums(top, m_ref) for top, m_ref in zip(tops, m_refs)]
        ctops = _distinct_maxima(cands)
        for hd, sc, top, rank2, cand, ctop in zip(hds, scs, tops, rank2s, cands, ctops):
            chosen = cand >= ctop[-1]
            tied = jnp.where(_count(chosen) == topk, tied, 1.0)
            zsum = jnp.sum(jnp.where(chosen, jnp.exp(cand - cand[0:1]), 0.0), axis=0, keepdims=True)
            rowcnt = _mm(sa_ref[...], jnp.where(chosen, 1.0, 0.0).astype(bf16))
            cnt = jnp.zeros((nk, tm), f32)
            for a in range(PEER_TOPK):
                cnt = jnp.where(sc[0] == top[0][a], rowcnt[a:a + 1], cnt)
            write_tables(hd, jnp.exp(sc[0] - top[0][0]), cnt, rank2, jnp.exp(sc[1] - top[1][0]) / zsum)
        return tied

    def exact_tile(hd, m_ref):
        rowf = _iota((nk, tm), 0).astype(f32)
        rowcf = _iota((_NCELL_PAD, tm), 0).astype(f32)
        scs = scores(hd, slice(0, tm))
        tops, ranks = _extract_top(scs, rowf, nk)
        cand = stair_sums(tops, m_ref)
        _, (crank,) = _extract_top([cand], rowcf, _NCELL_PAD)
        chosen = crank < topk
        zsum = jnp.sum(jnp.where(chosen, jnp.exp(cand - cand[0:1]), 0.0), axis=0, keepdims=True)
        rowcnt = _mm(sa_ref[...], jnp.where(chosen, 1.0, 0.0).astype(bf16))
        cnt = jnp.zeros((nk, tm), f32)
        for a in range(PEER_TOPK):
            cnt = jnp.where(ranks[0] == float(a), rowcnt[a:a + 1], cnt)
        write_tables(hd, jnp.exp(scs[0] - tops[0][0]), cnt, ranks[1], jnp.exp(scs[1] - tops[1][0]) / zsum)

    def per_group(hg, carry):
        hds = [hg * ROUTE_HEADS_TOGETHER + j for j in range(ROUTE_HEADS_TOGETHER)]
        tied = fast_tiles(hds)

        @pl.when(jnp.max(tied) > 0.0)
        def _():
            for hd, m_ref in zip(hds, m_refs):
                exact_tile(hd, m_ref)
        return carry

    lax.fori_loop(0, PEER_HEADS // ROUTE_HEADS_TOGETHER, per_group, 0)


def _route(h, g, wpq, keys, sa, tm=256):
    t = h.shape[0]
    nh, nk = PEER_HEADS, PEER_NKEYS
    tab = pl.BlockSpec((tm // LANES, nh * nk, LANES), lambda i: (i, 0, 0))
    ptab = pl.BlockSpec((tm // LANES, nh * nk // 2, LANES), lambda i: (i, 0, 0))
    return pl.pallas_call(
        functools.partial(_route_kernel, tm=tm),
        grid=(t // tm,),
        in_specs=[pl.BlockSpec((tm, D_MODEL), lambda i: (i, 0)), _const_spec((1, D_MODEL)),
                  _const_spec((D_MODEL, 2 * nh * nk)), _const_spec((2 * nh, nk, nk)),
                  _const_spec((PEER_TOPK, _NCELL_PAD))],
        out_specs=[pl.BlockSpec((D_MODEL, tm), lambda i: (0, i)), tab, tab, ptab, ptab],
        out_shape=[jax.ShapeDtypeStruct((D_MODEL, t), bf16)] + [jax.ShapeDtypeStruct((t // LANES, nh * nk, LANES), f32)] * 2
        + [jax.ShapeDtypeStruct((t // LANES, nh * nk // 2, LANES), jnp.uint32)] * 2,
        scratch_shapes=[pltpu.VMEM((tm, 2 * nh * nk), bf16)]
        + [pltpu.VMEM((_NCELL_PAD, tm), f32)] * ROUTE_HEADS_TOGETHER,
        compiler_params=_cparams("parallel"),
        name="route",
    )(h, g, wpq, keys, sa)


def _gelu_bf16(x):
    return (0.5 * x * (1.0 + lax.erf(x * math.sqrt(0.5)))).astype(bf16)


def _peer_gate_block(act_ref, p_ref, tabs, i0, ii, tb):
    a_ref, cnt_ref, rk_ref, bn_ref = tabs
    nk = PEER_NKEYS
    lanes = slice(tb * LANES, (tb + 1) * LANES)
    gate = None
    for hd in range(PEER_HEADS):
        row = pl.ds(hd * nk + i0 + ii, 1)
        a = jnp.broadcast_to(a_ref[tb, row, :], (nk, LANES)).astype(bf16)
        c = jnp.broadcast_to(cnt_ref[tb, row, :], (nk, LANES)).astype(bf16)
        prow = slice(hd * (nk // 2), (hd + 1) * (nk // 2))
        rk = pltpu.bitcast(rk_ref[tb, prow, :], bf16)
        bn = pltpu.bitcast(bn_ref[tb, prow, :], bf16)
        term = a * jnp.where(rk < c, bn, jnp.zeros((), bf16))
        gate = term if gate is None else gate + term
    p_ref[ii * nk:(ii + 1) * nk, lanes] = gate * act_ref[ii * nk:(ii + 1) * nk, lanes]


def _mxu_product(rhs_tile, lhs_tile, between):
    nt = PEER_SUB // MXU_TILE
    for q in range(2):
        pltpu.matmul_push_rhs(rhs_tile(0, q), staging_register=0, mxu_index=q)
    for kt in range(nt):
        reg = kt % 2
        for mc in range(nt):
            lhs = lhs_tile(mc, kt)
            for q in range(2):
                pltpu.matmul_acc_lhs(mc * (MXU_TILE // 4), lhs, mxu_index=q,
                                     load_staged_rhs=(reg if mc == 0 else None))
            if mc == 0 and kt + 1 < nt:
                for q in range(2):
                    pltpu.matmul_push_rhs(rhs_tile(kt + 1, q), staging_register=1 - reg, mxu_index=q)
            between()


def _peer_kernel(ut_ref, a_ref, cnt_ref, rk_ref, bn_ref, pu_ref, pvt_ref, o_ref, acc_ref, act0, act1, p0, p1,
                 *, ne, tp):
    g = pl.program_id(0)
    nblocks = pl.num_programs(0) - 2
    b1 = jnp.clip(g - 1, 0, nblocks - 1)
    b2 = jnp.clip(g - 2, 0, nblocks - 1)
    i1 = (b1 % ne) * (PEER_SUB // PEER_NKEYS)
    tabs = (a_ref, cnt_ref, rk_ref, bn_ref)

    @pl.when(g == 0)
    def _():
        for ref in (act0, act1, p0, p1):
            ref[...] = jnp.zeros(ref.shape, bf16)

    @pl.when(b2 % ne == 0)
    def _():
        acc_ref[...] = jnp.zeros(acc_ref.shape, f32)

    t256 = lambda i: slice(i * MXU_TILE, (i + 1) * MXU_TILE)

    def stage(act_w, act_r, p_w, p_r):
        blocks = iter([(ii, tb) for ii in range(PEER_SUB // PEER_NKEYS) for tb in range(tp // LANES)])

        def one_gate_block():
            ii, tb = next(blocks)
            _peer_gate_block(act_r, p_w, tabs, i1, ii, tb)

        nt = PEER_SUB // MXU_TILE
        _mxu_product(lambda kt, q: ut_ref[t256(kt), t256(q)], lambda mc, kt: pu_ref[t256(mc), t256(kt)],
                     one_gate_block)
        for mc in range(nt):
            for q in range(2):
                x = pltpu.matmul_pop(mc * (MXU_TILE // 4), (MXU_TILE, MXU_TILE), f32, q)
                act_w[t256(mc), t256(q)] = _gelu_bf16(x)
        _mxu_product(lambda kt, q: p_r[t256(kt), t256(q)], lambda mc, kt: pvt_ref[t256(mc), t256(kt)],
                     one_gate_block)
        for mc in range(nt):
            for q in range(2):
                acc_ref[t256(mc), t256(q)] += pltpu.matmul_pop(mc * (MXU_TILE // 4), (MXU_TILE, MXU_TILE), f32, q)

    @pl.when(g % 2 == 0)
    def _():
        stage(act0, act1, p1, p0)

    @pl.when(g % 2 == 1)
    def _():
        stage(act1, act0, p0, p1)

    @pl.when(jnp.logical_and(g >= 2, b2 % ne == ne - 1))
    def _():
        o_ref[...] = acc_ref[...].T


def _peer(ut, tabs, pu, pvt):
    t = ut.shape[1]
    tp = 2 * MXU_TILE
    nh, nk = PEER_HEADS, PEER_NKEYS
    assert D_MODEL == PEER_SUB and (PEER_SUB // nk) * (tp // LANES) == 2 * (PEER_SUB // MXU_TILE) ** 2
    ne = pu.shape[0] // PEER_SUB
    nblocks = (t // tp) * ne
    blk = lambda lag: (lambda g: jnp.clip(g - lag, 0, nblocks - 1))
    b0, b1, b2 = blk(0), blk(1), blk(2)
    tab = pl.BlockSpec((tp // LANES, nh * nk, LANES), lambda g: (b1(g) // ne, 0, 0))
    ptab = pl.BlockSpec((tp // LANES, nh * nk // 2, LANES), lambda g: (b1(g) // ne, 0, 0))
    return pl.pallas_call(
        functools.partial(_peer_kernel, ne=ne, tp=tp),
        grid=(nblocks + 2,),
        in_specs=[pl.BlockSpec((D_MODEL, tp), lambda g: (0, b0(g) // ne)), tab, tab, ptab, ptab,
                  pl.BlockSpec((PEER_SUB, D_MODEL), lambda g: (b0(g) % ne, 0)),
                  pl.BlockSpec((D_MODEL, PEER_SUB), lambda g: (0, b2(g) % ne))],
        out_specs=pl.BlockSpec((tp, D_MODEL), lambda g: (b2(g) // ne, 0)),
        out_shape=jax.ShapeDtypeStruct((t, D_MODEL), f32),
        scratch_shapes=[pltpu.VMEM((D_MODEL, tp), f32)] + [pltpu.VMEM((PEER_SUB, tp), bf16)] * 4,
        compiler_params=_cparams("arbitrary"),
        name="peer",
    )(ut, *tabs, pu, pvt)


def _pe_kernel(h_ref, po_ref, p_ref, g_ref, wpg_ref, wpe_ref, gf_ref, o_ref, *, final):
    h = h_ref[...] + po_ref[...]
    gate = jax.nn.sigmoid(_mm(_rms(h, g_ref[...]).astype(bf16), wpg_ref[...]))
    h = h + gate * _mm(p_ref[...].astype(bf16), wpe_ref[...])
    if final:
        h = _rms(h, gf_ref[...])
    o_ref[...] = h


def _pe(h, po, p, g, wpg, wpe, gf, final, tm=512):
    t = h.shape[0]
    tok = lambda w: pl.BlockSpec((tm, w), lambda i: (i, 0))
    return pl.pallas_call(
        functools.partial(_pe_kernel, final=final),
        grid=(t // tm,),
        in_specs=[tok(D_MODEL), tok(D_MODEL), tok(PE_DIM), _const_spec((1, D_MODEL)),
                  _const_spec((D_MODEL, D_MODEL)), _const_spec((PE_DIM, D_MODEL)), _const_spec((1, D_MODEL))],
        out_specs=tok(D_MODEL),
        out_shape=jax.ShapeDtypeStruct((t, D_MODEL), f32),
        compiler_params=_cparams("parallel"),
        name="pe_embed",
    )(h, po, p, g, wpg, wpe, gf)


def _cols(w, name):
    a, b = _PROJ_OFF[name]
    return w[..., a:b]


def _proj_groups(w):
    pad = jnp.zeros(w.shape[:-1] + (ZA_W - 800,), w.dtype)
    ga = jnp.concatenate([_cols(w, n) for n in ('a_q', 'a_k', 'a_v', 'a_g', 'a_lr')] + [pad], axis=-1)
    gates = [jnp.repeat(_cols(w, n), HEAD_W, axis=-1) for n in ('b_i', 'b_f')]
    gb = jnp.concatenate([_cols(w, n) for n in ('b_q', 'b_k', 'b_v', 'b_o')] + gates, axis=-1)
    gc = jnp.concatenate([_cols(w, n) for n in ('c_q', 'c_k', 'c_v')], axis=-1)
    gd = jnp.concatenate([_cols(w, n) for n in ('d_q', 'd_f', 'd_i', 'd_g')], axis=-1)
    return ga, gb, gc, gd


def _rope_tables(s):
    half = HEAD_W // 2
    inv = ROPE_THETA ** (-jnp.arange(half, dtype=f32) / half)
    ang = jnp.arange(s, dtype=f32)[:, None] * inv[None, :]
    cos, sin = jnp.cos(ang), jnp.sin(ang)
    zero = jnp.zeros_like(sin)
    tile = lambda a, b: jnp.tile(jnp.concatenate([a, b], axis=-1), (1, HEADS))
    return tile(cos, cos), tile(-sin, zero), tile(zero, sin)


def _row_count_matrix():
    m = np.zeros((PEER_TOPK, _NCELL_PAD), np.float32)
    for ci, (a, _) in enumerate(_CELLS):
        m[a, ci] = 1.0
    return jnp.asarray(m, bf16)


def _prepare(g_mix, w_in, b_in, w_lr2, b_lr2, conv_w, conv_b, lb_param, g_a, g_b, g_d, w_gate, w_br, w_out,
             g_ffn, w_pq, sub_keys, peer_u, peer_v, g_pe, w_pe, w_pg, g_final):
    layers = []
    row = lambda a: a.reshape(1, -1)
    for l in range(DEPTH):
        wlr = jnp.zeros((LANES, 256), f32)
        for n in range(2):
            wlr = wlr.at[n * GLA_RANK:(n + 1) * GLA_RANK, n * 128:(n + 1) * 128].set(w_lr2[l, n])
        layers.append(dict(
            g_mix=row(g_mix[l]),
            w_proj=[w.astype(bf16) for w in _proj_groups(w_in[l])],
            b_proj=list(_proj_groups(row(b_in[l]))),
            wlr=wlr.astype(bf16), blr=row(b_lr2[l]),
            conv_w=conv_w[l], conv_b=row(conv_b[l]),
            g_a=row(g_a[l]), g_b=row(g_b[l]), g_d=row(g_d[l]),
            w_gate=w_gate[l].astype(bf16), w_br=w_br[l].astype(bf16), w_out=w_out[l].astype(bf16),
            g_ffn=row(g_ffn[l]), w_pq=w_pq[l].astype(bf16),
            keys=sub_keys[l].reshape(2 * PEER_HEADS, PEER_NKEYS, -1).astype(bf16),
            peer_u=peer_u[l].astype(bf16), peer_vt=peer_v[l].astype(bf16).T,
            g_pe=row(g_pe[l]), w_pe=w_pe[l].astype(bf16), w_pg=w_pg[l].astype(bf16),
        ))
    shared = dict(
        lbp=jnp.transpose(lb_param, (1, 0, 2)).reshape(DEPTH, -1),
        g_final=row(g_final), sa=_row_count_matrix(),
    )
    return layers, shared


def _trunk(x, p, layers, shared):
    nb, s, d = x.shape
    t = nb * s
    h = x.reshape(t, d)
    cos, sa, sb = _rope_tables(s)
    dil_bias = _dil_bias(s)
    for l, w in enumerate(layers):
        za, zb, zc, zd = _proj(h, w['g_mix'], w['w_proj'], w['b_proj'])
        ya = _gla(za, w['wlr'], w['blr'], w['g_a'], nb, s)
        yb = _mlstm(zb, w['conv_w'], w['conv_b'], w['g_b'], nb, s)
        yc = _dil(zc, cos, sa, sb, dil_bias, nb, s)
        yd = _hgrn(zd, shared['lbp'], w['g_d'], nb, s, l)
        ys = [y.reshape(t, BRANCH_W) for y in (ya, yb, yc, yd)]
        h1 = _merge(h, ys, w['g_mix'], w['w_gate'], w['w_br'], w['w_out'])
        ut, *tabs = _route(h1, w['g_ffn'], w['w_pq'], w['keys'], shared['sa'])
        po = _peer(ut, tabs, w['peer_u'], w['peer_vt'])
        h = _pe(h1, po, p[l].reshape(t, PE_DIM), w['g_pe'], w['w_pg'], w['w_pe'], shared['g_final'],
                final=(l == DEPTH - 1))
    return h.reshape(nb, s, d)


def kernel(x_prompt, x_sample, p_prompt, p_sample, g_mix, w_in, b_in, w_lr2, b_lr2, conv_w, conv_b, lb_param, g_a, g_b, g_d, w_gate, w_br, w_out, g_ffn, w_pq, sub_keys, peer_u, peer_v, g_pe, w_pe, w_pg, g_final):
    layers, shared = _prepare(g_mix, w_in, b_in, w_lr2, b_lr2, conv_w, conv_b, lb_param, g_a, g_b, g_d, w_gate,
                              w_br, w_out, g_ffn, w_pq, sub_keys, peer_u, peer_v, g_pe, w_pe, w_pg, g_final)
    return (_trunk(x_prompt, p_prompt, layers, shared), _trunk(x_sample, p_sample, layers, shared))
```

```python
import functools
import math

import numpy as np
import jax
import jax.numpy as jnp
from jax import lax
from jax.experimental import pallas as pl
from jax.experimental.pallas import tpu as pltpu

f32 = jnp.float32
bf16 = jnp.bfloat16
i32 = jnp.int32

D_MODEL = 1024
DEPTH = 2
PE_DIM = 256
EPS = 1e-6
NEG_BIG = -1e30
HEADS = 4
HEAD_W = 64
BRANCH_W = HEADS * HEAD_W
GLA_DK = 32
GLA_RANK = 16
GLA_NORMALIZER = 16.0
GLA_CHUNK = 32
MLSTM_CHUNK = 64
HGRN_DF = 64
HGRN_CHUNK = 32
DIL_PATTERNS = ((128, 1), (512, 4), (2048, 16))
ROPE_THETA = 10000.0
PEER_HEADS = 8
PEER_NKEYS = 128
PEER_TOPK = 16
LANES = 128
PEER_SUB = 1024
MXU_TILE = 256
ROUTE_HEADS_TOGETHER = 2
EXP_CLAMP = 80.0
GL_UNROLL = 8
MLSTM_UNROLL = 2

_PROJ_LAYOUT = (
    ('a_q', 128), ('a_k', 128), ('a_v', 256), ('a_g', 256), ('a_lr', 32),
    ('b_q', 256), ('b_k', 256), ('b_v', 256), ('b_i', 8), ('b_f', 8), ('b_o', 256),
    ('c_q', 256), ('c_k', 256), ('c_v', 256),
    ('d_q', 256), ('d_f', 512), ('d_i', 256), ('d_g', 256),
)
_PROJ_OFF = {}
_o = 0
for _n, _s in _PROJ_LAYOUT:
    _PROJ_OFF[_n] = (_o, _o + _s)
    _o += _s

ZA_W = 896
ZB_W = 2048
ZC_W = 768
ZD_W = 1280

VMEM_LIMIT = 56 * 1024 * 1024


def _cparams(*sem):
    return pltpu.CompilerParams(dimension_semantics=sem, vmem_limit_bytes=VMEM_LIMIT)


def _const_spec(shape):
    nd = len(shape)
    return pl.BlockSpec(shape, lambda *_: (0,) * nd, pipeline_mode=pl.Buffered(1))


def _mm(a, b):
    return jnp.dot(a, b, preferred_element_type=f32)


def _mm_nt(a, b):
    return lax.dot_general(a, b, (((1,), (1,)), ((), ())), preferred_element_type=f32)


def _mm_tn(a, b):
    return lax.dot_general(a, b, (((0,), (0,)), ((), ())), preferred_element_type=f32)


def _split3(x):
    hi = x.astype(bf16)
    r1 = x - hi.astype(f32)
    mid = r1.astype(bf16)
    lo = (r1 - mid.astype(f32)).astype(bf16)
    return hi, mid, lo


def _lmul3(m, x):
    w = x.shape[1]
    y = _mm(m, jnp.concatenate(_split3(x), axis=1))
    return y[:, 0:w] + y[:, w:2 * w] + y[:, 2 * w:3 * w]


def _rmul3(x, m):
    r = x.shape[0]
    y = _mm(jnp.concatenate(_split3(x), axis=0), m)
    return y[0:r] + y[r:2 * r] + y[2 * r:3 * r]


def _rms(x, g):
    return x * lax.rsqrt(jnp.mean(x * x, axis=-1, keepdims=True) + EPS) * g


def _iota(shape, dim):
    return lax.broadcasted_iota(i32, shape, dim)


def _tri(c, reverse):
    r, k = _iota((c, c), 0), _iota((c, c), 1)
    return jnp.where((k >= r) if reverse else (k <= r), 1.0, 0.0).astype(bf16)


def _block_avg(w, blk):
    same = (_iota((w, w), 0) // blk) == (_iota((w, w), 1) // blk)
    return jnp.where(same, 1.0 / blk, 0.0).astype(bf16)


def _rows(n, blk, fn):
    def body(i, c):
        fn(pl.multiple_of(i * blk, blk), i)
        return c
    lax.fori_loop(0, n // blk, body, 0)


def _head_rmsnorm(o, g, mavg):
    ms = _rmul3(o * o, mavg)
    return o * lax.rsqrt(ms + EPS) * g


def _proj_kernel(h_ref, g_ref, wa, wb, wc, wd, ba, bb, bc, bd, za, zb, zc, zd):
    u = _rms(h_ref[...], g_ref[...]).astype(bf16)
    for w, b, z in ((wa, ba, za), (wb, bb, zb), (wc, bc, zc), (wd, bd, zd)):
        z[...] = _mm(u, w[...]) + b[...]


def _proj(h, g, ws, bs, tm=512):
    t = h.shape[0]
    widths = (ZA_W, ZB_W, ZC_W, ZD_W)
    tok = lambda w: pl.BlockSpec((tm, w), lambda i: (i, 0))
    return pl.pallas_call(
        _proj_kernel,
        grid=(t // tm,),
        in_specs=[tok(D_MODEL), _const_spec((1, D_MODEL))]
        + [_const_spec((D_MODEL, w)) for w in widths] + [_const_spec((1, w)) for w in widths],
        out_specs=[tok(w) for w in widths],
        out_shape=[jax.ShapeDtypeStruct((t, w), f32) for w in widths],
        compiler_params=_cparams("parallel"),
        name="proj",
    )(h, g, *ws, *bs)


def _gl_bidir(loads, acc_refs, st_refs, *, s, c, dk):
    h, dv = HEADS, HEAD_W
    kw, vw, hc = h * dk, h * dv, h * c
    n, un = s // c, GL_UNROLL
    khead = _iota((c, kw), 1) // dk
    vhead = _iota((c, vw), 1) // dv
    sidx, trow = _iota((c, hc), 1) % c, _iota((c, hc), 0)
    bd = (_iota((vw, kw), 0) // dv) == (_iota((vw, kw), 1) // dk)
    tris = (_tri(c, False), _tri(c, True))
    causals = (sidx <= trow, sidx >= trow)
    for st_ref in st_refs:
        st_ref[...] = jnp.zeros((vw, kw), f32)

    streams = [(d, u) for u in range(un) for d in range(2)]

    def body(ci, carry):
        idx = [(ci * un + u) if d == 0 else (n - 1 - ci * un - u) for d, u in streams]
        sts = [pl.multiple_of(i * c, c) for i in idx]
        ins = [loads[d](st) for (d, _), st in zip(streams, sts)]
        qs, ks, vs = [x[0] for x in ins], [x[1] for x in ins], [x[2] for x in ins]
        bs = [_lmul3(tris[d], x[3]) for (d, _), x in zip(streams, ins)]
        lasts = [b[0:1] if d == 1 else b[c - 1:c] for (d, _), b in zip(streams, bs)]
        mids = [b[c // 2:c // 2 + 1] for b in bs]
        qts = [(q * jnp.exp(jnp.minimum(b - m, EXP_CLAMP))).astype(bf16) for q, b, m in zip(qs, bs, mids)]
        kts = [k * jnp.exp(jnp.minimum(m - b, EXP_CLAMP)) for k, b, m in zip(ks, bs, mids)]
        kexps = [jnp.concatenate([jnp.where(khead == j, kt, 0.0).astype(bf16) for j in range(h)], axis=0)
                 for kt in kts]
        vexps = [jnp.concatenate([jnp.where(vhead == j, v, 0.0).astype(bf16) for j in range(h)], axis=0)
                 for v in vs]
        atts = [jnp.where(causals[d], _mm_nt(qt, kexp), 0.0).astype(bf16)
                for (d, _), qt, kexp in zip(streams, qts, kexps)]
        os_ = [_mm(att, vexp) for att, vexp in zip(atts, vexps)]
        qis = [(q * jnp.exp(b)).astype(bf16) for q, b in zip(qs, bs)]
        kds = [(k * jnp.exp(l - b)).astype(bf16) for k, l, b in zip(ks, lasts, bs)]
        upds = [jnp.where(bd, _mm_tn(v.astype(bf16), kd), 0.0) for v, kd in zip(vs, kds)]
        decs = [jnp.exp(l) for l in lasts]
        states = [st_ref[...] for st_ref in st_refs]
        for i, (d, _) in enumerate(streams):
            acc_refs[d][pl.ds(sts[i], c), :] = os_[i] + _mm_nt(qis[i], states[d].astype(bf16))
            states[d] = states[d] * decs[i] + upds[i]
        for d in range(2):
            st_refs[d][...] = states[d]
        return carry

    lax.fori_loop(0, n // un, body, 0)


def _gla_kernel(z_ref, wlr_ref, blr_ref, g_ref, o_ref, gk_ref, accf_ref, accb_ref, stf_ref, stb_ref, *, s):
    rb = 256

    def gates(st, i):
        lr = z_ref[pl.ds(st, rb), 768:896].astype(bf16)
        x = _mm(lr, wlr_ref[...]) + blr_ref[...]
        gk_ref[pl.ds(st, rb), :] = jax.nn.log_sigmoid(x) * (1.0 / GLA_NORMALIZER)
    _rows(s, rb, gates)

    scale = GLA_DK ** -0.5

    def load(st, d):
        q = z_ref[pl.ds(st, GLA_CHUNK), 0:128] * scale
        k = z_ref[pl.ds(st, GLA_CHUNK), 128:256]
        v = z_ref[pl.ds(st, GLA_CHUNK), 256:512]
        g = gk_ref[pl.ds(st, GLA_CHUNK), d * 128:(d + 1) * 128]
        return q, k, v, g
    _gl_bidir([functools.partial(load, d=d) for d in range(2)], (accf_ref, accb_ref), (stf_ref, stb_ref),
              s=s, c=GLA_CHUNK, dk=GLA_DK)

    mavg = _block_avg(BRANCH_W, HEAD_W)

    def out(st, i):
        y = _head_rmsnorm(accf_ref[pl.ds(st, rb), :] + accb_ref[pl.ds(st, rb), :], g_ref[...], mavg)
        gate = z_ref[pl.ds(st, rb), 512:768]
        o_ref[pl.ds(st, rb), :] = y * (gate * jax.nn.sigmoid(gate))
    _rows(s, rb, out)


def _gla(za, wlr, blr, g, nb, s):
    return pl.pallas_call(
        functools.partial(_gla_kernel, s=s),
        grid=(nb,),
        in_specs=[pl.BlockSpec((None, s, ZA_W), lambda b: (b, 0, 0)),
                  _const_spec((LANES, 256)), _const_spec((1, 256)), _const_spec((1, BRANCH_W))],
        out_specs=pl.BlockSpec((None, s, BRANCH_W), lambda b: (b, 0, 0)),
        out_shape=jax.ShapeDtypeStruct((nb, s, BRANCH_W), f32),
        scratch_shapes=[pltpu.VMEM((s, 256), f32)] + [pltpu.VMEM((s, BRANCH_W), f32)] * 2
        + [pltpu.VMEM((BRANCH_W, HEADS * GLA_DK), f32)] * 2,
        compiler_params=_cparams("parallel"),
        name="gla",
    )(za.reshape(nb, s, ZA_W), wlr, blr, g)


def _hgrn_kernel(z_ref, lbp_ref, g_ref, o_ref, q_ref, lf_ref, kd_ref, accf_ref, accb_ref, stf_ref, stb_ref,
                 *, s, layer):
    rb = 256
    x = lbp_ref[...]
    e = jnp.exp(x - jnp.max(x, axis=0, keepdims=True))
    p = e / jnp.sum(e, axis=0, keepdims=True)
    lb = p[0:1]
    for j in range(1, layer + 1):
        lb = lb + p[j:j + 1]
    lb = lb - p[0:1]

    def prep(st, i):
        qz = z_ref[pl.ds(st, rb), 0:256]
        q_ref[pl.ds(st, rb), :] = qz * jax.nn.sigmoid(qz)
        fz = z_ref[pl.ds(st, rb), 256:768]
        lf_ref[pl.ds(st, rb), :] = jnp.log(lb + (1.0 - lb) * jax.nn.sigmoid(fz))
        kd_ref[pl.ds(st, rb), :] = (1.0 - lb) * jax.nn.sigmoid(-fz)
    _rows(s, rb, prep)

    kw = HEADS * HGRN_DF

    def load(st, d):
        q = q_ref[pl.ds(st, HGRN_CHUNK), :]
        k = kd_ref[pl.ds(st, HGRN_CHUNK), d * kw:(d + 1) * kw]
        v = z_ref[pl.ds(st, HGRN_CHUNK), 768:1024]
        g = lf_ref[pl.ds(st, HGRN_CHUNK), d * kw:(d + 1) * kw]
        return q, k, v, g
    _gl_bidir([functools.partial(load, d=d) for d in range(2)], (accf_ref, accb_ref), (stf_ref, stb_ref),
              s=s, c=HGRN_CHUNK, dk=HGRN_DF)

    mavg = _block_avg(BRANCH_W, HEAD_W)

    def out(st, i):
        y = _head_rmsnorm(accf_ref[pl.ds(st, rb), :] + accb_ref[pl.ds(st, rb), :], g_ref[...], mavg)
        gate = z_ref[pl.ds(st, rb), 1024:1280]
        o_ref[pl.ds(st, rb), :] = y * (gate * jax.nn.sigmoid(gate))
    _rows(s, rb, out)


def _hgrn(zd, lbp, g, nb, s, layer):
    kw = HEADS * HGRN_DF
    return pl.pallas_call(
        functools.partial(_hgrn_kernel, s=s, layer=layer),
        grid=(nb,),
        in_specs=[pl.BlockSpec((None, s, ZD_W), lambda b: (b, 0, 0)),
                  _const_spec((DEPTH, 2 * kw)), _const_spec((1, BRANCH_W))],
        out_specs=pl.BlockSpec((None, s, BRANCH_W), lambda b: (b, 0, 0)),
        out_shape=jax.ShapeDtypeStruct((nb, s, BRANCH_W), f32),
        scratch_shapes=[pltpu.VMEM((s, kw), f32), pltpu.VMEM((s, 2 * kw), f32), pltpu.VMEM((s, 2 * kw), f32)]
        + [pltpu.VMEM((s, BRANCH_W), f32)] * 2 + [pltpu.VMEM((BRANCH_W, kw), f32)] * 2,
        compiler_params=_cparams("parallel"),
        name="hgrn",
    )(zd.reshape(nb, s, ZD_W), lbp, g)


def _mlstm_bidir(z_ref, q_ref, k_ref, acc_refs, st_refs, nv_refs, m_refs, *, s):
    c, w = MLSTM_CHUNK, BRANCH_W
    n, un = s // c, MLSTM_UNROLL
    ones = jnp.ones((c, c), bf16)
    lane, row = _iota((c, w), 1), _iota((c, w), 0)
    head, sidx = lane // HEAD_W, lane % HEAD_W
    diag = sidx == row
    same = (_iota((w, w), 0) // HEAD_W) == (_iota((w, w), 1) // HEAD_W)
    bones = jnp.where(same, 1.0, 0.0).astype(bf16)
    tris = (_tri(c, False), _tri(c, True))
    causals = (sidx <= row, sidx >= row)
    for d in range(2):
        st_refs[d][...] = jnp.zeros((w, w), f32)
        nv_refs[d][...] = jnp.zeros((1, w), f32)
        m_refs[d][...] = jnp.zeros((1, w), f32)

    streams = [(d, u) for u in range(un) for d in range(2)]

    def seg_max(x):
        out = jnp.full((c, w), NEG_BIG, f32)
        for j in range(HEADS):
            mj = jnp.max(jnp.where(head == j, x, NEG_BIG), axis=-1, keepdims=True)
            out = jnp.where(head == j, mj, out)
        return out

    def body(ci, carry):
        idx = [(ci * un + u) if d == 0 else (n - 1 - ci * un - u) for d, u in streams]
        sts = [pl.multiple_of(i * c, c) for i in idx]
        qs = [q_ref[pl.ds(st, c), :] for st in sts]
        ks = [k_ref[pl.ds(st, c), :] for st in sts]
        vs = [z_ref[pl.ds(st, c), 512:768] for st in sts]
        igs = [z_ref[pl.ds(st, c), 1024 + d * w:1024 + (d + 1) * w] for (d, _), st in zip(streams, sts)]
        lfs = [jax.nn.log_sigmoid(z_ref[pl.ds(st, c), 1536 + d * w:1536 + (d + 1) * w])
               for (d, _), st in zip(streams, sts)]
        bs = [_lmul3(tris[d], lf) for (d, _), lf in zip(streams, lfs)]
        rs = [_lmul3(ones, jnp.where(diag, ig - b, 0.0)) for ig, b in zip(igs, bs)]
        log_ds = [jnp.where(causals[d], b + r, NEG_BIG) for (d, _), b, r in zip(streams, bs, rs)]
        mrows = [seg_max(x) for x in log_ds]
        lasts = [b[0:1] if d == 1 else b[c - 1:c] for (d, _), b in zip(streams, bs)]
        log_ws = [l - b + ig for l, b, ig in zip(lasts, bs, igs)]
        mws = [jnp.max(x, axis=0, keepdims=True) for x in log_ws]
        qbs = [q.astype(bf16) for q in qs]
        kexps = [jnp.concatenate([jnp.where(head == j, k, 0.0).astype(bf16) for j in range(HEADS)], axis=0)
                 for k in ks]
        vexps = [jnp.concatenate([jnp.where(head == j, v, 0.0).astype(bf16) for j in range(HEADS)], axis=0)
                 for v in vs]
        qks = [_mm_nt(qb, kexp) for qb, kexp in zip(qbs, kexps)]
        ms = [m_ref[...] for m_ref in m_refs]
        m_ts, w_prevs, w_ss, decs = [], [], [], []
        for i, (d, _) in enumerate(streams):
            log_prev = bs[i] + ms[d]
            m_t = jnp.maximum(mrows[i], log_prev)
            m_new = jnp.maximum(lasts[i] + ms[d], mws[i])
            m_ts.append(m_t)
            w_prevs.append(jnp.exp(log_prev - m_t))
            w_ss.append(jnp.exp(log_ws[i] - m_new))
            decs.append(jnp.exp(lasts[i] + ms[d] - m_new))
            ms[d] = m_new
        scs = [qk * jnp.exp(log_d - m_t) for qk, log_d, m_t in zip(qks, log_ds, m_ts)]
        nums = [_mm(sc.astype(bf16), vexp) for sc, vexp in zip(scs, vexps)]
        dens = [_rmul3(sc, bones) for sc in scs]
        upds = [jnp.where(same, _mm_tn((v * w_s).astype(bf16), k.astype(bf16)), 0.0)
                for v, w_s, k in zip(vs, w_ss, ks)]
        nvus = [jnp.sum(w_s * k, axis=0, keepdims=True) for w_s, k in zip(w_ss, ks)]
        states = [st_ref[...] for st_ref in st_refs]
        nvs = [nv_ref[...] for nv_ref in nv_refs]
        for i, (d, _) in enumerate(streams):
            num = w_prevs[i] * _mm_nt(qbs[i], states[d].astype(bf16)) + nums[i]
            den = w_prevs[i] * _rmul3(qs[i] * nvs[d], bones) + dens[i]
            acc_refs[d][pl.ds(sts[i], c), :] = num / jnp.maximum(jnp.abs(den), jnp.exp(-m_ts[i]))
            states[d] = decs[i] * states[d] + upds[i]
            nvs[d] = decs[i] * nvs[d] + nvus[i]
        for d in range(2):
            st_refs[d][...] = states[d]
            nv_refs[d][...] = nvs[d]
            m_refs[d][...] = ms[d]
        return carry

    lax.fori_loop(0, n // un, body, 0)


def _mlstm_kernel(z_ref, cw_ref, cb_ref, g_ref, o_ref, q_ref, k_ref, accf_ref, accb_ref, stf_ref, stb_ref,
                  nvf_ref, nvb_ref, mf_ref, mb_ref, *, s):
    rb = 256
    nblk = s // rb
    w = BRANCH_W

    def conv(st, i):
        x = z_ref[pl.ds(st, rb), 0:2 * w]
        pst = pl.multiple_of(jnp.maximum(st - 8, 0), 8)
        nst = pl.multiple_of(jnp.minimum(st + rb, s - 8), 8)
        prev = z_ref[pl.ds(pst, 8), 0:2 * w][7:8]
        nxt = z_ref[pl.ds(nst, 8), 0:2 * w][0:1]
        prev = jnp.where(i == 0, 0.0, prev)
        nxt = jnp.where(i == nblk - 1, 0.0, nxt)
        row = _iota((rb, 2 * w), 0)
        xm = jnp.where(row == 0, prev, pltpu.roll(x, 1, 0))
        xp = jnp.where(row == rb - 1, nxt, pltpu.roll(x, rb - 1, 0))
        y = cb_ref[...] + xm * cw_ref[0:1, :]
        y = y + x * cw_ref[1:2, :]
        y = y + xp * cw_ref[2:3, :]
        y = y * jax.nn.sigmoid(y)
        q_ref[pl.ds(st, rb), :] = y[:, 0:w]
        k_ref[pl.ds(st, rb), :] = y[:, w:2 * w] * (HEAD_W ** -0.5)
    _rows(s, rb, conv)

    _mlstm_bidir(z_ref, q_ref, k_ref, (accf_ref, accb_ref), (stf_ref, stb_ref), (nvf_ref, nvb_ref),
                 (mf_ref, mb_ref), s=s)

    mavg = _block_avg(w, HEAD_W)

    def out(st, i):
        y = _head_rmsnorm(accf_ref[pl.ds(st, rb), :] + accb_ref[pl.ds(st, rb), :], g_ref[...], mavg)
        o_ref[pl.ds(st, rb), :] = jax.nn.sigmoid(z_ref[pl.ds(st, rb), 768:1024]) * y
    _rows(s, rb, out)


def _mlstm(zb, cw, cb, g, nb, s):
    w = BRANCH_W
    return pl.pallas_call(
        functools.partial(_mlstm_kernel, s=s),
        grid=(nb,),
        in_specs=[pl.BlockSpec((None, s, ZB_W), lambda b: (b, 0, 0)),
                  _const_spec((3, 2 * w)), _const_spec((1, 2 * w)), _const_spec((1, w))],
        out_specs=pl.BlockSpec((None, s, w), lambda b: (b, 0, 0)),
        out_shape=jax.ShapeDtypeStruct((nb, s, w), f32),
        scratch_shapes=[pltpu.VMEM((s, w), f32)] * 4 + [pltpu.VMEM((w, w), f32)] * 2 + [pltpu.VMEM((1, w), f32)] * 4,
        compiler_params=_cparams("parallel"),
        name="mlstm",
    )(zb.reshape(nb, s, ZB_W), cw, cb, g)


def _rope(x, cos, sa, sb):
    w = x.shape[-1]
    half = HEAD_W // 2
    return x * cos + pltpu.roll(x, w - half, 1) * sa + pltpu.roll(x, half, 1) * sb


def _dil_bias(s):
    rel = jnp.arange(s, dtype=i32)[None, :] - jnp.arange(s, dtype=i32)[:, None]
    dist = jnp.abs(rel)
    cnt = jnp.zeros((s, s), f32)
    for window, dil in DIL_PATTERNS:
        ok = dist <= (window // (2 * dil)) * dil
        if dil > 1:
            ok = jnp.logical_and(ok, (rel & (dil - 1)) == 0)
        cnt = cnt + jnp.where(ok, 1.0, 0.0)
    return jnp.where(cnt > 0.0, jnp.log(jnp.maximum(cnt, 1.0)), NEG_BIG)


def _dil_kernel(z_ref, cos_ref, sa_ref, sb_ref, bias_ref, o_ref, kr_ref, vm_ref, *, s, tq):
    w = BRANCH_W
    qi = pl.program_id(1)
    rb = 256

    @pl.when(qi == 0)
    def _():
        head = _iota((rb, w), 1) // HEAD_W

        def prep(st, i):
            x = z_ref[pl.ds(st, rb), w:2 * w]
            kr_ref[pl.ds(st, rb), :] = _rope(x, cos_ref[pl.ds(st, rb), :], sa_ref[pl.ds(st, rb), :],
                                             sb_ref[pl.ds(st, rb), :]).astype(bf16)
            v = z_ref[pl.ds(st, rb), 2 * w:3 * w]
            for j in range(HEADS):
                vm_ref[j, pl.ds(st, rb), :] = jnp.where(head == j, v, 0.0).astype(bf16)
        _rows(s, rb, prep)

    q0 = pl.multiple_of(qi * tq, tq)
    qr = _rope(z_ref[pl.ds(q0, tq), 0:w], cos_ref[pl.ds(q0, tq), :], sa_ref[pl.ds(q0, tq), :],
               sb_ref[pl.ds(q0, tq), :]) * (HEAD_W ** -0.5)
    qhead = _iota((tq, w), 1) // HEAD_W
    acc = jnp.zeros((tq, w), f32)
    for j in range(HEADS):
        qh = jnp.where(qhead == j, qr, 0.0).astype(bf16)
        sc = _mm_nt(qh, kr_ref[...]) + bias_ref[...]
        m = jnp.max(sc, axis=-1, keepdims=True)
        p = jnp.exp(sc - m)
        l = jnp.sum(p, axis=-1, keepdims=True)
        acc = acc + _mm(p.astype(bf16), vm_ref[j]) / l
    o_ref[...] = acc


def _dil(zc, cos, sa, sb, bias, nb, s, tq=256):
    w = BRANCH_W
    for _, dil in DIL_PATTERNS:
        assert dil & (dil - 1) == 0 and s % dil == 0
    return pl.pallas_call(
        functools.partial(_dil_kernel, s=s, tq=tq),
        grid=(nb, s // tq),
        in_specs=[pl.BlockSpec((None, s, ZC_W), lambda b, i: (b, 0, 0)),
                  _const_spec((s, w)), _const_spec((s, w)), _const_spec((s, w)),
                  pl.BlockSpec((tq, s), lambda b, i: (i, 0))],
        out_specs=pl.BlockSpec((None, tq, w), lambda b, i: (b, i, 0)),
        out_shape=jax.ShapeDtypeStruct((nb, s, w), f32),
        scratch_shapes=[pltpu.VMEM((s, w), bf16), pltpu.VMEM((HEADS, s, w), bf16)],
        compiler_params=_cparams("parallel", "arbitrary"),
        name="dilattn",
    )(zc.reshape(nb, s, ZC_W), cos, sa, sb, bias)


def _merge_kernel(h_ref, ya, yb, yc, yd, g_ref, wg_ref, wbr_ref, wo_ref, o_ref):
    x = h_ref[...]
    u = _rms(x, g_ref[...]).astype(bf16)
    merged = None
    for n, y in enumerate((ya, yb, yc, yd)):
        gate = jax.nn.sigmoid(_mm(u, wg_ref[n]))
        term = gate * _mm(y[...].astype(bf16), wbr_ref[n])
        merged = term if merged is None else merged + term
    o_ref[...] = x + _mm(merged.astype(bf16), wo_ref[...])


def _merge(h, ys, g, wg, wbr, wo, tm=512):
    t = h.shape[0]
    tok = lambda w: pl.BlockSpec((tm, w), lambda i: (i, 0))
    return pl.pallas_call(
        _merge_kernel,
        grid=(t // tm,),
        in_specs=[tok(D_MODEL)] + [tok(BRANCH_W)] * 4
        + [_const_spec((1, D_MODEL)), _const_spec((4, D_MODEL, D_MODEL)),
           _const_spec((4, BRANCH_W, D_MODEL)), _const_spec((D_MODEL, D_MODEL))],
        out_specs=tok(D_MODEL),
        out_shape=jax.ShapeDtypeStruct((t, D_MODEL), f32),
        compiler_params=_cparams("parallel"),
        name="merge",
    )(h, *ys, g, wg, wbr, wo)


def _stair_cells():
    return [(a, b) for a in range(PEER_TOPK) for b in range(PEER_TOPK) if (a + 1) * (b + 1) <= PEER_TOPK]


_CELLS = _stair_cells()
_NCELL = len(_CELLS)
_NCELL_PAD = -(-_NCELL // 8) * 8


def _extract_top(works, rowf, nrows):
    works = list(works)
    tops = [[] for _ in works]
    ranks = [jnp.full(w.shape, float(PEER_TOPK), f32) for w in works]
    for r in range(PEER_TOPK):
        mxs = [jnp.max(w, axis=0, keepdims=True) for w in works]
        cands = [jnp.where(w == mx, rowf, float(nrows)) for w, mx in zip(works, mxs)]
        hits = [cand == jnp.min(cand, axis=0, keepdims=True) for cand in cands]
        works = [jnp.where(hit, -jnp.inf, w) for hit, w in zip(hits, works)]
        ranks = [jnp.where(hit, float(r), rank) for hit, rank in zip(hits, ranks)]
        for t, mx in zip(tops, mxs):
            t.append(mx)
    return tops, ranks


def _distinct_maxima(xs):
    tops = [[] for _ in xs]
    mxs = [None] * len(xs)
    for _ in range(PEER_TOPK):
        curs = [x if mx is None else jnp.where(x < mx, x, -jnp.inf) for x, mx in zip(xs, mxs)]
        mxs = [jnp.max(cur, axis=0, keepdims=True) for cur in curs]
        for t, mx in zip(tops, mxs):
            t.append(mx)
    return tops


def _count(mask):
    return jnp.sum(jnp.where(mask, 1.0, 0.0), axis=0, keepdims=True)


def _route_kernel(h_ref, g_ref, wpq_ref, keys_ref, sa_ref, ut_ref, a_ref, cnt_ref, rk_ref, bn_ref,
                  q_ref, *m_refs, tm):
    uf = _rms(h_ref[...], g_ref[...])
    ut_ref[...] = uf.T.astype(bf16)
    q_ref[...] = _mm(uf.astype(bf16), wpq_ref[...]).astype(bf16)
    nk = PEER_NKEYS
    topk = float(PEER_TOPK)

    def scores(hd, toks):
        return [_mm_nt(keys_ref[hd * 2 + p], q_ref[toks, pl.ds(pl.multiple_of((hd * 2 + p) * nk, nk), nk)])
                for p in range(2)]

    def write_tables(hd, amat, cnt, rank2, bnorm):
        rows = pl.ds(pl.multiple_of(hd * nk, nk), nk)
        prow = pl.ds(pl.multiple_of(hd * (nk // 2), nk // 2), nk // 2)
        rkp = pltpu.bitcast(rank2.astype(bf16), jnp.uint32)
        bnp = pltpu.bitcast(bnorm.astype(bf16), jnp.uint32)
        for tb in range(tm // LANES):
            lanes = slice(tb * LANES, (tb + 1) * LANES)
            a_ref[tb, rows, :] = amat[:, lanes]
            cnt_ref[tb, rows, :] = cnt[:, lanes]
            rk_ref[tb, prow, :] = rkp[:, lanes]
            bn_ref[tb, prow, :] = bnp[:, lanes]

    def stair_sums(tops, m_ref):
        m_ref[...] = jnp.full((_NCELL_PAD, tm), -jnp.inf, f32)
        for ci, (a, b) in enumerate(_CELLS):
            m_ref[ci:ci + 1, :] = tops[0][a] + tops[1][b]
        return m_ref[...]

    def fast_tiles(hds):
        scs = [scores(hd, slice(0, tm)) for hd in hds]
        flat = _distinct_maxima([s for pair in scs for s in pair])
        tops = [flat[2 * i:2 * i + 2] for i in range(len(hds))]
        rank2s = [jnp.zeros((nk, tm), f32) for _ in hds]
        for r in range(PEER_TOPK):
            rank2s = [rank2 + jnp.where(sc[1] < top[1][r], 1.0, 0.0) for rank2, sc, top in zip(rank2s, scs, tops)]
        tied = jnp.zeros((1, tm), f32)
        for sc, top, rank2 in zip(scs, tops, rank2s):
            tied = jnp.where(_count(sc[0] >= top[0][-1]) == topk, tied, 1.0)
            tied = jnp.where(_count(rank2 < topk) == topk, tied, 1.0)
        cands = [stair_sums(top, m_ref) for top, m_ref in zip(tops, m_refs)]
        ctops = _distinct_maxima(cands)
        for hd, sc, top, rank2, cand, ctop in zip(hds, scs, tops, rank2s, cands, ctops):
            chosen = cand >= ctop[-1]
            tied = jnp.where(_count(chosen) == topk, tied, 1.0)
            zsum = jnp.sum(jnp.where(chosen, jnp.exp(cand - cand[0:1]), 0.0), axis=0, keepdims=True)
            rowcnt = _mm(sa_ref[...], jnp.where(chosen, 1.0, 0.0).astype(bf16))
            cnt = jnp.zeros((nk, tm), f32)
            for a in range(PEER_TOPK):
                cnt = jnp.where(sc[0] == top[0][a], rowcnt[a:a + 1], cnt)
            write_tables(hd, jnp.exp(sc[0] - top[0][0]), cnt, rank2, jnp.exp(sc[1] - top[1][0]) / zsum)
        return tied

    def exact_tile(hd, m_ref):
        rowf = _iota((nk, tm), 0).astype(f32)
        rowcf = _iota((_NCELL_PAD, tm), 0).astype(f32)
        scs = scores(hd, slice(0, tm))
        tops, ranks = _extract_top(scs, rowf, nk)
        cand = stair_sums(tops, m_ref)
        _, (crank,) = _extract_top([cand], rowcf, _NCELL_PAD)
        chosen = crank < topk
        zsum = jnp.sum(jnp.where(chosen, jnp.exp(cand - cand[0:1]), 0.0), axis=0, keepdims=True)
        rowcnt = _mm(sa_ref[...], jnp.where(chosen, 1.0, 0.0).astype(bf16))
        cnt = jnp.zeros((nk, tm), f32)
        for a in range(PEER_TOPK):
            cnt = jnp.where(ranks[0] == float(a), rowcnt[a:a + 1], cnt)
        write_tables(hd, jnp.exp(scs[0] - tops[0][0]), cnt, ranks[1], jnp.exp(scs[1] - tops[1][0]) / zsum)

    def per_group(hg, carry):
        hds = [hg * ROUTE_HEADS_TOGETHER + j for j in range(ROUTE_HEADS_TOGETHER)]
        tied = fast_tiles(hds)

        @pl.when(jnp.max(tied) > 0.0)
        def _():
            for hd, m_ref in zip(hds, m_refs):
                exact_tile(hd, m_ref)
        return carry

    lax.fori_loop(0, PEER_HEADS // ROUTE_HEADS_TOGETHER, per_group, 0)


def _route(h, g, wpq, keys, sa, tm=256):
    t = h.shape[0]
    nh, nk = PEER_HEADS, PEER_NKEYS
    tab = pl.BlockSpec((tm // LANES, nh * nk, LANES), lambda i: (i, 0, 0))
    ptab = pl.BlockSpec((tm // LANES, nh * nk // 2, LANES), lambda i: (i, 0, 0))
    return pl.pallas_call(
        functools.partial(_route_kernel, tm=tm),
        grid=(t // tm,),
        in_specs=[pl.BlockSpec((tm, D_MODEL), lambda i: (i, 0)), _const_spec((1, D_MODEL)),
                  _const_spec((D_MODEL, 2 * nh * nk)), _const_spec((2 * nh, nk, nk)),
                  _const_spec((PEER_TOPK, _NCELL_PAD))],
        out_specs=[pl.BlockSpec((D_MODEL, tm), lambda i: (0, i)), tab, tab, ptab, ptab],
        out_shape=[jax.ShapeDtypeStruct((D_MODEL, t), bf16)] + [jax.ShapeDtypeStruct((t // LANES, nh * nk, LANES), f32)] * 2
        + [jax.ShapeDtypeStruct((t // LANES, nh * nk // 2, LANES), jnp.uint32)] * 2,
        scratch_shapes=[pltpu.VMEM((tm, 2 * nh * nk), bf16)]
        + [pltpu.VMEM((_NCELL_PAD, tm), f32)] * ROUTE_HEADS_TOGETHER,
        compiler_params=_cparams("parallel"),
        name="route",
    )(h, g, wpq, keys, sa)


def _gelu_bf16(x):
    return (0.5 * x * (1.0 + lax.erf(x * math.sqrt(0.5)))).astype(bf16)


def _peer_gate_block(act_ref, p_ref, tabs, i0, ii, tb):
    a_ref, cnt_ref, rk_ref, bn_ref = tabs
    nk = PEER_NKEYS
    lanes = slice(tb * LANES, (tb + 1) * LANES)
    gate = None
    for hd in range(PEER_HEADS):
        row = pl.ds(hd * nk + i0 + ii, 1)
        a = jnp.broadcast_to(a_ref[tb, row, :], (nk, LANES)).astype(bf16)
        c = jnp.broadcast_to(cnt_ref[tb, row, :], (nk, LANES)).astype(bf16)
        prow = slice(hd * (nk // 2), (hd + 1) * (nk // 2))
        rk = pltpu.bitcast(rk_ref[tb, prow, :], bf16)
        bn = pltpu.bitcast(bn_ref[tb, prow, :], bf16)
        term = a * jnp.where(rk < c, bn, jnp.zeros((), bf16))
        gate = term if gate is None else gate + term
    p_ref[ii * nk:(ii + 1) * nk, lanes] = gate * act_ref[ii * nk:(ii + 1) * nk, lanes]


def _mxu_product(rhs_tile, lhs_tile, between):
    nt = PEER_SUB // MXU_TILE
    for q in range(2):
        pltpu.matmul_push_rhs(rhs_tile(0, q), staging_register=0, mxu_index=q)
    for kt in range(nt):
        reg = kt % 2
        for mc in range(nt):
            lhs = lhs_tile(mc, kt)
            for q in range(2):
                pltpu.matmul_acc_lhs(mc * (MXU_TILE // 4), lhs, mxu_index=q,
                                     load_staged_rhs=(reg if mc == 0 else None))
            if mc == 0 and kt + 1 < nt:
                for q in range(2):
                    pltpu.matmul_push_rhs(rhs_tile(kt + 1, q), staging_register=1 - reg, mxu_index=q)
            between()


def _peer_kernel(ut_ref, a_ref, cnt_ref, rk_ref, bn_ref, pu_ref, pvt_ref, o_ref, acc_ref, act0, act1, p0, p1,
                 *, ne, tp):
    g = pl.program_id(0)
    nblocks = pl.num_programs(0) - 2
    b1 = jnp.clip(g - 1, 0, nblocks - 1)
    b2 = jnp.clip(g - 2, 0, nblocks - 1)
    i1 = (b1 % ne) * (PEER_SUB // PEER_NKEYS)
    tabs = (a_ref, cnt_ref, rk_ref, bn_ref)

    @pl.when(g == 0)
    def _():
        for ref in (act0, act1, p0, p1):
            ref[...] = jnp.zeros(ref.shape, bf16)

    @pl.when(b2 % ne == 0)
    def _():
        acc_ref[...] = jnp.zeros(acc_ref.shape, f32)

    t256 = lambda i: slice(i * MXU_TILE, (i + 1) * MXU_TILE)

    def stage(act_w, act_r, p_w, p_r):
        blocks = iter([(ii, tb) for ii in range(PEER_SUB // PEER_NKEYS) for tb in range(tp // LANES)])

        def one_gate_block():
            ii, tb = next(blocks)
            _peer_gate_block(act_r, p_w, tabs, i1, ii, tb)

        nt = PEER_SUB // MXU_TILE
        _mxu_product(lambda kt, q: ut_ref[t256(kt), t256(q)], lambda mc, kt: pu_ref[t256(mc), t256(kt)],
                     one_gate_block)
        for mc in range(nt):
            for q in range(2):
                x = pltpu.matmul_pop(mc * (MXU_TILE // 4), (MXU_TILE, MXU_TILE), f32, q)
                act_w[t256(mc), t256(q)] = _gelu_bf16(x)
        _mxu_product(lambda kt, q: p_r[t256(kt), t256(q)], lambda mc, kt: pvt_ref[t256(mc), t256(kt)],
                     one_gate_block)
        for mc in range(nt):
            for q in range(2):
                acc_ref[t256(mc), t256(q)] += pltpu.matmul_pop(mc * (MXU_TILE // 4), (MXU_TILE, MXU_TILE), f32, q)

    @pl.when(g % 2 == 0)
    def _():
        stage(act0, act1, p1, p0)

    @pl.when(g % 2 == 1)
    def _():
        stage(act1, act0, p0, p1)

    @pl.when(jnp.logical_and(g >= 2, b2 % ne == ne - 1))
    def _():
        o_ref[...] = acc_ref[...].T


def _peer(ut, tabs, pu, pvt):
    t = ut.shape[1]
    tp = 2 * MXU_TILE
    nh, nk = PEER_HEADS, PEER_NKEYS
    assert D_MODEL == PEER_SUB and (PEER_SUB // nk) * (tp // LANES) == 2 * (PEER_SUB // MXU_TILE) ** 2
    ne = pu.shape[0] // PEER_SUB
    nblocks = (t // tp) * ne
    blk = lambda lag: (lambda g: jnp.clip(g - lag, 0, nblocks - 1))
    b0, b1, b2 = blk(0), blk(1), blk(2)
    tab = pl.BlockSpec((tp // LANES, nh * nk, LANES), lambda g: (b1(g) // ne, 0, 0))
    ptab = pl.BlockSpec((tp // LANES, nh * nk // 2, LANES), lambda g: (b1(g) // ne, 0, 0))
    return pl.pallas_call(
        functools.partial(_peer_kernel, ne=ne, tp=tp),
        grid=(nblocks + 2,),
        in_specs=[pl.BlockSpec((D_MODEL, tp), lambda g: (0, b0(g) // ne)), tab, tab, ptab, ptab,
                  pl.BlockSpec((PEER_SUB, D_MODEL), lambda g: (b0(g) % ne, 0)),
                  pl.BlockSpec((D_MODEL, PEER_SUB), lambda g: (0, b2(g) % ne))],
        out_specs=pl.BlockSpec((tp, D_MODEL), lambda g: (b2(g) // ne, 0)),
        out_shape=jax.ShapeDtypeStruct((t, D_MODEL), f32),
        scratch_shapes=[pltpu.VMEM((D_MODEL, tp), f32)] + [pltpu.VMEM((PEER_SUB, tp), bf16)] * 4,
        compiler_params=_cparams("arbitrary"),
        name="peer",
    )(ut, *tabs, pu, pvt)


def _pe_kernel(h_ref, po_ref, p_ref, g_ref, wpg_ref, wpe_ref, gf_ref, o_ref, *, final):
    h = h_ref[...] + po_ref[...]
    gate = jax.nn.sigmoid(_mm(_rms(h, g_ref[...]).astype(bf16), wpg_ref[...]))
    h = h + gate * _mm(p_ref[...].astype(bf16), wpe_ref[...])
    if final:
        h = _rms(h, gf_ref[...])
    o_ref[...] = h


def _pe(h, po, p, g, wpg, wpe, gf, final, tm=512):
    t = h.shape[0]
    tok = lambda w: pl.BlockSpec((tm, w), lambda i: (i, 0))
    return pl.pallas_call(
        functools.partial(_pe_kernel, final=final),
        grid=(t // tm,),
        in_specs=[tok(D_MODEL), tok(D_MODEL), tok(PE_DIM), _const_spec((1, D_MODEL)),
                  _const_spec((D_MODEL, D_MODEL)), _const_spec((PE_DIM, D_MODEL)), _const_spec((1, D_MODEL))],
        out_specs=tok(D_MODEL),
        out_shape=jax.ShapeDtypeStruct((t, D_MODEL), f32),
        compiler_params=_cparams("parallel"),
        name="pe_embed",
    )(h, po, p, g, wpg, wpe, gf)


def _cols(w, name):
    a, b = _PROJ_OFF[name]
    return w[..., a:b]


def _proj_groups(w):
    pad = jnp.zeros(w.shape[:-1] + (ZA_W - 800,), w.dtype)
    ga = jnp.concatenate([_cols(w, n) for n in ('a_q', 'a_k', 'a_v', 'a_g', 'a_lr')] + [pad], axis=-1)
    gates = [jnp.repeat(_cols(w, n), HEAD_W, axis=-1) for n in ('b_i', 'b_f')]
    gb = jnp.concatenate([_cols(w, n) for n in ('b_q', 'b_k', 'b_v', 'b_o')] + gates, axis=-1)
    gc = jnp.concatenate([_cols(w, n) for n in ('c_q', 'c_k', 'c_v')], axis=-1)
    gd = jnp.concatenate([_cols(w, n) for n in ('d_q', 'd_f', 'd_i', 'd_g')], axis=-1)
    return ga, gb, gc, gd


def _rope_tables(s):
    half = HEAD_W // 2
    inv = ROPE_THETA ** (-jnp.arange(half, dtype=f32) / half)
    ang = jnp.arange(s, dtype=f32)[:, None] * inv[None, :]
    cos, sin = jnp.cos(ang), jnp.sin(ang)
    zero = jnp.zeros_like(sin)
    tile = lambda a, b: jnp.tile(jnp.concatenate([a, b], axis=-1), (1, HEADS))
    return tile(cos, cos), tile(-sin, zero), tile(zero, sin)


def _row_count_matrix():
    m = np.zeros((PEER_TOPK, _NCELL_PAD), np.float32)
    for ci, (a, _) in enumerate(_CELLS):
        m[a, ci] = 1.0
    return jnp.asarray(m, bf16)


def _prepare(g_mix, w_in, b_in, w_lr2, b_lr2, conv_w, conv_b, lb_param, g_a, g_b, g_d, w_gate, w_br, w_out,
             g_ffn, w_pq, sub_keys, peer_u, peer_v, g_pe, w_pe, w_pg, g_final):
    layers = []
    row = lambda a: a.reshape(1, -1)
    for l in range(DEPTH):
        wlr = jnp.zeros((LANES, 256), f32)
        for n in range(2):
            wlr = wlr.at[n * GLA_RANK:(n + 1) * GLA_RANK, n * 128:(n + 1) * 128].set(w_lr2[l, n])
        layers.append(dict(
            g_mix=row(g_mix[l]),
            w_proj=[w.astype(bf16) for w in _proj_groups(w_in[l])],
            b_proj=list(_proj_groups(row(b_in[l]))),
            wlr=wlr.astype(bf16), blr=row(b_lr2[l]),
            conv_w=conv_w[l], conv_b=row(conv_b[l]),
            g_a=row(g_a[l]), g_b=row(g_b[l]), g_d=row(g_d[l]),
            w_gate=w_gate[l].astype(bf16), w_br=w_br[l].astype(bf16), w_out=w_out[l].astype(bf16),
            g_ffn=row(g_ffn[l]), w_pq=w_pq[l].astype(bf16),
            keys=sub_keys[l].reshape(2 * PEER_HEADS, PEER_NKEYS, -1).astype(bf16),
            peer_u=peer_u[l].astype(bf16), peer_vt=peer_v[l].astype(bf16).T,
            g_pe=row(g_pe[l]), w_pe=w_pe[l].astype(bf16), w_pg=w_pg[l].astype(bf16),
        ))
    shared = dict(
        lbp=jnp.transpose(lb_param, (1, 0, 2)).reshape(DEPTH, -1),
        g_final=row(g_final), sa=_row_count_matrix(),
    )
    return layers, shared


def _trunk(x, p, layers, shared):
    nb, s, d = x.shape
    t = nb * s
    h = x.reshape(t, d)
    cos, sa, sb = _rope_tables(s)
    dil_bias = _dil_bias(s)
    for l, w in enumerate(layers):
        za, zb, zc, zd = _proj(h, w['g_mix'], w['w_proj'], w['b_proj'])
        ya = _gla(za, w['wlr'], w['blr'], w['g_a'], nb, s)
        yb = _mlstm(zb, w['conv_w'], w['conv_b'], w['g_b'], nb, s)
        yc = _dil(zc, cos, sa, sb, dil_bias, nb, s)
        yd = _hgrn(zd, shared['lbp'], w['g_d'], nb, s, l)
        ys = [y.reshape(t, BRANCH_W) for y in (ya, yb, yc, yd)]
        h1 = _merge(h, ys, w['g_mix'], w['w_gate'], w['w_br'], w['w_out'])
        ut, *tabs = _route(h1, w['g_ffn'], w['w_pq'], w['keys'], shared['sa'])
        po = _peer(ut, tabs, w['peer_u'], w['peer_vt'])
        h = _pe(h1, po, p[l].reshape(t, PE_DIM), w['g_pe'], w['w_pg'], w['w_pe'], shared['g_final'],
                final=(l == DEPTH - 1))
    return h.reshape(nb, s, d)


def kernel(x_prompt, x_sample, p_prompt, p_sample, g_mix, w_in, b_in, w_lr2, b_lr2, conv_w, conv_b, lb_param, g_a, g_b, g_d, w_gate, w_br, w_out, g_ffn, w_pq, sub_keys, peer_u, peer_v, g_pe, w_pe, w_pg, g_final):
    layers, shared = _prepare(g_mix, w_in, b_in, w_lr2, b_lr2, conv_w, conv_b, lb_param, g_a, g_b, g_d, w_gate,
                              w_br, w_out, g_ffn, w_pq, sub_keys, peer_u, peer_v, g_pe, w_pe, w_pg, g_final)
    return (_trunk(x_prompt, p_prompt, layers, shared), _trunk(x_sample, p_sample, layers, shared))
```

```python
import functools
import math

import numpy as np
import jax
import jax.numpy as jnp
from jax import lax
from jax.experimental import pallas as pl
from jax.experimental.pallas import tpu as pltpu

f32 = jnp.float32
bf16 = jnp.bfloat16
i32 = jnp.int32

D_MODEL = 1024
DEPTH = 2
PE_DIM = 256
EPS = 1e-6
NEG_BIG = -1e30
HEADS = 4
HEAD_W = 64
BRANCH_W = HEADS * HEAD_W
GLA_DK = 32
GLA_RANK = 16
GLA_NORMALIZER = 16.0
GLA_CHUNK = 32
MLSTM_CHUNK = 64
HGRN_DF = 64
HGRN_CHUNK = 32
DIL_PATTERNS = ((128, 1), (512, 4), (2048, 16))
ROPE_THETA = 10000.0
PEER_HEADS = 8
PEER_NKEYS = 128
PEER_TOPK = 16
LANES = 128
PEER_SUB = 1024
MXU_TILE = 256
ROUTE_HEADS_TOGETHER = 2
EXP_CLAMP = 80.0
GL_UNROLL = 8
MLSTM_UNROLL = 2

_PROJ_LAYOUT = (
    ('a_q', 128), ('a_k', 128), ('a_v', 256), ('a_g', 256), ('a_lr', 32),
    ('b_q', 256), ('b_k', 256), ('b_v', 256), ('b_i', 8), ('b_f', 8), ('b_o', 256),
    ('c_q', 256), ('c_k', 256), ('c_v', 256),
    ('d_q', 256), ('d_f', 512), ('d_i', 256), ('d_g', 256),
)
_PROJ_OFF = {}
_o = 0
for _n, _s in _PROJ_LAYOUT:
    _PROJ_OFF[_n] = (_o, _o + _s)
    _o += _s


def _lane_layout(*fields):
    out, start = {}, 0
    for name, width in fields:
        assert width % 128 == 0
        out[name] = slice(start, start + width)
        start += width
    return out, start


def _sub(field, index, width):
    return slice(field.start + index * width, field.start + (index + 1) * width)


ZA, ZA_W = _lane_layout(('q', 128), ('k', 128), ('v', 256), ('g', 256), ('lr', 128))
ZB, ZB_W = _lane_layout(('q', 256), ('k', 256), ('v', 256), ('o', 256), ('i', 512), ('f', 512))
ZC, ZC_W = _lane_layout(('q', 256), ('k', 256), ('v', 256))
ZD, ZD_W = _lane_layout(('q', 256), ('f', 512), ('i', 256), ('g', 256))
ROW_BLOCK = 256

VMEM_LIMIT = 56 * 1024 * 1024


def _cparams(*sem):
    return pltpu.CompilerParams(dimension_semantics=sem, vmem_limit_bytes=VMEM_LIMIT)


def _const_spec(shape):
    nd = len(shape)
    return pl.BlockSpec(shape, lambda *_: (0,) * nd, pipeline_mode=pl.Buffered(1))


def _mm(a, b):
    return jnp.dot(a, b, preferred_element_type=f32)


def _mm_nt(a, b):
    return lax.dot_general(a, b, (((1,), (1,)), ((), ())), preferred_element_type=f32)


def _mm_tn(a, b):
    return lax.dot_general(a, b, (((0,), (0,)), ((), ())), preferred_element_type=f32)


def _split3(x):
    hi = x.astype(bf16)
    r1 = x - hi.astype(f32)
    mid = r1.astype(bf16)
    lo = (r1 - mid.astype(f32)).astype(bf16)
    return hi, mid, lo


def _lmul3(m, x):
    w = x.shape[1]
    y = _mm(m, jnp.concatenate(_split3(x), axis=1))
    return y[:, 0:w] + y[:, w:2 * w] + y[:, 2 * w:3 * w]


def _rmul3(x, m):
    r = x.shape[0]
    y = _mm(jnp.concatenate(_split3(x), axis=0), m)
    return y[0:r] + y[r:2 * r] + y[2 * r:3 * r]


def _rms(x, g):
    return x * lax.rsqrt(jnp.mean(x * x, axis=-1, keepdims=True) + EPS) * g


def _iota(shape, dim):
    return lax.broadcasted_iota(i32, shape, dim)


def _tri(c, reverse):
    r, k = _iota((c, c), 0), _iota((c, c), 1)
    return jnp.where((k >= r) if reverse else (k <= r), 1.0, 0.0).astype(bf16)


def _block_avg(w, blk):
    same = (_iota((w, w), 0) // blk) == (_iota((w, w), 1) // blk)
    return jnp.where(same, 1.0 / blk, 0.0).astype(bf16)


def _rows(n, blk, fn):
    def body(i, c):
        fn(pl.multiple_of(i * blk, blk), i)
        return c
    lax.fori_loop(0, n // blk, body, 0)


def _head_rmsnorm(o, g, mavg):
    ms = _rmul3(o * o, mavg)
    return o * lax.rsqrt(ms + EPS) * g


def _proj_kernel(h_ref, g_ref, wa, wb, wc, wd, ba, bb, bc, bd, za, zb, zc, zd):
    u = _rms(h_ref[...], g_ref[...]).astype(bf16)
    for w, b, z in ((wa, ba, za), (wb, bb, zb), (wc, bc, zc), (wd, bd, zd)):
        z[...] = _mm(u, w[...]) + b[...]


def _proj(h, g, ws, bs, tm=512):
    t = h.shape[0]
    widths = (ZA_W, ZB_W, ZC_W, ZD_W)
    tok = lambda w: pl.BlockSpec((tm, w), lambda i: (i, 0))
    return pl.pallas_call(
        _proj_kernel,
        grid=(t // tm,),
        in_specs=[tok(D_MODEL), _const_spec((1, D_MODEL))]
        + [_const_spec((D_MODEL, w)) for w in widths] + [_const_spec((1, w)) for w in widths],
        out_specs=[tok(w) for w in widths],
        out_shape=[jax.ShapeDtypeStruct((t, w), f32) for w in widths],
        compiler_params=_cparams("parallel"),
        name="proj",
    )(h, g, *ws, *bs)


def _gl_bidir(loads, acc_refs, st_refs, *, s, c, dk):
    h, dv = HEADS, HEAD_W
    kw, vw, hc = h * dk, h * dv, h * c
    n, un = s // c, GL_UNROLL
    khead = _iota((c, kw), 1) // dk
    vhead = _iota((c, vw), 1) // dv
    sidx, trow = _iota((c, hc), 1) % c, _iota((c, hc), 0)
    bd = (_iota((vw, kw), 0) // dv) == (_iota((vw, kw), 1) // dk)
    tris = (_tri(c, False), _tri(c, True))
    causals = (sidx <= trow, sidx >= trow)
    for st_ref in st_refs:
        st_ref[...] = jnp.zeros((vw, kw), f32)

    streams = [(d, u) for u in range(un) for d in range(2)]

    def body(ci, carry):
        idx = [(ci * un + u) if d == 0 else (n - 1 - ci * un - u) for d, u in streams]
        sts = [pl.multiple_of(i * c, c) for i in idx]
        ins = [loads[d](st) for (d, _), st in zip(streams, sts)]
        qs, ks, vs = [x[0] for x in ins], [x[1] for x in ins], [x[2] for x in ins]
        bs = [_lmul3(tris[d], x[3]) for (d, _), x in zip(streams, ins)]
        lasts = [b[0:1] if d == 1 else b[c - 1:c] for (d, _), b in zip(streams, bs)]
        mids = [b[c // 2:c // 2 + 1] for b in bs]
        qts = [(q * jnp.exp(jnp.minimum(b - m, EXP_CLAMP))).astype(bf16) for q, b, m in zip(qs, bs, mids)]
        kts = [k * jnp.exp(jnp.minimum(m - b, EXP_CLAMP)) for k, b, m in zip(ks, bs, mids)]
        kexps = [jnp.concatenate([jnp.where(khead == j, kt, 0.0).astype(bf16) for j in range(h)], axis=0)
                 for kt in kts]
        vexps = [jnp.concatenate([jnp.where(vhead == j, v, 0.0).astype(bf16) for j in range(h)], axis=0)
                 for v in vs]
        atts = [jnp.where(causals[d], _mm_nt(qt, kexp), 0.0).astype(bf16)
                for (d, _), qt, kexp in zip(streams, qts, kexps)]
        os_ = [_mm(att, vexp) for att, vexp in zip(atts, vexps)]
        qis = [(q * jnp.exp(b)).astype(bf16) for q, b in zip(qs, bs)]
        kds = [(k * jnp.exp(l - b)).astype(bf16) for k, l, b in zip(ks, lasts, bs)]
        upds = [jnp.where(bd, _mm_tn(v.astype(bf16), kd), 0.0) for v, kd in zip(vs, kds)]
        decs = [jnp.exp(l) for l in lasts]
        states = [st_ref[...] for st_ref in st_refs]
        for i, (d, _) in enumerate(streams):
            acc_refs[d][pl.ds(sts[i], c), :] = os_[i] + _mm_nt(qis[i], states[d].astype(bf16))
            states[d] = states[d] * decs[i] + upds[i]
        for d in range(2):
            st_refs[d][...] = states[d]
        return carry

    lax.fori_loop(0, n // un, body, 0)


def _gla_kernel(z_ref, wlr_ref, blr_ref, g_ref, o_ref, gk_ref, accf_ref, accb_ref, stf_ref, stb_ref, *, s):
    rb = ROW_BLOCK

    def gates(st, i):
        lr = z_ref[pl.ds(st, rb), ZA['lr']].astype(bf16)
        x = _mm(lr, wlr_ref[...]) + blr_ref[...]
        gk_ref[pl.ds(st, rb), :] = jax.nn.log_sigmoid(x) * (1.0 / GLA_NORMALIZER)
    _rows(s, rb, gates)

    scale = GLA_DK ** -0.5

    def load(st, d):
        q = z_ref[pl.ds(st, GLA_CHUNK), ZA['q']] * scale
        k = z_ref[pl.ds(st, GLA_CHUNK), ZA['k']]
        v = z_ref[pl.ds(st, GLA_CHUNK), ZA['v']]
        g = gk_ref[pl.ds(st, GLA_CHUNK), d * 128:(d + 1) * 128]
        return q, k, v, g
    _gl_bidir([functools.partial(load, d=d) for d in range(2)], (accf_ref, accb_ref), (stf_ref, stb_ref),
              s=s, c=GLA_CHUNK, dk=GLA_DK)

    mavg = _block_avg(BRANCH_W, HEAD_W)

    def out(st, i):
        y = _head_rmsnorm(accf_ref[pl.ds(st, rb), :] + accb_ref[pl.ds(st, rb), :], g_ref[...], mavg)
        gate = z_ref[pl.ds(st, rb), ZA['g']]
        o_ref[pl.ds(st, rb), :] = y * (gate * jax.nn.sigmoid(gate))
    _rows(s, rb, out)


def _gla(za, wlr, blr, g, nb, s):
    return pl.pallas_call(
        functools.partial(_gla_kernel, s=s),
        grid=(nb,),
        in_specs=[pl.BlockSpec((None, s, ZA_W), lambda b: (b, 0, 0)),
                  _const_spec((LANES, 256)), _const_spec((1, 256)), _const_spec((1, BRANCH_W))],
        out_specs=pl.BlockSpec((None, s, BRANCH_W), lambda b: (b, 0, 0)),
        out_shape=jax.ShapeDtypeStruct((nb, s, BRANCH_W), f32),
        scratch_shapes=[pltpu.VMEM((s, 256), f32)] + [pltpu.VMEM((s, BRANCH_W), f32)] * 2
        + [pltpu.VMEM((BRANCH_W, HEADS * GLA_DK), f32)] * 2,
        compiler_params=_cparams("parallel"),
        name="gla",
    )(za.reshape(nb, s, ZA_W), wlr, blr, g)


def _hgrn_kernel(z_ref, lbp_ref, g_ref, o_ref, q_ref, lf_ref, kd_ref, accf_ref, accb_ref, stf_ref, stb_ref,
                 *, s, layer):
    rb = ROW_BLOCK
    x = lbp_ref[...]
    e = jnp.exp(x - jnp.max(x, axis=0, keepdims=True))
    p = e / jnp.sum(e, axis=0, keepdims=True)
    lb = p[0:1]
    for j in range(1, layer + 1):
        lb = lb + p[j:j + 1]
    lb = lb - p[0:1]

    def prep(st, i):
        qz = z_ref[pl.ds(st, rb), ZD['q']]
        q_ref[pl.ds(st, rb), :] = qz * jax.nn.sigmoid(qz)
        fz = z_ref[pl.ds(st, rb), ZD['f']]
        lf_ref[pl.ds(st, rb), :] = jnp.log(lb + (1.0 - lb) * jax.nn.sigmoid(fz))
        kd_ref[pl.ds(st, rb), :] = (1.0 - lb) * jax.nn.sigmoid(-fz)
    _rows(s, rb, prep)

    kw = HEADS * HGRN_DF

    def load(st, d):
        q = q_ref[pl.ds(st, HGRN_CHUNK), :]
        k = kd_ref[pl.ds(st, HGRN_CHUNK), d * kw:(d + 1) * kw]
        v = z_ref[pl.ds(st, HGRN_CHUNK), ZD['i']]
        g = lf_ref[pl.ds(st, HGRN_CHUNK), d * kw:(d + 1) * kw]
        return q, k, v, g
    _gl_bidir([functools.partial(load, d=d) for d in range(2)], (accf_ref, accb_ref), (stf_ref, stb_ref),
              s=s, c=HGRN_CHUNK, dk=HGRN_DF)

    mavg = _block_avg(BRANCH_W, HEAD_W)

    def out(st, i):
        y = _head_rmsnorm(accf_ref[pl.ds(st, rb), :] + accb_ref[pl.ds(st, rb), :], g_ref[...], mavg)
        gate = z_ref[pl.ds(st, rb), ZD['g']]
        o_ref[pl.ds(st, rb), :] = y * (gate * jax.nn.sigmoid(gate))
    _rows(s, rb, out)


def _hgrn(zd, lbp, g, nb, s, layer):
    kw = HEADS * HGRN_DF
    return pl.pallas_call(
        functools.partial(_hgrn_kernel, s=s, layer=layer),
        grid=(nb,),
        in_specs=[pl.BlockSpec((None, s, ZD_W), lambda b: (b, 0, 0)),
                  _const_spec((DEPTH, 2 * kw)), _const_spec((1, BRANCH_W))],
        out_specs=pl.BlockSpec((None, s, BRANCH_W), lambda b: (b, 0, 0)),
        out_shape=jax.ShapeDtypeStruct((nb, s, BRANCH_W), f32),
        scratch_shapes=[pltpu.VMEM((s, kw), f32), pltpu.VMEM((s, 2 * kw), f32), pltpu.VMEM((s, 2 * kw), f32)]
        + [pltpu.VMEM((s, BRANCH_W), f32)] * 2 + [pltpu.VMEM((BRANCH_W, kw), f32)] * 2,
        compiler_params=_cparams("parallel"),
        name="hgrn",
    )(zd.reshape(nb, s, ZD_W), lbp, g)


def _mlstm_bidir(z_ref, q_ref, k_ref, acc_refs, st_refs, nv_refs, m_refs, *, s):
    c, w = MLSTM_CHUNK, BRANCH_W
    n, un = s // c, MLSTM_UNROLL
    ones = jnp.ones((c, c), bf16)
    lane, row = _iota((c, w), 1), _iota((c, w), 0)
    head, sidx = lane // HEAD_W, lane % HEAD_W
    diag = sidx == row
    same = (_iota((w, w), 0) // HEAD_W) == (_iota((w, w), 1) // HEAD_W)
    bones = jnp.where(same, 1.0, 0.0).astype(bf16)
    tris = (_tri(c, False), _tri(c, True))
    causals = (sidx <= row, sidx >= row)
    for d in range(2):
        st_refs[d][...] = jnp.zeros((w, w), f32)
        nv_refs[d][...] = jnp.zeros((1, w), f32)
        m_refs[d][...] = jnp.zeros((1, w), f32)

    streams = [(d, u) for u in range(un) for d in range(2)]

    def seg_max(x):
        out = jnp.full((c, w), NEG_BIG, f32)
        for j in range(HEADS):
            mj = jnp.max(jnp.where(head == j, x, NEG_BIG), axis=-1, keepdims=True)
            out = jnp.where(head == j, mj, out)
        return out

    def body(ci, carry):
        idx = [(ci * un + u) if d == 0 else (n - 1 - ci * un - u) for d, u in streams]
        sts = [pl.multiple_of(i * c, c) for i in idx]
        qs = [q_ref[pl.ds(st, c), :] for st in sts]
        ks = [k_ref[pl.ds(st, c), :] for st in sts]
        vs = [z_ref[pl.ds(st, c), ZB['v']] for st in sts]
        igs = [z_ref[pl.ds(st, c), _sub(ZB['i'], d, w)] for (d, _), st in zip(streams, sts)]
        lfs = [jax.nn.log_sigmoid(z_ref[pl.ds(st, c), _sub(ZB['f'], d, w)])
               for (d, _), st in zip(streams, sts)]
        bs = [_lmul3(tris[d], lf) for (d, _), lf in zip(streams, lfs)]
        rs = [_lmul3(ones, jnp.where(diag, ig - b, 0.0)) for ig, b in zip(igs, bs)]
        log_ds = [jnp.where(causals[d], b + r, NEG_BIG) for (d, _), b, r in zip(streams, bs, rs)]
        mrows = [seg_max(x) for x in log_ds]
        lasts = [b[0:1] if d == 1 else b[c - 1:c] for (d, _), b in zip(streams, bs)]
        log_ws = [l - b + ig for l, b, ig in zip(lasts, bs, igs)]
        mws = [jnp.max(x, axis=0, keepdims=True) for x in log_ws]
        qbs = [q.astype(bf16) for q in qs]
        kexps = [jnp.concatenate([jnp.where(head == j, k, 0.0).astype(bf16) for j in range(HEADS)], axis=0)
                 for k in ks]
        vexps = [jnp.concatenate([jnp.where(head == j, v, 0.0).astype(bf16) for j in range(HEADS)], axis=0)
                 for v in vs]
        qks = [_mm_nt(qb, kexp) for qb, kexp in zip(qbs, kexps)]
        ms = [m_ref[...] for m_ref in m_refs]
        m_ts, w_prevs, w_ss, decs = [], [], [], []
        for i, (d, _) in enumerate(streams):
            log_prev = bs[i] + ms[d]
            m_t = jnp.maximum(mrows[i], log_prev)
            m_new = jnp.maximum(lasts[i] + ms[d], mws[i])
            m_ts.append(m_t)
            w_prevs.append(jnp.exp(log_prev - m_t))
            w_ss.append(jnp.exp(log_ws[i] - m_new))
            decs.append(jnp.exp(lasts[i] + ms[d] - m_new))
            ms[d] = m_new
        scs = [qk * jnp.exp(log_d - m_t) for qk, log_d, m_t in zip(qks, log_ds, m_ts)]
        nums = [_mm(sc.astype(bf16), vexp) for sc, vexp in zip(scs, vexps)]
        dens = [_rmul3(sc, bones) for sc in scs]
        upds = [jnp.where(same, _mm_tn((v * w_s).astype(bf16), k.astype(bf16)), 0.0)
                for v, w_s, k in zip(vs, w_ss, ks)]
        nvus = [jnp.sum(w_s * k, axis=0, keepdims=True) for w_s, k in zip(w_ss, ks)]
        states = [st_ref[...] for st_ref in st_refs]
        nvs = [nv_ref[...] for nv_ref in nv_refs]
        for i, (d, _) in enumerate(streams):
            num = w_prevs[i] * _mm_nt(qbs[i], states[d].astype(bf16)) + nums[i]
            den = w_prevs[i] * _rmul3(qs[i] * nvs[d], bones) + dens[i]
            acc_refs[d][pl.ds(sts[i], c), :] = num / jnp.maximum(jnp.abs(den), jnp.exp(-m_ts[i]))
            states[d] = decs[i] * states[d] + upds[i]
            nvs[d] = decs[i] * nvs[d] + nvus[i]
        for d in range(2):
            st_refs[d][...] = states[d]
            nv_refs[d][...] = nvs[d]
            m_refs[d][...] = ms[d]
        return carry

    lax.fori_loop(0, n // un, body, 0)


def _mlstm_kernel(z_ref, cw_ref, cb_ref, g_ref, o_ref, q_ref, k_ref, accf_ref, accb_ref, stf_ref, stb_ref,
                  nvf_ref, nvb_ref, mf_ref, mb_ref, *, s):
    rb = ROW_BLOCK
    nblk = s // rb
    w = BRANCH_W

    qk = slice(ZB['q'].start, ZB['k'].stop)

    def conv(st, i):
        x = z_ref[pl.ds(st, rb), qk]
        pst = pl.multiple_of(jnp.maximum(st - 8, 0), 8)
        nst = pl.multiple_of(jnp.minimum(st + rb, s - 8), 8)
        prev = z_ref[pl.ds(pst, 8), qk][7:8]
        nxt = z_ref[pl.ds(nst, 8), qk][0:1]
        prev = jnp.where(i == 0, 0.0, prev)
        nxt = jnp.where(i == nblk - 1, 0.0, nxt)
        row = _iota((rb, 2 * w), 0)
        xm = jnp.where(row == 0, prev, pltpu.roll(x, 1, 0))
        xp = jnp.where(row == rb - 1, nxt, pltpu.roll(x, rb - 1, 0))
        y = cb_ref[...] + xm * cw_ref[0:1, :]
        y = y + x * cw_ref[1:2, :]
        y = y + xp * cw_ref[2:3, :]
        y = y * jax.nn.sigmoid(y)
        q_ref[pl.ds(st, rb), :] = y[:, 0:w]
        k_ref[pl.ds(st, rb), :] = y[:, w:2 * w] * (HEAD_W ** -0.5)
    _rows(s, rb, conv)

    _mlstm_bidir(z_ref, q_ref, k_ref, (accf_ref, accb_ref), (stf_ref, stb_ref), (nvf_ref, nvb_ref),
                 (mf_ref, mb_ref), s=s)

    mavg = _block_avg(w, HEAD_W)

    def out(st, i):
        y = _head_rmsnorm(accf_ref[pl.ds(st, rb), :] + accb_ref[pl.ds(st, rb), :], g_ref[...], mavg)
        o_ref[pl.ds(st, rb), :] = jax.nn.sigmoid(z_ref[pl.ds(st, rb), ZB['o']]) * y
    _rows(s, rb, out)


def _mlstm(zb, cw, cb, g, nb, s):
    w = BRANCH_W
    return pl.pallas_call(
        functools.partial(_mlstm_kernel, s=s),
        grid=(nb,),
        in_specs=[pl.BlockSpec((None, s, ZB_W), lambda b: (b, 0, 0)),
                  _const_spec((3, 2 * w)), _const_spec((1, 2 * w)), _const_spec((1, w))],
        out_specs=pl.BlockSpec((None, s, w), lambda b: (b, 0, 0)),
        out_shape=jax.ShapeDtypeStruct((nb, s, w), f32),
        scratch_shapes=[pltpu.VMEM((s, w), f32)] * 4 + [pltpu.VMEM((w, w), f32)] * 2 + [pltpu.VMEM((1, w), f32)] * 4,
        compiler_params=_cparams("parallel"),
        name="mlstm",
    )(zb.reshape(nb, s, ZB_W), cw, cb, g)


def _rope(x, cos, sa, sb):
    w = x.shape[-1]
    half = HEAD_W // 2
    return x * cos + pltpu.roll(x, w - half, 1) * sa + pltpu.roll(x, half, 1) * sb


def _dil_bias(s):
    rel = jnp.arange(s, dtype=i32)[None, :] - jnp.arange(s, dtype=i32)[:, None]
    dist = jnp.abs(rel)
    cnt = jnp.zeros((s, s), f32)
    for window, dil in DIL_PATTERNS:
        ok = dist <= (window // (2 * dil)) * dil
        if dil > 1:
            ok = jnp.logical_and(ok, (rel & (dil - 1)) == 0)
        cnt = cnt + jnp.where(ok, 1.0, 0.0)
    return jnp.where(cnt > 0.0, jnp.log(jnp.maximum(cnt, 1.0)), NEG_BIG)


def _dil_kernel(z_ref, cos_ref, sa_ref, sb_ref, bias_ref, o_ref, kr_ref, vm_ref, *, s, tq):
    w = BRANCH_W
    qi = pl.program_id(1)
    rb = ROW_BLOCK

    @pl.when(qi == 0)
    def _():
        head = _iota((rb, w), 1) // HEAD_W

        def prep(st, i):
            x = z_ref[pl.ds(st, rb), ZC['k']]
            kr_ref[pl.ds(st, rb), :] = _rope(x, cos_ref[pl.ds(st, rb), :], sa_ref[pl.ds(st, rb), :],
                                             sb_ref[pl.ds(st, rb), :]).astype(bf16)
            v = z_ref[pl.ds(st, rb), ZC['v']]
            for j in range(HEADS):
                vm_ref[j, pl.ds(st, rb), :] = jnp.where(head == j, v, 0.0).astype(bf16)
        _rows(s, rb, prep)

    q0 = pl.multiple_of(qi * tq, tq)
    qr = _rope(z_ref[pl.ds(q0, tq), ZC['q']], cos_ref[pl.ds(q0, tq), :], sa_ref[pl.ds(q0, tq), :],
               sb_ref[pl.ds(q0, tq), :]) * (HEAD_W ** -0.5)
    qhead = _iota((tq, w), 1) // HEAD_W
    acc = jnp.zeros((tq, w), f32)
    for j in range(HEADS):
        qh = jnp.where(qhead == j, qr, 0.0).astype(bf16)
        sc = _mm_nt(qh, kr_ref[...]) + bias_ref[...]
        m = jnp.max(sc, axis=-1, keepdims=True)
        p = jnp.exp(sc - m)
        l = jnp.sum(p, axis=-1, keepdims=True)
        acc = acc + _mm(p.astype(bf16), vm_ref[j]) / l
    o_ref[...] = acc


def _dil(zc, cos, sa, sb, bias, nb, s, tq=512):
    w = BRANCH_W
    for _, dil in DIL_PATTERNS:
        assert dil & (dil - 1) == 0 and s % dil == 0
    return pl.pallas_call(
        functools.partial(_dil_kernel, s=s, tq=tq),
        grid=(nb, s // tq),
        in_specs=[pl.BlockSpec((None, s, ZC_W), lambda b, i: (b, 0, 0)),
                  _const_spec((s, w)), _const_spec((s, w)), _const_spec((s, w)),
                  pl.BlockSpec((tq, s), lambda b, i: (i, 0))],
        out_specs=pl.BlockSpec((None, tq, w), lambda b, i: (b, i, 0)),
        out_shape=jax.ShapeDtypeStruct((nb, s, w), f32),
        scratch_shapes=[pltpu.VMEM((s, w), bf16), pltpu.VMEM((HEADS, s, w), bf16)],
        compiler_params=_cparams("parallel", "arbitrary"),
        name="dilattn",
    )(zc.reshape(nb, s, ZC_W), cos, sa, sb, bias)


def _merge_kernel(h_ref, ya, yb, yc, yd, g_ref, wg_ref, wbr_ref, wo_ref, o_ref):
    x = h_ref[...]
    u = _rms(x, g_ref[...]).astype(bf16)
    merged = None
    for n, y in enumerate((ya, yb, yc, yd)):
        gate = jax.nn.sigmoid(_mm(u, wg_ref[n]))
        term = gate * _mm(y[...].astype(bf16), wbr_ref[n])
        merged = term if merged is None else merged + term
    o_ref[...] = x + _mm(merged.astype(bf16), wo_ref[...])


def _merge(h, ys, g, wg, wbr, wo, tm=512):
    t = h.shape[0]
    tok = lambda w: pl.BlockSpec((tm, w), lambda i: (i, 0))
    return pl.pallas_call(
        _merge_kernel,
        grid=(t // tm,),
        in_specs=[tok(D_MODEL)] + [tok(BRANCH_W)] * 4
        + [_const_spec((1, D_MODEL)), _const_spec((4, D_MODEL, D_MODEL)),
           _const_spec((4, BRANCH_W, D_MODEL)), _const_spec((D_MODEL, D_MODEL))],
        out_specs=tok(D_MODEL),
        out_shape=jax.ShapeDtypeStruct((t, D_MODEL), f32),
        compiler_params=_cparams("parallel"),
        name="merge",
    )(h, *ys, g, wg, wbr, wo)


def _stair_cells():
    return [(a, b) for a in range(PEER_TOPK) for b in range(PEER_TOPK) if (a + 1) * (b + 1) <= PEER_TOPK]


_CELLS = _stair_cells()
_NCELL = len(_CELLS)
_NCELL_PAD = -(-_NCELL // 8) * 8


def _extract_top(works, rowf, nrows):
    works = list(works)
    tops = [[] for _ in works]
    ranks = [jnp.full(w.shape, float(PEER_TOPK), f32) for w in works]
    for r in range(PEER_TOPK):
        mxs = [jnp.max(w, axis=0, keepdims=True) for w in works]
        cands = [jnp.where(w == mx, rowf, float(nrows)) for w, mx in zip(works, mxs)]
        hits = [cand == jnp.min(cand, axis=0, keepdims=True) for cand in cands]
        works = [jnp.where(hit, -jnp.inf, w) for hit, w in zip(hits, works)]
        ranks = [jnp.where(hit, float(r), rank) for hit, rank in zip(hits, ranks)]
        for t, mx in zip(tops, mxs):
            t.append(mx)
    return tops, ranks


def _distinct_maxima(xs):
    tops = [[] for _ in xs]
    mxs = [None] * len(xs)
    for _ in range(PEER_TOPK):
        curs = [x if mx is None else jnp.where(x < mx, x, -jnp.inf) for x, mx in zip(xs, mxs)]
        mxs = [jnp.max(cur, axis=0, keepdims=True) for cur in curs]
        for t, mx in zip(tops, mxs):
            t.append(mx)
    return tops


def _count(mask):
    return jnp.sum(jnp.where(mask, 1.0, 0.0), axis=0, keepdims=True)


def _route_kernel(h_ref, g_ref, wpq_ref, keys_ref, sa_ref, ut_ref, a_ref, cnt_ref, rk_ref, bn_ref,
                  q_ref, *m_refs, tm):
    uf = _rms(h_ref[...], g_ref[...])
    ut_ref[...] = uf.T.astype(bf16)
    q_ref[...] = _mm(uf.astype(bf16), wpq_ref[...]).astype(bf16)
    nk = PEER_NKEYS
    topk = float(PEER_TOPK)

    def scores(hd, toks):
        return [_mm_nt(keys_ref[hd * 2 + p], q_ref[toks, pl.ds(pl.multiple_of((hd * 2 + p) * nk, nk), nk)])
                for p in range(2)]

    def write_tables(hd, amat, cnt, rank2, bnorm):
        rows = pl.ds(pl.multiple_of(hd * nk, nk), nk)
        prow = pl.ds(pl.multiple_of(hd * (nk // 2), nk // 2), nk // 2)
        rkp = pltpu.bitcast(rank2.astype(bf16), jnp.uint32)
        bnp = pltpu.bitcast(bnorm.astype(bf16), jnp.uint32)
        for tb in range(tm // LANES):
            lanes = slice(tb * LANES, (tb + 1) * LANES)
            a_ref[tb, rows, :] = amat[:, lanes]
            cnt_ref[tb, rows, :] = cnt[:, lanes]
            rk_ref[tb, prow, :] = rkp[:, lanes]
            bn_ref[tb, prow, :] = bnp[:, lanes]

    def stair_sums(tops, m_ref):
        m_ref[...] = jnp.full((_NCELL_PAD, tm), -jnp.inf, f32)
        for ci, (a, b) in enumerate(_CELLS):
            m_ref[ci:ci + 1, :] = tops[0][a] + tops[1][b]
        return m_ref[...]

    def fast_tiles(hds):
        scs = [scores(hd, slice(0, tm)) for hd in hds]
        flat = _distinct_maxima([s for pair in scs for s in pair])
        tops = [flat[2 * i:2 * i + 2] for i in range(len(hds))]
        rank2s = [jnp.zeros((nk, tm), f32) for _ in hds]
        for r in range(PEER_TOPK):
            rank2s = [rank2 + jnp.where(sc[1] < top[1][r], 1.0, 0.0) for rank2, sc, top in zip(rank2s, scs, tops)]
        tied = jnp.zeros((1, tm), f32)
        for sc, top, rank2 in zip(scs, tops, rank2s):
            tied = jnp.where(_count(sc[0] >= top[0][-1]) == topk, tied, 1.0)
            tied = jnp.where(_count(rank2 < topk) == topk, tied, 1.0)
        cands = [stair_sums(top, m_ref) for top, m_ref in zip(tops, m_refs)]
        ctops = _distinct_maxima(cands)
        for hd, sc, top, rank2, cand, ctop in zip(hds, scs, tops, rank2s, cands, ctops):
            chosen = cand >= ctop[-1]
            tied = jnp.where(_count(chosen) == topk, tied, 1.0)
            zsum = jnp.sum(jnp.where(chosen, jnp.exp(cand - cand[0:1]), 0.0), axis=0, keepdims=True)
            rowcnt = _mm(sa_ref[...], jnp.where(chosen, 1.0, 0.0).astype(bf16))
            cnt = jnp.zeros((nk, tm), f32)
            for a in range(PEER_TOPK):
                cnt = jnp.where(sc[0] == top[0][a], rowcnt[a:a + 1], cnt)
            write_tables(hd, jnp.exp(sc[0] - top[0][0]), cnt, rank2, jnp.exp(sc[1] - top[1][0]) / zsum)
        return tied

    def exact_tile(hd, m_ref):
        rowf = _iota((nk, tm), 0).astype(f32)
        rowcf = _iota((_NCELL_PAD, tm), 0).astype(f32)
        scs = scores(hd, slice(0, tm))
        tops, ranks = _extract_top(scs, rowf, nk)
        cand = stair_sums(tops, m_ref)
        _, (crank,) = _extract_top([cand], rowcf, _NCELL_PAD)
        chosen = crank < topk
        zsum = jnp.sum(jnp.where(chosen, jnp.exp(cand - cand[0:1]), 0.0), axis=0, keepdims=True)
        rowcnt = _mm(sa_ref[...], jnp.where(chosen, 1.0, 0.0).astype(bf16))
        cnt = jnp.zeros((nk, tm), f32)
        for a in range(PEER_TOPK):
            cnt = jnp.where(ranks[0] == float(a), rowcnt[a:a + 1], cnt)
        write_tables(hd, jnp.exp(scs[0] - tops[0][0]), cnt, ranks[1], jnp.exp(scs[1] - tops[1][0]) / zsum)

    def per_group(hg, carry):
        hds = [hg * ROUTE_HEADS_TOGETHER + j for j in range(ROUTE_HEADS_TOGETHER)]
        tied = fast_tiles(hds)

        @pl.when(jnp.max(tied) > 0.0)
        def _():
            for hd, m_ref in zip(hds, m_refs):
                exact_tile(hd, m_ref)
        return carry

    lax.fori_loop(0, PEER_HEADS // ROUTE_HEADS_TOGETHER, per_group, 0)


def _route(h, g, wpq, keys, sa, tm=256):
    t = h.shape[0]
    nh, nk = PEER_HEADS, PEER_NKEYS
    tab = pl.BlockSpec((tm // LANES, nh * nk, LANES), lambda i: (i, 0, 0))
    ptab = pl.BlockSpec((tm // LANES, nh * nk // 2, LANES), lambda i: (i, 0, 0))
    return pl.pallas_call(
        functools.partial(_route_kernel, tm=tm),
        grid=(t // tm,),
        in_specs=[pl.BlockSpec((tm, D_MODEL), lambda i: (i, 0)), _const_spec((1, D_MODEL)),
                  _const_spec((D_MODEL, 2 * nh * nk)), _const_spec((2 * nh, nk, nk)),
                  _const_spec((PEER_TOPK, _NCELL_PAD))],
        out_specs=[pl.BlockSpec((D_MODEL, tm), lambda i: (0, i)), tab, tab, ptab, ptab],
        out_shape=[jax.ShapeDtypeStruct((D_MODEL, t), bf16)] + [jax.ShapeDtypeStruct((t // LANES, nh * nk, LANES), f32)] * 2
        + [jax.ShapeDtypeStruct((t // LANES, nh * nk // 2, LANES), jnp.uint32)] * 2,
        scratch_shapes=[pltpu.VMEM((tm, 2 * nh * nk), bf16)]
        + [pltpu.VMEM((_NCELL_PAD, tm), f32)] * ROUTE_HEADS_TOGETHER,
        compiler_params=_cparams("parallel"),
        name="route",
    )(h, g, wpq, keys, sa)


def _gelu_bf16(x):
    return (0.5 * x * (1.0 + lax.erf(x * math.sqrt(0.5)))).astype(bf16)


def _peer_gate_block(act_ref, p_ref, tabs, i0, ii, tb):
    a_ref, cnt_ref, rk_ref, bn_ref = tabs
    nk = PEER_NKEYS
    lanes = slice(tb * LANES, (tb + 1) * LANES)
    gate = None
    for hd in range(PEER_HEADS):
        row = pl.ds(hd * nk + i0 + ii, 1)
        a = jnp.broadcast_to(a_ref[tb, row, :], (nk, LANES)).astype(bf16)
        c = jnp.broadcast_to(cnt_ref[tb, row, :], (nk, LANES)).astype(bf16)
        prow = slice(hd * (nk // 2), (hd + 1) * (nk // 2))
        rk = pltpu.bitcast(rk_ref[tb, prow, :], bf16)
        bn = pltpu.bitcast(bn_ref[tb, prow, :], bf16)
        term = a * jnp.where(rk < c, bn, jnp.zeros((), bf16))
        gate = term if gate is None else gate + term
    p_ref[ii * nk:(ii + 1) * nk, lanes] = gate * act_ref[ii * nk:(ii + 1) * nk, lanes]


def _mxu_product(rhs_tile, lhs_tile, between):
    nt = PEER_SUB // MXU_TILE
    for q in range(2):
        pltpu.matmul_push_rhs(rhs_tile(0, q), staging_register=0, mxu_index=q)
    for kt in range(nt):
        reg = kt % 2
        for mc in range(nt):
            lhs = lhs_tile(mc, kt)
            for q in range(2):
                pltpu.matmul_acc_lhs(mc * (MXU_TILE // 4), lhs, mxu_index=q,
                                     load_staged_rhs=(reg if mc == 0 else None))
            if mc == 0 and kt + 1 < nt:
                for q in range(2):
                    pltpu.matmul_push_rhs(rhs_tile(kt + 1, q), staging_register=1 - reg, mxu_index=q)
            between()


def _peer_kernel(ut_ref, a_ref, cnt_ref, rk_ref, bn_ref, pu_ref, pvt_ref, o_ref, acc_ref, act0, act1, p0, p1,
                 *, ne, tp):
    g = pl.program_id(0)
    nblocks = pl.num_programs(0) - 2
    b1 = jnp.clip(g - 1, 0, nblocks - 1)
    b2 = jnp.clip(g - 2, 0, nblocks - 1)
    i1 = (b1 % ne) * (PEER_SUB // PEER_NKEYS)
    tabs = (a_ref, cnt_ref, rk_ref, bn_ref)

    @pl.when(g == 0)
    def _():
        for ref in (act0, act1, p0, p1):
            ref[...] = jnp.zeros(ref.shape, bf16)

    @pl.when(b2 % ne == 0)
    def _():
        acc_ref[...] = jnp.zeros(acc_ref.shape, f32)

    t256 = lambda i: slice(i * MXU_TILE, (i + 1) * MXU_TILE)

    def stage(act_w, act_r, p_w, p_r):
        blocks = iter([(ii, tb) for ii in range(PEER_SUB // PEER_NKEYS) for tb in range(tp // LANES)])

        def one_gate_block():
            ii, tb = next(blocks)
            _peer_gate_block(act_r, p_w, tabs, i1, ii, tb)

        nt = PEER_SUB // MXU_TILE
        _mxu_product(lambda kt, q: ut_ref[t256(kt), t256(q)], lambda mc, kt: pu_ref[t256(mc), t256(kt)],
                     one_gate_block)
        for mc in range(nt):
            for q in range(2):
                x = pltpu.matmul_pop(mc * (MXU_TILE // 4), (MXU_TILE, MXU_TILE), f32, q)
                act_w[t256(mc), t256(q)] = _gelu_bf16(x)
        _mxu_product(lambda kt, q: p_r[t256(kt), t256(q)], lambda mc, kt: pvt_ref[t256(mc), t256(kt)],
                     one_gate_block)
        for mc in range(nt):
            for q in range(2):
                acc_ref[t256(mc), t256(q)] += pltpu.matmul_pop(mc * (MXU_TILE // 4), (MXU_TILE, MXU_TILE), f32, q)

    @pl.when(g % 2 == 0)
    def _():
        stage(act0, act1, p1, p0)

    @pl.when(g % 2 == 1)
    def _():
        stage(act1, act0, p0, p1)

    @pl.when(jnp.logical_and(g >= 2, b2 % ne == ne - 1))
    def _():
        o_ref[...] = acc_ref[...].T


def _peer(ut, tabs, pu, pvt):
    t = ut.shape[1]
    tp = 2 * MXU_TILE
    nh, nk = PEER_HEADS, PEER_NKEYS
    assert D_MODEL == PEER_SUB and (PEER_SUB // nk) * (tp // LANES) == 2 * (PEER_SUB // MXU_TILE) ** 2
    ne = pu.shape[0] // PEER_SUB
    nblocks = (t // tp) * ne
    blk = lambda lag: (lambda g: jnp.clip(g - lag, 0, nblocks - 1))
    b0, b1, b2 = blk(0), blk(1), blk(2)
    tab = pl.BlockSpec((tp // LANES, nh * nk, LANES), lambda g: (b1(g) // ne, 0, 0))
    ptab = pl.BlockSpec((tp // LANES, nh * nk // 2, LANES), lambda g: (b1(g) // ne, 0, 0))
    return pl.pallas_call(
        functools.partial(_peer_kernel, ne=ne, tp=tp),
        grid=(nblocks + 2,),
        in_specs=[pl.BlockSpec((D_MODEL, tp), lambda g: (0, b0(g) // ne)), tab, tab, ptab, ptab,
                  pl.BlockSpec((PEER_SUB, D_MODEL), lambda g: (b0(g) % ne, 0)),
                  pl.BlockSpec((D_MODEL, PEER_SUB), lambda g: (0, b2(g) % ne))],
        out_specs=pl.BlockSpec((tp, D_MODEL), lambda g: (b2(g) // ne, 0)),
        out_shape=jax.ShapeDtypeStruct((t, D_MODEL), f32),
        scratch_shapes=[pltpu.VMEM((D_MODEL, tp), f32)] + [pltpu.VMEM((PEER_SUB, tp), bf16)] * 4,
        compiler_params=_cparams("arbitrary"),
        name="peer",
    )(ut, *tabs, pu, pvt)


def _pe_kernel(h_ref, po_ref, p_ref, g_ref, wpg_ref, wpe_ref, gf_ref, o_ref, *, final):
    h = h_ref[...] + po_ref[...]
    gate = jax.nn.sigmoid(_mm(_rms(h, g_ref[...]).astype(bf16), wpg_ref[...]))
    h = h + gate * _mm(p_ref[...].astype(bf16), wpe_ref[...])
    if final:
        h = _rms(h, gf_ref[...])
    o_ref[...] = h


def _pe(h, po, p, g, wpg, wpe, gf, final, tm=512):
    t = h.shape[0]
    tok = lambda w: pl.BlockSpec((tm, w), lambda i: (i, 0))
    return pl.pallas_call(
        functools.partial(_pe_kernel, final=final),
        grid=(t // tm,),
        in_specs=[tok(D_MODEL), tok(D_MODEL), tok(PE_DIM), _const_spec((1, D_MODEL)),
                  _const_spec((D_MODEL, D_MODEL)), _const_spec((PE_DIM, D_MODEL)), _const_spec((1, D_MODEL))],
        out_specs=tok(D_MODEL),
        out_shape=jax.ShapeDtypeStruct((t, D_MODEL), f32),
        compiler_params=_cparams("parallel"),
        name="pe_embed",
    )(h, po, p, g, wpg, wpe, gf)


def _cols(w, name):
    a, b = _PROJ_OFF[name]
    return w[..., a:b]


def _proj_groups(w):
    lr_real = _PROJ_OFF['a_lr'][1] - _PROJ_OFF['a_lr'][0]
    pad = jnp.zeros(w.shape[:-1] + (ZA['lr'].stop - ZA['lr'].start - lr_real,), w.dtype)
    ga = jnp.concatenate([_cols(w, n) for n in ('a_q', 'a_k', 'a_v', 'a_g', 'a_lr')] + [pad], axis=-1)
    gates = [jnp.repeat(_cols(w, n), HEAD_W, axis=-1) for n in ('b_i', 'b_f')]
    gb = jnp.concatenate([_cols(w, n) for n in ('b_q', 'b_k', 'b_v', 'b_o')] + gates, axis=-1)
    gc = jnp.concatenate([_cols(w, n) for n in ('c_q', 'c_k', 'c_v')], axis=-1)
    gd = jnp.concatenate([_cols(w, n) for n in ('d_q', 'd_f', 'd_i', 'd_g')], axis=-1)
    return ga, gb, gc, gd


def _rope_tables(s):
    half = HEAD_W // 2
    inv = ROPE_THETA ** (-jnp.arange(half, dtype=f32) / half)
    ang = jnp.arange(s, dtype=f32)[:, None] * inv[None, :]
    cos, sin = jnp.cos(ang), jnp.sin(ang)
    zero = jnp.zeros_like(sin)
    tile = lambda a, b: jnp.tile(jnp.concatenate([a, b], axis=-1), (1, HEADS))
    return tile(cos, cos), tile(-sin, zero), tile(zero, sin)


def _row_count_matrix():
    m = np.zeros((PEER_TOPK, _NCELL_PAD), np.float32)
    for ci, (a, _) in enumerate(_CELLS):
        m[a, ci] = 1.0
    return jnp.asarray(m, bf16)


def _prepare(g_mix, w_in, b_in, w_lr2, b_lr2, conv_w, conv_b, lb_param, g_a, g_b, g_d, w_gate, w_br, w_out,
             g_ffn, w_pq, sub_keys, peer_u, peer_v, g_pe, w_pe, w_pg, g_final):
    layers = []
    row = lambda a: a.reshape(1, -1)
    for l in range(DEPTH):
        wlr = jnp.zeros((LANES, 256), f32)
        for n in range(2):
            wlr = wlr.at[n * GLA_RANK:(n + 1) * GLA_RANK, n * 128:(n + 1) * 128].set(w_lr2[l, n])
        layers.append(dict(
            g_mix=row(g_mix[l]),
            w_proj=[w.astype(bf16) for w in _proj_groups(w_in[l])],
            b_proj=list(_proj_groups(row(b_in[l]))),
            wlr=wlr.astype(bf16), blr=row(b_lr2[l]),
            conv_w=conv_w[l], conv_b=row(conv_b[l]),
            g_a=row(g_a[l]), g_b=row(g_b[l]), g_d=row(g_d[l]),
            w_gate=w_gate[l].astype(bf16), w_br=w_br[l].astype(bf16), w_out=w_out[l].astype(bf16),
            g_ffn=row(g_ffn[l]), w_pq=w_pq[l].astype(bf16),
            keys=sub_keys[l].reshape(2 * PEER_HEADS, PEER_NKEYS, -1).astype(bf16),
            peer_u=peer_u[l].astype(bf16), peer_vt=peer_v[l].astype(bf16).T,
            g_pe=row(g_pe[l]), w_pe=w_pe[l].astype(bf16), w_pg=w_pg[l].astype(bf16),
        ))
    shared = dict(
        lbp=jnp.transpose(lb_param, (1, 0, 2)).reshape(DEPTH, -1),
        g_final=row(g_final), sa=_row_count_matrix(),
    )
    return layers, shared


def _trunk(x, p, layers, shared):
    nb, s, d = x.shape
    t = nb * s
    h = x.reshape(t, d)
    cos, sa, sb = _rope_tables(s)
    dil_bias = _dil_bias(s)
    for l, w in enumerate(layers):
        za, zb, zc, zd = _proj(h, w['g_mix'], w['w_proj'], w['b_proj'])
        ya = _gla(za, w['wlr'], w['blr'], w['g_a'], nb, s)
        yb = _mlstm(zb, w['conv_w'], w['conv_b'], w['g_b'], nb, s)
        yc = _dil(zc, cos, sa, sb, dil_bias, nb, s)
        yd = _hgrn(zd, shared['lbp'], w['g_d'], nb, s, l)
        ys = [y.reshape(t, BRANCH_W) for y in (ya, yb, yc, yd)]
        h1 = _merge(h, ys, w['g_mix'], w['w_gate'], w['w_br'], w['w_out'])
        ut, *tabs = _route(h1, w['g_ffn'], w['w_pq'], w['keys'], shared['sa'])
        po = _peer(ut, tabs, w['peer_u'], w['peer_vt'])
        h = _pe(h1, po, p[l].reshape(t, PE_DIM), w['g_pe'], w['w_pg'], w['w_pe'], shared['g_final'],
                final=(l == DEPTH - 1))
    return h.reshape(nb, s, d)


def kernel(x_prompt, x_sample, p_prompt, p_sample, g_mix, w_in, b_in, w_lr2, b_lr2, conv_w, conv_b, lb_param, g_a, g_b, g_d, w_gate, w_br, w_out, g_ffn, w_pq, sub_keys, peer_u, peer_v, g_pe, w_pe, w_pg, g_final):
    layers, shared = _prepare(g_mix, w_in, b_in, w_lr2, b_lr2, conv_w, conv_b, lb_param, g_a, g_b, g_d, w_gate,
                              w_br, w_out, g_ffn, w_pq, sub_keys, peer_u, peer_v, g_pe, w_pe, w_pg, g_final)
    return (_trunk(x_prompt, p_prompt, layers, shared), _trunk(x_sample, p_sample, layers, shared))
```

```python
import functools
import math

import numpy as np
import jax
import jax.numpy as jnp
from jax import lax
from jax.experimental import pallas as pl
from jax.experimental.pallas import tpu as pltpu

f32 = jnp.float32
bf16 = jnp.bfloat16
i32 = jnp.int32

D_MODEL = 1024
DEPTH = 2
PE_DIM = 256
EPS = 1e-6
NEG_BIG = -1e30
HEADS = 4
HEAD_W = 64
BRANCH_W = HEADS * HEAD_W
GLA_DK = 32
GLA_RANK = 16
GLA_NORMALIZER = 16.0
GLA_CHUNK = 32
MLSTM_CHUNK = 64
HGRN_DF = 64
HGRN_CHUNK = 32
DIL_PATTERNS = ((128, 1), (512, 4), (2048, 16))
ROPE_THETA = 10000.0
PEER_HEADS = 8
PEER_NKEYS = 128
PEER_TOPK = 16
LANES = 128
PEER_SUB = 1024
MXU_TILE = 256
ROUTE_HEADS_TOGETHER = 2
EXP_CLAMP = 80.0
GL_UNROLL = 8
MLSTM_UNROLL = 2

_PROJ_LAYOUT = (
    ('a_q', 128), ('a_k', 128), ('a_v', 256), ('a_g', 256), ('a_lr', 32),
    ('b_q', 256), ('b_k', 256), ('b_v', 256), ('b_i', 8), ('b_f', 8), ('b_o', 256),
    ('c_q', 256), ('c_k', 256), ('c_v', 256),
    ('d_q', 256), ('d_f', 512), ('d_i', 256), ('d_g', 256),
)
_PROJ_OFF = {}
_o = 0
for _n, _s in _PROJ_LAYOUT:
    _PROJ_OFF[_n] = (_o, _o + _s)
    _o += _s


def _lane_layout(*fields):
    out, start = {}, 0
    for name, width in fields:
        assert width % 128 == 0
        out[name] = slice(start, start + width)
        start += width
    return out, start


def _sub(field, index, width):
    return slice(field.start + index * width, field.start + (index + 1) * width)


ZA, ZA_W = _lane_layout(('q', 128), ('k', 128), ('v', 256), ('g', 256), ('lr', 128))
ZB, ZB_W = _lane_layout(('q', 256), ('k', 256), ('v', 256), ('o', 256), ('i', 512), ('f', 512))
ZC, ZC_W = _lane_layout(('q', 256), ('k', 256), ('v', 256))
ZD, ZD_W = _lane_layout(('q', 256), ('f', 512), ('i', 256), ('g', 256))
ROW_BLOCK = 256

VMEM_LIMIT = 56 * 1024 * 1024


def _cparams(*sem):
    return pltpu.CompilerParams(dimension_semantics=sem, vmem_limit_bytes=VMEM_LIMIT)


def _const_spec(shape):
    nd = len(shape)
    return pl.BlockSpec(shape, lambda *_: (0,) * nd, pipeline_mode=pl.Buffered(1))


def _mm(a, b):
    return jnp.dot(a, b, preferred_element_type=f32)


def _mm_nt(a, b):
    return lax.dot_general(a, b, (((1,), (1,)), ((), ())), preferred_element_type=f32)


def _mm_tn(a, b):
    return lax.dot_general(a, b, (((0,), (0,)), ((), ())), preferred_element_type=f32)


def _split3(x):
    hi = x.astype(bf16)
    r1 = x - hi.astype(f32)
    mid = r1.astype(bf16)
    lo = (r1 - mid.astype(f32)).astype(bf16)
    return hi, mid, lo


def _lmul3(m, x):
    w = x.shape[1]
    y = _mm(m, jnp.concatenate(_split3(x), axis=1))
    return y[:, 0:w] + y[:, w:2 * w] + y[:, 2 * w:3 * w]


def _rmul3(x, m):
    r = x.shape[0]
    y = _mm(jnp.concatenate(_split3(x), axis=0), m)
    return y[0:r] + y[r:2 * r] + y[2 * r:3 * r]


def _rms(x, g):
    return x * lax.rsqrt(jnp.mean(x * x, axis=-1, keepdims=True) + EPS) * g


def _iota(shape, dim):
    return lax.broadcasted_iota(i32, shape, dim)


def _tri(c, reverse):
    r, k = _iota((c, c), 0), _iota((c, c), 1)
    return jnp.where((k >= r) if reverse else (k <= r), 1.0, 0.0).astype(bf16)


def _block_avg(w, blk):
    same = (_iota((w, w), 0) // blk) == (_iota((w, w), 1) // blk)
    return jnp.where(same, 1.0 / blk, 0.0).astype(bf16)


def _rows(n, blk, fn):
    def body(i, c):
        fn(pl.multiple_of(i * blk, blk), i)
        return c
    lax.fori_loop(0, n // blk, body, 0)


def _head_rmsnorm(o, g, mavg):
    ms = _rmul3(o * o, mavg)
    return o * lax.rsqrt(ms + EPS) * g


def _proj_kernel(h_ref, g_ref, wa, wb, wc, wd, ba, bb, bc, bd, za, zb, zc, zd):
    u = _rms(h_ref[...], g_ref[...]).astype(bf16)
    for w, b, z in ((wa, ba, za), (wb, bb, zb), (wc, bc, zc), (wd, bd, zd)):
        z[...] = _mm(u, w[...]) + b[...]


def _proj(h, g, ws, bs, tm=512):
    t = h.shape[0]
    widths = (ZA_W, ZB_W, ZC_W, ZD_W)
    tok = lambda w: pl.BlockSpec((tm, w), lambda i: (i, 0))
    return pl.pallas_call(
        _proj_kernel,
        grid=(t // tm,),
        in_specs=[tok(D_MODEL), _const_spec((1, D_MODEL))]
        + [_const_spec((D_MODEL, w)) for w in widths] + [_const_spec((1, w)) for w in widths],
        out_specs=[tok(w) for w in widths],
        out_shape=[jax.ShapeDtypeStruct((t, w), f32) for w in widths],
        compiler_params=_cparams("parallel"),
        name="proj",
    )(h, g, *ws, *bs)


def _gl_bidir(loads, acc_refs, st_refs, *, s, c, dk):
    h, dv = HEADS, HEAD_W
    kw, vw, hc = h * dk, h * dv, h * c
    n, un = s // c, GL_UNROLL
    khead = _iota((c, kw), 1) // dk
    vhead = _iota((c, vw), 1) // dv
    sidx, trow = _iota((c, hc), 1) % c, _iota((c, hc), 0)
    bd = (_iota((vw, kw), 0) // dv) == (_iota((vw, kw), 1) // dk)
    tris = (_tri(c, False), _tri(c, True))
    causals = (sidx <= trow, sidx >= trow)
    for st_ref in st_refs:
        st_ref[...] = jnp.zeros((vw, kw), f32)

    streams = [(d, u) for u in range(un) for d in range(2)]

    def body(ci, carry):
        idx = [(ci * un + u) if d == 0 else (n - 1 - ci * un - u) for d, u in streams]
        sts = [pl.multiple_of(i * c, c) for i in idx]
        ins = [loads[d](st) for (d, _), st in zip(streams, sts)]
        qs, ks, vs = [x[0] for x in ins], [x[1] for x in ins], [x[2] for x in ins]
        bs = [_lmul3(tris[d], x[3]) for (d, _), x in zip(streams, ins)]
        lasts = [b[0:1] if d == 1 else b[c - 1:c] for (d, _), b in zip(streams, bs)]
        mids = [b[c // 2:c // 2 + 1] for b in bs]
        qts = [(q * jnp.exp(jnp.minimum(b - m, EXP_CLAMP))).astype(bf16) for q, b, m in zip(qs, bs, mids)]
        kts = [k * jnp.exp(jnp.minimum(m - b, EXP_CLAMP)) for k, b, m in zip(ks, bs, mids)]
        kexps = [jnp.concatenate([jnp.where(khead == j, kt, 0.0).astype(bf16) for j in range(h)], axis=0)
                 for kt in kts]
        vexps = [jnp.concatenate([jnp.where(vhead == j, v, 0.0).astype(bf16) for j in range(h)], axis=0)
                 for v in vs]
        atts = [jnp.where(causals[d], _mm_nt(qt, kexp), 0.0).astype(bf16)
                for (d, _), qt, kexp in zip(streams, qts, kexps)]
        os_ = [_mm(att, vexp) for att, vexp in zip(atts, vexps)]
        qis = [(q * jnp.exp(b)).astype(bf16) for q, b in zip(qs, bs)]
        kds = [(k * jnp.exp(l - b)).astype(bf16) for k, l, b in zip(ks, lasts, bs)]
        upds = [jnp.where(bd, _mm_tn(v.astype(bf16), kd), 0.0) for v, kd in zip(vs, kds)]
        decs = [jnp.exp(l) for l in lasts]
        states = [st_ref[...] for st_ref in st_refs]
        for i, (d, _) in enumerate(streams):
            acc_refs[d][pl.ds(sts[i], c), :] = os_[i] + _mm_nt(qis[i], states[d].astype(bf16))
            states[d] = states[d] * decs[i] + upds[i]
        for d in range(2):
            st_refs[d][...] = states[d]
        return carry

    lax.fori_loop(0, n // un, body, 0)


def _gla_kernel(z_ref, wlr_ref, blr_ref, g_ref, o_ref, gk_ref, accf_ref, accb_ref, stf_ref, stb_ref, *, s):
    rb = ROW_BLOCK

    def gates(st, i):
        lr = z_ref[pl.ds(st, rb), ZA['lr']].astype(bf16)
        x = _mm(lr, wlr_ref[...]) + blr_ref[...]
        gk_ref[pl.ds(st, rb), :] = jax.nn.log_sigmoid(x) * (1.0 / GLA_NORMALIZER)
    _rows(s, rb, gates)

    scale = GLA_DK ** -0.5

    def load(st, d):
        q = z_ref[pl.ds(st, GLA_CHUNK), ZA['q']] * scale
        k = z_ref[pl.ds(st, GLA_CHUNK), ZA['k']]
        v = z_ref[pl.ds(st, GLA_CHUNK), ZA['v']]
        g = gk_ref[pl.ds(st, GLA_CHUNK), d * 128:(d + 1) * 128]
        return q, k, v, g
    _gl_bidir([functools.partial(load, d=d) for d in range(2)], (accf_ref, accb_ref), (stf_ref, stb_ref),
              s=s, c=GLA_CHUNK, dk=GLA_DK)

    mavg = _block_avg(BRANCH_W, HEAD_W)

    def out(st, i):
        y = _head_rmsnorm(accf_ref[pl.ds(st, rb), :] + accb_ref[pl.ds(st, rb), :], g_ref[...], mavg)
        gate = z_ref[pl.ds(st, rb), ZA['g']]
        o_ref[pl.ds(st, rb), :] = y * (gate * jax.nn.sigmoid(gate))
    _rows(s, rb, out)


def _gla(za, wlr, blr, g, nb, s):
    return pl.pallas_call(
        functools.partial(_gla_kernel, s=s),
        grid=(nb,),
        in_specs=[pl.BlockSpec((None, s, ZA_W), lambda b: (b, 0, 0)),
                  _const_spec((LANES, 256)), _const_spec((1, 256)), _const_spec((1, BRANCH_W))],
        out_specs=pl.BlockSpec((None, s, BRANCH_W), lambda b: (b, 0, 0)),
        out_shape=jax.ShapeDtypeStruct((nb, s, BRANCH_W), f32),
        scratch_shapes=[pltpu.VMEM((s, 256), f32)] + [pltpu.VMEM((s, BRANCH_W), f32)] * 2
        + [pltpu.VMEM((BRANCH_W, HEADS * GLA_DK), f32)] * 2,
        compiler_params=_cparams("parallel"),
        name="gla",
    )(za.reshape(nb, s, ZA_W), wlr, blr, g)


def _hgrn_kernel(z_ref, lbp_ref, g_ref, o_ref, q_ref, lf_ref, kd_ref, accf_ref, accb_ref, stf_ref, stb_ref,
                 *, s, layer):
    rb = ROW_BLOCK
    x = lbp_ref[...]
    e = jnp.exp(x - jnp.max(x, axis=0, keepdims=True))
    p = e / jnp.sum(e, axis=0, keepdims=True)
    lb = p[0:1]
    for j in range(1, layer + 1):
        lb = lb + p[j:j + 1]
    lb = lb - p[0:1]

    def prep(st, i):
        qz = z_ref[pl.ds(st, rb), ZD['q']]
        q_ref[pl.ds(st, rb), :] = qz * jax.nn.sigmoid(qz)
        fz = z_ref[pl.ds(st, rb), ZD['f']]
        lf_ref[pl.ds(st, rb), :] = jnp.log(lb + (1.0 - lb) * jax.nn.sigmoid(fz))
        kd_ref[pl.ds(st, rb), :] = (1.0 - lb) * jax.nn.sigmoid(-fz)
    _rows(s, rb, prep)

    kw = HEADS * HGRN_DF

    def load(st, d):
        q = q_ref[pl.ds(st, HGRN_CHUNK), :]
        k = kd_ref[pl.ds(st, HGRN_CHUNK), d * kw:(d + 1) * kw]
        v = z_ref[pl.ds(st, HGRN_CHUNK), ZD['i']]
        g = lf_ref[pl.ds(st, HGRN_CHUNK), d * kw:(d + 1) * kw]
        return q, k, v, g
    _gl_bidir([functools.partial(load, d=d) for d in range(2)], (accf_ref, accb_ref), (stf_ref, stb_ref),
              s=s, c=HGRN_CHUNK, dk=HGRN_DF)

    mavg = _block_avg(BRANCH_W, HEAD_W)

    def out(st, i):
        y = _head_rmsnorm(accf_ref[pl.ds(st, rb), :] + accb_ref[pl.ds(st, rb), :], g_ref[...], mavg)
        gate = z_ref[pl.ds(st, rb), ZD['g']]
        o_ref[pl.ds(st, rb), :] = y * (gate * jax.nn.sigmoid(gate))
    _rows(s, rb, out)


def _hgrn(zd, lbp, g, nb, s, layer):
    kw = HEADS * HGRN_DF
    return pl.pallas_call(
        functools.partial(_hgrn_kernel, s=s, layer=layer),
        grid=(nb,),
        in_specs=[pl.BlockSpec((None, s, ZD_W), lambda b: (b, 0, 0)),
                  _const_spec((DEPTH, 2 * kw)), _const_spec((1, BRANCH_W))],
        out_specs=pl.BlockSpec((None, s, BRANCH_W), lambda b: (b, 0, 0)),
        out_shape=jax.ShapeDtypeStruct((nb, s, BRANCH_W), f32),
        scratch_shapes=[pltpu.VMEM((s, kw), f32), pltpu.VMEM((s, 2 * kw), f32), pltpu.VMEM((s, 2 * kw), f32)]
        + [pltpu.VMEM((s, BRANCH_W), f32)] * 2 + [pltpu.VMEM((BRANCH_W, kw), f32)] * 2,
        compiler_params=_cparams("parallel"),
        name="hgrn",
    )(zd.reshape(nb, s, ZD_W), lbp, g)


def _mlstm_bidir(z_ref, q_ref, k_ref, acc_refs, st_refs, nv_refs, m_refs, *, s):
    c, w = MLSTM_CHUNK, BRANCH_W
    n, un = s // c, MLSTM_UNROLL
    ones = jnp.ones((c, c), bf16)
    lane, row = _iota((c, w), 1), _iota((c, w), 0)
    head, sidx = lane // HEAD_W, lane % HEAD_W
    diag = sidx == row
    same = (_iota((w, w), 0) // HEAD_W) == (_iota((w, w), 1) // HEAD_W)
    bones = jnp.where(same, 1.0, 0.0).astype(bf16)
    tris = (_tri(c, False), _tri(c, True))
    causals = (sidx <= row, sidx >= row)
    for d in range(2):
        st_refs[d][...] = jnp.zeros((w, w), f32)
        nv_refs[d][...] = jnp.zeros((1, w), f32)
        m_refs[d][...] = jnp.zeros((1, w), f32)

    streams = [(d, u) for u in range(un) for d in range(2)]

    def seg_max(x):
        out = jnp.full((c, w), NEG_BIG, f32)
        for j in range(HEADS):
            mj = jnp.max(jnp.where(head == j, x, NEG_BIG), axis=-1, keepdims=True)
            out = jnp.where(head == j, mj, out)
        return out

    def body(ci, carry):
        idx = [(ci * un + u) if d == 0 else (n - 1 - ci * un - u) for d, u in streams]
        sts = [pl.multiple_of(i * c, c) for i in idx]
        qs = [q_ref[pl.ds(st, c), :] for st in sts]
        ks = [k_ref[pl.ds(st, c), :] for st in sts]
        vs = [z_ref[pl.ds(st, c), ZB['v']] for st in sts]
        igs = [z_ref[pl.ds(st, c), _sub(ZB['i'], d, w)] for (d, _), st in zip(streams, sts)]
        lfs = [jax.nn.log_sigmoid(z_ref[pl.ds(st, c), _sub(ZB['f'], d, w)])
               for (d, _), st in zip(streams, sts)]
        bs = [_lmul3(tris[d], lf) for (d, _), lf in zip(streams, lfs)]
        rs = [_lmul3(ones, jnp.where(diag, ig - b, 0.0)) for ig, b in zip(igs, bs)]
        log_ds = [jnp.where(causals[d], b + r, NEG_BIG) for (d, _), b, r in zip(streams, bs, rs)]
        mrows = [seg_max(x) for x in log_ds]
        lasts = [b[0:1] if d == 1 else b[c - 1:c] for (d, _), b in zip(streams, bs)]
        log_ws = [l - b + ig for l, b, ig in zip(lasts, bs, igs)]
        mws = [jnp.max(x, axis=0, keepdims=True) for x in log_ws]
        qbs = [q.astype(bf16) for q in qs]
        kexps = [jnp.concatenate([jnp.where(head == j, k, 0.0).astype(bf16) for j in range(HEADS)], axis=0)
                 for k in ks]
        vexps = [jnp.concatenate([jnp.where(head == j, v, 0.0).astype(bf16) for j in range(HEADS)], axis=0)
                 for v in vs]
        qks = [_mm_nt(qb, kexp) for qb, kexp in zip(qbs, kexps)]
        ms = [m_ref[...] for m_ref in m_refs]
        m_ts, w_prevs, w_ss, decs = [], [], [], []
        for i, (d, _) in enumerate(streams):
            log_prev = bs[i] + ms[d]
            m_t = jnp.maximum(mrows[i], log_prev)
            m_new = jnp.maximum(lasts[i] + ms[d], mws[i])
            m_ts.append(m_t)
            w_prevs.append(jnp.exp(log_prev - m_t))
            w_ss.append(jnp.exp(log_ws[i] - m_new))
            decs.append(jnp.exp(lasts[i] + ms[d] - m_new))
            ms[d] = m_new
        scs = [qk * jnp.exp(log_d - m_t) for qk, log_d, m_t in zip(qks, log_ds, m_ts)]
        nums = [_mm(sc.astype(bf16), vexp) for sc, vexp in zip(scs, vexps)]
        dens = [_rmul3(sc, bones) for sc in scs]
        upds = [jnp.where(same, _mm_tn((v * w_s).astype(bf16), k.astype(bf16)), 0.0)
                for v, w_s, k in zip(vs, w_ss, ks)]
        nvus = [jnp.sum(w_s * k, axis=0, keepdims=True) for w_s, k in zip(w_ss, ks)]
        states = [st_ref[...] for st_ref in st_refs]
        nvs = [nv_ref[...] for nv_ref in nv_refs]
        for i, (d, _) in enumerate(streams):
            num = w_prevs[i] * _mm_nt(qbs[i], states[d].astype(bf16)) + nums[i]
            den = w_prevs[i] * _rmul3(qs[i] * nvs[d], bones) + dens[i]
            acc_refs[d][pl.ds(sts[i], c), :] = num / jnp.maximum(jnp.abs(den), jnp.exp(-m_ts[i]))
            states[d] = decs[i] * states[d] + upds[i]
            nvs[d] = decs[i] * nvs[d] + nvus[i]
        for d in range(2):
            st_refs[d][...] = states[d]
            nv_refs[d][...] = nvs[d]
            m_refs[d][...] = ms[d]
        return carry

    lax.fori_loop(0, n // un, body, 0)


def _mlstm_kernel(z_ref, cw_ref, cb_ref, g_ref, o_ref, q_ref, k_ref, accf_ref, accb_ref, stf_ref, stb_ref,
                  nvf_ref, nvb_ref, mf_ref, mb_ref, *, s):
    rb = ROW_BLOCK
    nblk = s // rb
    w = BRANCH_W

    qk = slice(ZB['q'].start, ZB['k'].stop)

    def conv(st, i):
        x = z_ref[pl.ds(st, rb), qk]
        pst = pl.multiple_of(jnp.maximum(st - 8, 0), 8)
        nst = pl.multiple_of(jnp.minimum(st + rb, s - 8), 8)
        prev = z_ref[pl.ds(pst, 8), qk][7:8]
        nxt = z_ref[pl.ds(nst, 8), qk][0:1]
        prev = jnp.where(i == 0, 0.0, prev)
        nxt = jnp.where(i == nblk - 1, 0.0, nxt)
        row = _iota((rb, 2 * w), 0)
        xm = jnp.where(row == 0, prev, pltpu.roll(x, 1, 0))
        xp = jnp.where(row == rb - 1, nxt, pltpu.roll(x, rb - 1, 0))
        y = cb_ref[...] + xm * cw_ref[0:1, :]
        y = y + x * cw_ref[1:2, :]
        y = y + xp * cw_ref[2:3, :]
        y = y * jax.nn.sigmoid(y)
        q_ref[pl.ds(st, rb), :] = y[:, 0:w]
        k_ref[pl.ds(st, rb), :] = y[:, w:2 * w] * (HEAD_W ** -0.5)
    _rows(s, rb, conv)

    _mlstm_bidir(z_ref, q_ref, k_ref, (accf_ref, accb_ref), (stf_ref, stb_ref), (nvf_ref, nvb_ref),
                 (mf_ref, mb_ref), s=s)

    mavg = _block_avg(w, HEAD_W)

    def out(st, i):
        y = _head_rmsnorm(accf_ref[pl.ds(st, rb), :] + accb_ref[pl.ds(st, rb), :], g_ref[...], mavg)
        o_ref[pl.ds(st, rb), :] = jax.nn.sigmoid(z_ref[pl.ds(st, rb), ZB['o']]) * y
    _rows(s, rb, out)


def _mlstm(zb, cw, cb, g, nb, s):
    w = BRANCH_W
    return pl.pallas_call(
        functools.partial(_mlstm_kernel, s=s),
        grid=(nb,),
        in_specs=[pl.BlockSpec((None, s, ZB_W), lambda b: (b, 0, 0)),
                  _const_spec((3, 2 * w)), _const_spec((1, 2 * w)), _const_spec((1, w))],
        out_specs=pl.BlockSpec((None, s, w), lambda b: (b, 0, 0)),
        out_shape=jax.ShapeDtypeStruct((nb, s, w), f32),
        scratch_shapes=[pltpu.VMEM((s, w), f32)] * 4 + [pltpu.VMEM((w, w), f32)] * 2 + [pltpu.VMEM((1, w), f32)] * 4,
        compiler_params=_cparams("parallel"),
        name="mlstm",
    )(zb.reshape(nb, s, ZB_W), cw, cb, g)


def _rope(x, cos, sa, sb):
    w = x.shape[-1]
    half = HEAD_W // 2
    return x * cos + pltpu.roll(x, w - half, 1) * sa + pltpu.roll(x, half, 1) * sb


def _dil_bias(s):
    rel = jnp.arange(s, dtype=i32)[None, :] - jnp.arange(s, dtype=i32)[:, None]
    dist = jnp.abs(rel)
    cnt = jnp.zeros((s, s), f32)
    for window, dil in DIL_PATTERNS:
        ok = dist <= (window // (2 * dil)) * dil
        if dil > 1:
            ok = jnp.logical_and(ok, (rel & (dil - 1)) == 0)
        cnt = cnt + jnp.where(ok, 1.0, 0.0)
    return jnp.where(cnt > 0.0, jnp.log(jnp.maximum(cnt, 1.0)), NEG_BIG)


def _dil_kernel(z_ref, cos_ref, sa_ref, sb_ref, bias_ref, o_ref, kr_ref, vm_ref, *, s, tq):
    w = BRANCH_W
    qi = pl.program_id(1)
    rb = ROW_BLOCK

    @pl.when(qi == 0)
    def _():
        head = _iota((rb, w), 1) // HEAD_W

        def prep(st, i):
            x = z_ref[pl.ds(st, rb), ZC['k']]
            kr_ref[pl.ds(st, rb), :] = _rope(x, cos_ref[pl.ds(st, rb), :], sa_ref[pl.ds(st, rb), :],
                                             sb_ref[pl.ds(st, rb), :]).astype(bf16)
            v = z_ref[pl.ds(st, rb), ZC['v']]
            for j in range(HEADS):
                vm_ref[j, pl.ds(st, rb), :] = jnp.where(head == j, v, 0.0).astype(bf16)
        _rows(s, rb, prep)

    q0 = pl.multiple_of(qi * tq, tq)
    qr = _rope(z_ref[pl.ds(q0, tq), ZC['q']], cos_ref[pl.ds(q0, tq), :], sa_ref[pl.ds(q0, tq), :],
               sb_ref[pl.ds(q0, tq), :]) * (HEAD_W ** -0.5)
    qhead = _iota((tq, w), 1) // HEAD_W
    acc = jnp.zeros((tq, w), f32)
    for j in range(HEADS):
        qh = jnp.where(qhead == j, qr, 0.0).astype(bf16)
        sc = _mm_nt(qh, kr_ref[...]) + bias_ref[...]
        m = jnp.max(sc, axis=-1, keepdims=True)
        p = jnp.exp(sc - m)
        l = jnp.sum(p, axis=-1, keepdims=True)
        acc = acc + _mm(p.astype(bf16), vm_ref[j]) / l
    o_ref[...] = acc


def _dil(zc, cos, sa, sb, bias, nb, s, tq=512):
    w = BRANCH_W
    for _, dil in DIL_PATTERNS:
        assert dil & (dil - 1) == 0 and s % dil == 0
    return pl.pallas_call(
        functools.partial(_dil_kernel, s=s, tq=tq),
        grid=(nb, s // tq),
        in_specs=[pl.BlockSpec((None, s, ZC_W), lambda b, i: (b, 0, 0)),
                  _const_spec((s, w)), _const_spec((s, w)), _const_spec((s, w)),
                  pl.BlockSpec((tq, s), lambda b, i: (i, 0))],
        out_specs=pl.BlockSpec((None, tq, w), lambda b, i: (b, i, 0)),
        out_shape=jax.ShapeDtypeStruct((nb, s, w), f32),
        scratch_shapes=[pltpu.VMEM((s, w), bf16), pltpu.VMEM((HEADS, s, w), bf16)],
        compiler_params=_cparams("parallel", "arbitrary"),
        name="dilattn",
    )(zc.reshape(nb, s, ZC_W), cos, sa, sb, bias)


def _merge_kernel(h_ref, ya, yb, yc, yd, g_ref, wg_ref, wbr_ref, wo_ref, o_ref):
    x = h_ref[...]
    u = _rms(x, g_ref[...]).astype(bf16)
    merged = None
    for n, y in enumerate((ya, yb, yc, yd)):
        gate = jax.nn.sigmoid(_mm(u, wg_ref[n]))
        term = gate * _mm(y[...].astype(bf16), wbr_ref[n])
        merged = term if merged is None else merged + term
    o_ref[...] = x + _mm(merged.astype(bf16), wo_ref[...])


def _merge(h, ys, g, wg, wbr, wo, tm=512):
    t = h.shape[0]
    tok = lambda w: pl.BlockSpec((tm, w), lambda i: (i, 0))
    return pl.pallas_call(
        _merge_kernel,
        grid=(t // tm,),
        in_specs=[tok(D_MODEL)] + [tok(BRANCH_W)] * 4
        + [_const_spec((1, D_MODEL)), _const_spec((4, D_MODEL, D_MODEL)),
           _const_spec((4, BRANCH_W, D_MODEL)), _const_spec((D_MODEL, D_MODEL))],
        out_specs=tok(D_MODEL),
        out_shape=jax.ShapeDtypeStruct((t, D_MODEL), f32),
        compiler_params=_cparams("parallel"),
        name="merge",
    )(h, *ys, g, wg, wbr, wo)


def _stair_cells():
    return [(a, b) for a in range(PEER_TOPK) for b in range(PEER_TOPK) if (a + 1) * (b + 1) <= PEER_TOPK]


_CELLS = _stair_cells()
_NCELL = len(_CELLS)
_NCELL_PAD = -(-_NCELL // 8) * 8


def _extract_top(works, rowf, nrows):
    works = list(works)
    tops = [[] for _ in works]
    ranks = [jnp.full(w.shape, float(PEER_TOPK), f32) for w in works]
    for r in range(PEER_TOPK):
        mxs = [jnp.max(w, axis=0, keepdims=True) for w in works]
        cands = [jnp.where(w == mx, rowf, float(nrows)) for w, mx in zip(works, mxs)]
        hits = [cand == jnp.min(cand, axis=0, keepdims=True) for cand in cands]
        works = [jnp.where(hit, -jnp.inf, w) for hit, w in zip(hits, works)]
        ranks = [jnp.where(hit, float(r), rank) for hit, rank in zip(hits, ranks)]
        for t, mx in zip(tops, mxs):
            t.append(mx)
    return tops, ranks


def _distinct_maxima(xs):
    tops = [[] for _ in xs]
    mxs = [None] * len(xs)
    for _ in range(PEER_TOPK):
        curs = [x if mx is None else jnp.where(x < mx, x, -jnp.inf) for x, mx in zip(xs, mxs)]
        mxs = [jnp.max(cur, axis=0, keepdims=True) for cur in curs]
        for t, mx in zip(tops, mxs):
            t.append(mx)
    return tops


def _count(mask):
    return jnp.sum(jnp.where(mask, 1.0, 0.0), axis=0, keepdims=True)


def _route_kernel(h_ref, g_ref, wpq_ref, keys_ref, sa_ref, ut_ref, a_ref, cnt_ref, rk_ref, bn_ref,
                  q_ref, *m_refs, tm):
    uf = _rms(h_ref[...], g_ref[...])
    ut_ref[...] = uf.T.astype(bf16)
    q_ref[...] = _mm(uf.astype(bf16), wpq_ref[...]).astype(bf16)
    nk = PEER_NKEYS
    topk = float(PEER_TOPK)

    def scores(hd, toks):
        return [_mm_nt(keys_ref[hd * 2 + p], q_ref[toks, pl.ds(pl.multiple_of((hd * 2 + p) * nk, nk), nk)])
                for p in range(2)]

    def write_tables(hd, amat, cnt, rank2, bnorm):
        rows = pl.ds(pl.multiple_of(hd * nk, nk), nk)
        prow = pl.ds(pl.multiple_of(hd * (nk // 2), nk // 2), nk // 2)
        rkp = pltpu.bitcast(rank2.astype(bf16), jnp.uint32)
        bnp = pltpu.bitcast(bnorm.astype(bf16), jnp.uint32)
        amat = 0.5 * amat
        for tb in range(tm // LANES):
            lanes = slice(tb * LANES, (tb + 1) * LANES)
            a_ref[tb, rows, :] = amat[:, lanes]
            cnt_ref[tb, rows, :] = cnt[:, lanes]
            rk_ref[tb, prow, :] = rkp[:, lanes]
            bn_ref[tb, prow, :] = bnp[:, lanes]

    def stair_sums(tops, m_ref):
        m_ref[...] = jnp.full((_NCELL_PAD, tm), -jnp.inf, f32)
        for ci, (a, b) in enumerate(_CELLS):
            m_ref[ci:ci + 1, :] = tops[0][a] + tops[1][b]
        return m_ref[...]

    def fast_tiles(hds):
        scs = [scores(hd, slice(0, tm)) for hd in hds]
        flat = _distinct_maxima([s for pair in scs for s in pair])
        tops = [flat[2 * i:2 * i + 2] for i in range(len(hds))]
        rank2s = [jnp.zeros((nk, tm), f32) for _ in hds]
        for r in range(PEER_TOPK):
            rank2s = [rank2 + jnp.where(sc[1] < top[1][r], 1.0, 0.0) for rank2, sc, top in zip(rank2s, scs, tops)]
        tied = jnp.zeros((1, tm), f32)
        for sc, top, rank2 in zip(scs, tops, rank2s):
            tied = jnp.where(_count(sc[0] >= top[0][-1]) == topk, tied, 1.0)
            tied = jnp.where(_count(rank2 < topk) == topk, tied, 1.0)
        cands = [stair_sums(top, m_ref) for top, m_ref in zip(tops, m_refs)]
        ctops = _distinct_maxima(cands)
        for hd, sc, top, rank2, cand, ctop in zip(hds, scs, tops, rank2s, cands, ctops):
            chosen = cand >= ctop[-1]
            tied = jnp.where(_count(chosen) == topk, tied, 1.0)
            zsum = jnp.sum(jnp.where(chosen, jnp.exp(cand - cand[0:1]), 0.0), axis=0, keepdims=True)
            rowcnt = _mm(sa_ref[...], jnp.where(chosen, 1.0, 0.0).astype(bf16))
            cnt = jnp.zeros((nk, tm), f32)
            for a in range(PEER_TOPK):
                cnt = jnp.where(sc[0] == top[0][a], rowcnt[a:a + 1], cnt)
            write_tables(hd, jnp.exp(sc[0] - top[0][0]), cnt, rank2, jnp.exp(sc[1] - top[1][0]) / zsum)
        return tied

    def exact_tile(hd, m_ref):
        rowf = _iota((nk, tm), 0).astype(f32)
        rowcf = _iota((_NCELL_PAD, tm), 0).astype(f32)
        scs = scores(hd, slice(0, tm))
        tops, ranks = _extract_top(scs, rowf, nk)
        cand = stair_sums(tops, m_ref)
        _, (crank,) = _extract_top([cand], rowcf, _NCELL_PAD)
        chosen = crank < topk
        zsum = jnp.sum(jnp.where(chosen, jnp.exp(cand - cand[0:1]), 0.0), axis=0, keepdims=True)
        rowcnt = _mm(sa_ref[...], jnp.where(chosen, 1.0, 0.0).astype(bf16))
        cnt = jnp.zeros((nk, tm), f32)
        for a in range(PEER_TOPK):
            cnt = jnp.where(ranks[0] == float(a), rowcnt[a:a + 1], cnt)
        write_tables(hd, jnp.exp(scs[0] - tops[0][0]), cnt, ranks[1], jnp.exp(scs[1] - tops[1][0]) / zsum)

    def per_group(hg, carry):
        hds = [hg * ROUTE_HEADS_TOGETHER + j for j in range(ROUTE_HEADS_TOGETHER)]
        tied = fast_tiles(hds)

        @pl.when(jnp.max(tied) > 0.0)
        def _():
            for hd, m_ref in zip(hds, m_refs):
                exact_tile(hd, m_ref)
        return carry

    lax.fori_loop(0, PEER_HEADS // ROUTE_HEADS_TOGETHER, per_group, 0)


def _route(h, g, wpq, keys, sa, tm=256):
    t = h.shape[0]
    nh, nk = PEER_HEADS, PEER_NKEYS
    tab = pl.BlockSpec((tm // LANES, nh * nk, LANES), lambda i: (i, 0, 0))
    ptab = pl.BlockSpec((tm // LANES, nh * nk // 2, LANES), lambda i: (i, 0, 0))
    return pl.pallas_call(
        functools.partial(_route_kernel, tm=tm),
        grid=(t // tm,),
        in_specs=[pl.BlockSpec((tm, D_MODEL), lambda i: (i, 0)), _const_spec((1, D_MODEL)),
                  _const_spec((D_MODEL, 2 * nh * nk)), _const_spec((2 * nh, nk, nk)),
                  _const_spec((PEER_TOPK, _NCELL_PAD))],
        out_specs=[pl.BlockSpec((D_MODEL, tm), lambda i: (0, i)), tab, tab, ptab, ptab],
        out_shape=[jax.ShapeDtypeStruct((D_MODEL, t), bf16)] + [jax.ShapeDtypeStruct((t // LANES, nh * nk, LANES), f32)] * 2
        + [jax.ShapeDtypeStruct((t // LANES, nh * nk // 2, LANES), jnp.uint32)] * 2,
        scratch_shapes=[pltpu.VMEM((tm, 2 * nh * nk), bf16)]
        + [pltpu.VMEM((_NCELL_PAD, tm), f32)] * ROUTE_HEADS_TOGETHER,
        compiler_params=_cparams("parallel"),
        name="route",
    )(h, g, wpq, keys, sa)


def _twice_gelu_bf16(x):
    return (x * (1.0 + lax.erf(x * math.sqrt(0.5)))).astype(bf16)


def _peer_gate_block(act_ref, p_ref, tabs, i0, ii, tb):
    a_ref, cnt_ref, rk_ref, bn_ref = tabs
    nk = PEER_NKEYS
    lanes = slice(tb * LANES, (tb + 1) * LANES)
    gate = None
    for hd in range(PEER_HEADS):
        row = pl.ds(hd * nk + i0 + ii, 1)
        a = jnp.broadcast_to(a_ref[tb, row, :], (nk, LANES)).astype(bf16)
        c = jnp.broadcast_to(cnt_ref[tb, row, :], (nk, LANES)).astype(bf16)
        prow = slice(hd * (nk // 2), (hd + 1) * (nk // 2))
        rk = pltpu.bitcast(rk_ref[tb, prow, :], bf16)
        bn = pltpu.bitcast(bn_ref[tb, prow, :], bf16)
        term = a * jnp.where(rk < c, bn, jnp.zeros((), bf16))
        gate = term if gate is None else gate + term
    p_ref[ii * nk:(ii + 1) * nk, lanes] = gate * act_ref[ii * nk:(ii + 1) * nk, lanes]


def _mxu_product(rhs_tile, lhs_tile, between):
    nt = PEER_SUB // MXU_TILE
    for q in range(2):
        pltpu.matmul_push_rhs(rhs_tile(0, q), staging_register=0, mxu_index=q)
    for kt in range(nt):
        reg = kt % 2
        for mc in range(nt):
            lhs = lhs_tile(mc, kt)
            for q in range(2):
                pltpu.matmul_acc_lhs(mc * (MXU_TILE // 4), lhs, mxu_index=q,
                                     load_staged_rhs=(reg if mc == 0 else None))
            if mc == 0 and kt + 1 < nt:
                for q in range(2):
                    pltpu.matmul_push_rhs(rhs_tile(kt + 1, q), staging_register=1 - reg, mxu_index=q)
            between()


def _peer_kernel(ut_ref, a_ref, cnt_ref, rk_ref, bn_ref, pu_ref, pvt_ref, o_ref, acc_ref, act0, act1, p0, p1,
                 *, ne, tp):
    g = pl.program_id(0)
    nblocks = pl.num_programs(0) - 2
    b1 = jnp.clip(g - 1, 0, nblocks - 1)
    b2 = jnp.clip(g - 2, 0, nblocks - 1)
    i1 = (b1 % ne) * (PEER_SUB // PEER_NKEYS)
    tabs = (a_ref, cnt_ref, rk_ref, bn_ref)

    @pl.when(g == 0)
    def _():
        for ref in (act0, act1, p0, p1):
            ref[...] = jnp.zeros(ref.shape, bf16)

    @pl.when(b2 % ne == 0)
    def _():
        acc_ref[...] = jnp.zeros(acc_ref.shape, f32)

    t256 = lambda i: slice(i * MXU_TILE, (i + 1) * MXU_TILE)

    def stage(act_w, act_r, p_w, p_r):
        blocks = iter([(ii, tb) for ii in range(PEER_SUB // PEER_NKEYS) for tb in range(tp // LANES)])

        def one_gate_block():
            ii, tb = next(blocks)
            _peer_gate_block(act_r, p_w, tabs, i1, ii, tb)

        nt = PEER_SUB // MXU_TILE
        _mxu_product(lambda kt, q: ut_ref[t256(kt), t256(q)], lambda mc, kt: pu_ref[t256(mc), t256(kt)],
                     one_gate_block)
        for mc in range(nt):
            for q in range(2):
                x = pltpu.matmul_pop(mc * (MXU_TILE // 4), (MXU_TILE, MXU_TILE), f32, q)
                act_w[t256(mc), t256(q)] = _twice_gelu_bf16(x)
        _mxu_product(lambda kt, q: p_r[t256(kt), t256(q)], lambda mc, kt: pvt_ref[t256(mc), t256(kt)],
                     one_gate_block)
        for mc in range(nt):
            for q in range(2):
                acc_ref[t256(mc), t256(q)] += pltpu.matmul_pop(mc * (MXU_TILE // 4), (MXU_TILE, MXU_TILE), f32, q)

    @pl.when(g % 2 == 0)
    def _():
        stage(act0, act1, p1, p0)

    @pl.when(g % 2 == 1)
    def _():
        stage(act1, act0, p0, p1)

    @pl.when(jnp.logical_and(g >= 2, b2 % ne == ne - 1))
    def _():
        o_ref[...] = acc_ref[...].T


def _peer(ut, tabs, pu, pvt):
    t = ut.shape[1]
    tp = 2 * MXU_TILE
    nh, nk = PEER_HEADS, PEER_NKEYS
    assert D_MODEL == PEER_SUB and (PEER_SUB // nk) * (tp // LANES) == 2 * (PEER_SUB // MXU_TILE) ** 2
    ne = pu.shape[0] // PEER_SUB
    nblocks = (t // tp) * ne
    blk = lambda lag: (lambda g: jnp.clip(g - lag, 0, nblocks - 1))
    b0, b1, b2 = blk(0), blk(1), blk(2)
    tab = pl.BlockSpec((tp // LANES, nh * nk, LANES), lambda g: (b1(g) // ne, 0, 0))
    ptab = pl.BlockSpec((tp // LANES, nh * nk // 2, LANES), lambda g: (b1(g) // ne, 0, 0))
    return pl.pallas_call(
        functools.partial(_peer_kernel, ne=ne, tp=tp),
        grid=(nblocks + 2,),
        in_specs=[pl.BlockSpec((D_MODEL, tp), lambda g: (0, b0(g) // ne)), tab, tab, ptab, ptab,
                  pl.BlockSpec((PEER_SUB, D_MODEL), lambda g: (b0(g) % ne, 0)),
                  pl.BlockSpec((D_MODEL, PEER_SUB), lambda g: (0, b2(g) % ne))],
        out_specs=pl.BlockSpec((tp, D_MODEL), lambda g: (b2(g) // ne, 0)),
        out_shape=jax.ShapeDtypeStruct((t, D_MODEL), f32),
        scratch_shapes=[pltpu.VMEM((D_MODEL, tp), f32)] + [pltpu.VMEM((PEER_SUB, tp), bf16)] * 4,
        compiler_params=_cparams("arbitrary"),
        name="peer",
    )(ut, *tabs, pu, pvt)


def _pe_kernel(h_ref, po_ref, p_ref, g_ref, wpg_ref, wpe_ref, gf_ref, o_ref, *, final):
    h = h_ref[...] + po_ref[...]
    gate = jax.nn.sigmoid(_mm(_rms(h, g_ref[...]).astype(bf16), wpg_ref[...]))
    h = h + gate * _mm(p_ref[...].astype(bf16), wpe_ref[...])
    if final:
        h = _rms(h, gf_ref[...])
    o_ref[...] = h


def _pe(h, po, p, g, wpg, wpe, gf, final, tm=512):
    t = h.shape[0]
    tok = lambda w: pl.BlockSpec((tm, w), lambda i: (i, 0))
    return pl.pallas_call(
        functools.partial(_pe_kernel, final=final),
        grid=(t // tm,),
        in_specs=[tok(D_MODEL), tok(D_MODEL), tok(PE_DIM), _const_spec((1, D_MODEL)),
                  _const_spec((D_MODEL, D_MODEL)), _const_spec((PE_DIM, D_MODEL)), _const_spec((1, D_MODEL))],
        out_specs=tok(D_MODEL),
        out_shape=jax.ShapeDtypeStruct((t, D_MODEL), f32),
        compiler_params=_cparams("parallel"),
        name="pe_embed",
    )(h, po, p, g, wpg, wpe, gf)


def _cols(w, name):
    a, b = _PROJ_OFF[name]
    return w[..., a:b]


def _proj_groups(w):
    lr_real = _PROJ_OFF['a_lr'][1] - _PROJ_OFF['a_lr'][0]
    pad = jnp.zeros(w.shape[:-1] + (ZA['lr'].stop - ZA['lr'].start - lr_real,), w.dtype)
    ga = jnp.concatenate([_cols(w, n) for n in ('a_q', 'a_k', 'a_v', 'a_g', 'a_lr')] + [pad], axis=-1)
    gates = [jnp.repeat(_cols(w, n), HEAD_W, axis=-1) for n in ('b_i', 'b_f')]
    gb = jnp.concatenate([_cols(w, n) for n in ('b_q', 'b_k', 'b_v', 'b_o')] + gates, axis=-1)
    gc = jnp.concatenate([_cols(w, n) for n in ('c_q', 'c_k', 'c_v')], axis=-1)
    gd = jnp.concatenate([_cols(w, n) for n in ('d_q', 'd_f', 'd_i', 'd_g')], axis=-1)
    return ga, gb, gc, gd


def _rope_tables(s):
    half = HEAD_W // 2
    inv = ROPE_THETA ** (-jnp.arange(half, dtype=f32) / half)
    ang = jnp.arange(s, dtype=f32)[:, None] * inv[None, :]
    cos, sin = jnp.cos(ang), jnp.sin(ang)
    zero = jnp.zeros_like(sin)
    tile = lambda a, b: jnp.tile(jnp.concatenate([a, b], axis=-1), (1, HEADS))
    return tile(cos, cos), tile(-sin, zero), tile(zero, sin)


def _row_count_matrix():
    m = np.zeros((PEER_TOPK, _NCELL_PAD), np.float32)
    for ci, (a, _) in enumerate(_CELLS):
        m[a, ci] = 1.0
    return jnp.asarray(m, bf16)


def _prepare(g_mix, w_in, b_in, w_lr2, b_lr2, conv_w, conv_b, lb_param, g_a, g_b, g_d, w_gate, w_br, w_out,
             g_ffn, w_pq, sub_keys, peer_u, peer_v, g_pe, w_pe, w_pg, g_final):
    layers = []
    row = lambda a: a.reshape(1, -1)
    for l in range(DEPTH):
        wlr = jnp.zeros((LANES, 256), f32)
        for n in range(2):
            wlr = wlr.at[n * GLA_RANK:(n + 1) * GLA_RANK, n * 128:(n + 1) * 128].set(w_lr2[l, n])
        layers.append(dict(
            g_mix=row(g_mix[l]),
            w_proj=[w.astype(bf16) for w in _proj_groups(w_in[l])],
            b_proj=list(_proj_groups(row(b_in[l]))),
            wlr=wlr.astype(bf16), blr=row(b_lr2[l]),
            conv_w=conv_w[l], conv_b=row(conv_b[l]),
            g_a=row(g_a[l]), g_b=row(g_b[l]), g_d=row(g_d[l]),
            w_gate=w_gate[l].astype(bf16), w_br=w_br[l].astype(bf16), w_out=w_out[l].astype(bf16),
            g_ffn=row(g_ffn[l]), w_pq=w_pq[l].astype(bf16),
            keys=sub_keys[l].reshape(2 * PEER_HEADS, PEER_NKEYS, -1).astype(bf16),
            peer_u=peer_u[l].astype(bf16), peer_vt=peer_v[l].astype(bf16).T,
            g_pe=row(g_pe[l]), w_pe=w_pe[l].astype(bf16), w_pg=w_pg[l].astype(bf16),
        ))
    shared = dict(
        lbp=jnp.transpose(lb_param, (1, 0, 2)).reshape(DEPTH, -1),
        g_final=row(g_final), sa=_row_count_matrix(),
    )
    return layers, shared


def _trunk(x, p, layers, shared):
    nb, s, d = x.shape
    t = nb * s
    h = x.reshape(t, d)
    cos, sa, sb = _rope_tables(s)
    dil_bias = _dil_bias(s)
    for l, w in enumerate(layers):
        za, zb, zc, zd = _proj(h, w['g_mix'], w['w_proj'], w['b_proj'])
        ya = _gla(za, w['wlr'], w['blr'], w['g_a'], nb, s)
        yb = _mlstm(zb, w['conv_w'], w['conv_b'], w['g_b'], nb, s)
        yc = _dil(zc, cos, sa, sb, dil_bias, nb, s)
        yd = _hgrn(zd, shared['lbp'], w['g_d'], nb, s, l)
        ys = [y.reshape(t, BRANCH_W) for y in (ya, yb, yc, yd)]
        h1 = _merge(h, ys, w['g_mix'], w['w_gate'], w['w_br'], w['w_out'])
        ut, *tabs = _route(h1, w['g_ffn'], w['w_pq'], w['keys'], shared['sa'])
        po = _peer(ut, tabs, w['peer_u'], w['peer_vt'])
        h = _pe(h1, po, p[l].reshape(t, PE_DIM), w['g_pe'], w['w_pg'], w['w_pe'], shared['g_final'],
                final=(l == DEPTH - 1))
    return h.reshape(nb, s, d)


def kernel(x_prompt, x_sample, p_prompt, p_sample, g_mix, w_in, b_in, w_lr2, b_lr2, conv_w, conv_b, lb_param, g_a, g_b, g_d, w_gate, w_br, w_out, g_ffn, w_pq, sub_keys, peer_u, peer_v, g_pe, w_pe, w_pg, g_final):
    layers, shared = _prepare(g_mix, w_in, b_in, w_lr2, b_lr2, conv_w, conv_b, lb_param, g_a, g_b, g_d, w_gate,
                              w_br, w_out, g_ffn, w_pq, sub_keys, peer_u, peer_v, g_pe, w_pe, w_pg, g_final)
    return (_trunk(x_prompt, p_prompt, layers, shared), _trunk(x_sample, p_sample, layers, shared))
```

```python
import functools
import math

import numpy as np
import jax
import jax.numpy as jnp
from jax import lax
from jax.experimental import pallas as pl
from jax.experimental.pallas import tpu as pltpu

f32 = jnp.float32
bf16 = jnp.bfloat16
i32 = jnp.int32

D_MODEL = 1024
DEPTH = 2
PE_DIM = 256
EPS = 1e-6
NEG_BIG = -1e30
HEADS = 4
HEAD_W = 64
BRANCH_W = HEADS * HEAD_W
GLA_DK = 32
GLA_RANK = 16
GLA_NORMALIZER = 16.0
GLA_CHUNK = 32
MLSTM_CHUNK = 64
HGRN_DF = 64
HGRN_CHUNK = 32
DIL_PATTERNS = ((128, 1), (512, 4), (2048, 16))
ROPE_THETA = 10000.0
PEER_HEADS = 8
PEER_NKEYS = 128
PEER_TOPK = 16
LANES = 128
PEER_SUB = 1024
MXU_TILE = 256
ROUTE_HEADS_TOGETHER = 2
EXP_CLAMP = 80.0
GL_UNROLL = 8
MLSTM_UNROLL = 2

_PROJ_LAYOUT = (
    ('a_q', 128), ('a_k', 128), ('a_v', 256), ('a_g', 256), ('a_lr', 32),
    ('b_q', 256), ('b_k', 256), ('b_v', 256), ('b_i', 8), ('b_f', 8), ('b_o', 256),
    ('c_q', 256), ('c_k', 256), ('c_v', 256),
    ('d_q', 256), ('d_f', 512), ('d_i', 256), ('d_g', 256),
)
_PROJ_OFF = {}
_o = 0
for _n, _s in _PROJ_LAYOUT:
    _PROJ_OFF[_n] = (_o, _o + _s)
    _o += _s


def _lane_layout(*fields):
    out, start = {}, 0
    for name, width in fields:
        assert width % 128 == 0
        out[name] = slice(start, start + width)
        start += width
    return out, start


def _sub(field, index, width):
    return slice(field.start + index * width, field.start + (index + 1) * width)


ZA, ZA_W = _lane_layout(('q', 128), ('k', 128), ('v', 256), ('g', 256), ('lr', 128))
ZB, ZB_W = _lane_layout(('q', 256), ('k', 256), ('v', 256), ('o', 256), ('i', 512), ('f', 512))
ZC, ZC_W = _lane_layout(('q', 256), ('k', 256), ('v', 256))
ZD, ZD_W = _lane_layout(('q', 256), ('f', 512), ('i', 256), ('g', 256))
ROW_BLOCK = 256

VMEM_LIMIT = 56 * 1024 * 1024


def _cparams(*sem):
    return pltpu.CompilerParams(dimension_semantics=sem, vmem_limit_bytes=VMEM_LIMIT)


def _const_spec(shape):
    nd = len(shape)
    return pl.BlockSpec(shape, lambda *_: (0,) * nd, pipeline_mode=pl.Buffered(1))


def _mm(a, b):
    return jnp.dot(a, b, preferred_element_type=f32)


def _mm_nt(a, b):
    return lax.dot_general(a, b, (((1,), (1,)), ((), ())), preferred_element_type=f32)


def _mm_tn(a, b):
    return lax.dot_general(a, b, (((0,), (0,)), ((), ())), preferred_element_type=f32)


def _split3(x):
    hi = x.astype(bf16)
    r1 = x - hi.astype(f32)
    mid = r1.astype(bf16)
    lo = (r1 - mid.astype(f32)).astype(bf16)
    return hi, mid, lo


def _lmul3(m, x):
    w = x.shape[1]
    y = _mm(m, jnp.concatenate(_split3(x), axis=1))
    return y[:, 0:w] + y[:, w:2 * w] + y[:, 2 * w:3 * w]


def _rmul3(x, m):
    r = x.shape[0]
    y = _mm(jnp.concatenate(_split3(x), axis=0), m)
    return y[0:r] + y[r:2 * r] + y[2 * r:3 * r]


def _rms(x, g):
    return x * lax.rsqrt(jnp.mean(x * x, axis=-1, keepdims=True) + EPS) * g


def _iota(shape, dim):
    return lax.broadcasted_iota(i32, shape, dim)


def _tri(c, reverse):
    r, k = _iota((c, c), 0), _iota((c, c), 1)
    return jnp.where((k >= r) if reverse else (k <= r), 1.0, 0.0).astype(bf16)


def _block_avg(w, blk):
    same = (_iota((w, w), 0) // blk) == (_iota((w, w), 1) // blk)
    return jnp.where(same, 1.0 / blk, 0.0).astype(bf16)


def _rows(n, blk, fn):
    def body(i, c):
        fn(pl.multiple_of(i * blk, blk), i)
        return c
    lax.fori_loop(0, n // blk, body, 0)


def _head_rmsnorm(o, g, mavg):
    ms = _rmul3(o * o, mavg)
    return o * lax.rsqrt(ms + EPS) * g


def _proj_kernel(h_ref, g_ref, wa, wb, wc, wd, ba, bb, bc, bd, za, zb, zc, zd):
    u = _rms(h_ref[...], g_ref[...]).astype(bf16)
    for w, b, z in ((wa, ba, za), (wb, bb, zb), (wc, bc, zc), (wd, bd, zd)):
        z[...] = _mm(u, w[...]) + b[...]


def _proj(h, g, ws, bs, tm=512):
    t = h.shape[0]
    widths = (ZA_W, ZB_W, ZC_W, ZD_W)
    tok = lambda w: pl.BlockSpec((tm, w), lambda i: (i, 0))
    return pl.pallas_call(
        _proj_kernel,
        grid=(t // tm,),
        in_specs=[tok(D_MODEL), _const_spec((1, D_MODEL))]
        + [_const_spec((D_MODEL, w)) for w in widths] + [_const_spec((1, w)) for w in widths],
        out_specs=[tok(w) for w in widths],
        out_shape=[jax.ShapeDtypeStruct((t, w), f32) for w in widths],
        compiler_params=_cparams("parallel"),
        name="proj",
    )(h, g, *ws, *bs)


def _gl_bidir(loads, acc_refs, st_refs, *, s, c, dk):
    h, dv = HEADS, HEAD_W
    kw, vw, hc = h * dk, h * dv, h * c
    n, un = s // c, GL_UNROLL
    khead = _iota((c, kw), 1) // dk
    vhead = _iota((c, vw), 1) // dv
    sidx, trow = _iota((c, hc), 1) % c, _iota((c, hc), 0)
    bd = (_iota((vw, kw), 0) // dv) == (_iota((vw, kw), 1) // dk)
    tris = (_tri(c, False), _tri(c, True))
    causals = (sidx <= trow, sidx >= trow)
    for st_ref in st_refs:
        st_ref[...] = jnp.zeros((vw, kw), f32)

    streams = [(d, u) for u in range(un) for d in range(2)]

    def body(ci, carry):
        idx = [(ci * un + u) if d == 0 else (n - 1 - ci * un - u) for d, u in streams]
        sts = [pl.multiple_of(i * c, c) for i in idx]
        ins = [loads[d](st) for (d, _), st in zip(streams, sts)]
        qs, ks, vs = [x[0] for x in ins], [x[1] for x in ins], [x[2] for x in ins]
        bs = [_lmul3(tris[d], x[3]) for (d, _), x in zip(streams, ins)]
        lasts = [b[0:1] if d == 1 else b[c - 1:c] for (d, _), b in zip(streams, bs)]
        mids = [b[c // 2:c // 2 + 1] for b in bs]
        qts = [(q * jnp.exp(jnp.minimum(b - m, EXP_CLAMP))).astype(bf16) for q, b, m in zip(qs, bs, mids)]
        kts = [k * jnp.exp(jnp.minimum(m - b, EXP_CLAMP)) for k, b, m in zip(ks, bs, mids)]
        kexps = [jnp.concatenate([jnp.where(khead == j, kt, 0.0).astype(bf16) for j in range(h)], axis=0)
                 for kt in kts]
        vexps = [jnp.concatenate([jnp.where(vhead == j, v, 0.0).astype(bf16) for j in range(h)], axis=0)
                 for v in vs]
        atts = [jnp.where(causals[d], _mm_nt(qt, kexp), 0.0).astype(bf16)
                for (d, _), qt, kexp in zip(streams, qts, kexps)]
        os_ = [_mm(att, vexp) for att, vexp in zip(atts, vexps)]
        qis = [(q * jnp.exp(b)).astype(bf16) for q, b in zip(qs, bs)]
        kds = [(k * jnp.exp(l - b)).astype(bf16) for k, l, b in zip(ks, lasts, bs)]
        upds = [jnp.where(bd, _mm_tn(v.astype(bf16), kd), 0.0) for v, kd in zip(vs, kds)]
        decs = [jnp.exp(l) for l in lasts]
        states = [st_ref[...] for st_ref in st_refs]
        for i, (d, _) in enumerate(streams):
            acc_refs[d][pl.ds(sts[i], c), :] = os_[i] + _mm_nt(qis[i], states[d].astype(bf16))
            states[d] = states[d] * decs[i] + upds[i]
        for d in range(2):
            st_refs[d][...] = states[d]
        return carry

    lax.fori_loop(0, n // un, body, 0)


def _gla_kernel(z_ref, wlr_ref, blr_ref, g_ref, o_ref, gk_ref, accf_ref, accb_ref, stf_ref, stb_ref, *, s):
    rb = ROW_BLOCK

    def gates(st, i):
        lr = z_ref[pl.ds(st, rb), ZA['lr']].astype(bf16)
        x = _mm(lr, wlr_ref[...]) + blr_ref[...]
        gk_ref[pl.ds(st, rb), :] = jax.nn.log_sigmoid(x) * (1.0 / GLA_NORMALIZER)
    _rows(s, rb, gates)

    scale = GLA_DK ** -0.5

    def load(st, d):
        q = z_ref[pl.ds(st, GLA_CHUNK), ZA['q']] * scale
        k = z_ref[pl.ds(st, GLA_CHUNK), ZA['k']]
        v = z_ref[pl.ds(st, GLA_CHUNK), ZA['v']]
        g = gk_ref[pl.ds(st, GLA_CHUNK), d * 128:(d + 1) * 128]
        return q, k, v, g
    _gl_bidir([functools.partial(load, d=d) for d in range(2)], (accf_ref, accb_ref), (stf_ref, stb_ref),
              s=s, c=GLA_CHUNK, dk=GLA_DK)

    mavg = _block_avg(BRANCH_W, HEAD_W)

    def out(st, i):
        y = _head_rmsnorm(accf_ref[pl.ds(st, rb), :] + accb_ref[pl.ds(st, rb), :], g_ref[...], mavg)
        gate = z_ref[pl.ds(st, rb), ZA['g']]
        o_ref[pl.ds(st, rb), :] = y * (gate * jax.nn.sigmoid(gate))
    _rows(s, rb, out)


def _gla(za, wlr, blr, g, nb, s):
    return pl.pallas_call(
        functools.partial(_gla_kernel, s=s),
        grid=(nb,),
        in_specs=[pl.BlockSpec((None, s, ZA_W), lambda b: (b, 0, 0)),
                  _const_spec((LANES, 256)), _const_spec((1, 256)), _const_spec((1, BRANCH_W))],
        out_specs=pl.BlockSpec((None, s, BRANCH_W), lambda b: (b, 0, 0)),
        out_shape=jax.ShapeDtypeStruct((nb, s, BRANCH_W), f32),
        scratch_shapes=[pltpu.VMEM((s, 256), f32)] + [pltpu.VMEM((s, BRANCH_W), f32)] * 2
        + [pltpu.VMEM((BRANCH_W, HEADS * GLA_DK), f32)] * 2,
        compiler_params=_cparams("parallel"),
        name="gla",
    )(za.reshape(nb, s, ZA_W), wlr, blr, g)


def _hgrn_kernel(z_ref, lbp_ref, g_ref, o_ref, q_ref, lf_ref, kd_ref, accf_ref, accb_ref, stf_ref, stb_ref,
                 *, s, layer):
    rb = ROW_BLOCK
    x = lbp_ref[...]
    e = jnp.exp(x - jnp.max(x, axis=0, keepdims=True))
    p = e / jnp.sum(e, axis=0, keepdims=True)
    lb = p[0:1]
    for j in range(1, layer + 1):
        lb = lb + p[j:j + 1]
    lb = lb - p[0:1]

    def prep(st, i):
        qz = z_ref[pl.ds(st, rb), ZD['q']]
        q_ref[pl.ds(st, rb), :] = qz * jax.nn.sigmoid(qz)
        fz = z_ref[pl.ds(st, rb), ZD['f']]
        lf_ref[pl.ds(st, rb), :] = jnp.log(lb + (1.0 - lb) * jax.nn.sigmoid(fz))
        kd_ref[pl.ds(st, rb), :] = (1.0 - lb) * jax.nn.sigmoid(-fz)
    _rows(s, rb, prep)

    kw = HEADS * HGRN_DF

    def load(st, d):
        q = q_ref[pl.ds(st, HGRN_CHUNK), :]
        k = kd_ref[pl.ds(st, HGRN_CHUNK), d * kw:(d + 1) * kw]
        v = z_ref[pl.ds(st, HGRN_CHUNK), ZD['i']]
        g = lf_ref[pl.ds(st, HGRN_CHUNK), d * kw:(d + 1) * kw]
        return q, k, v, g
    _gl_bidir([functools.partial(load, d=d) for d in range(2)], (accf_ref, accb_ref), (stf_ref, stb_ref),
              s=s, c=HGRN_CHUNK, dk=HGRN_DF)

    mavg = _block_avg(BRANCH_W, HEAD_W)

    def out(st, i):
        y = _head_rmsnorm(accf_ref[pl.ds(st, rb), :] + accb_ref[pl.ds(st, rb), :], g_ref[...], mavg)
        gate = z_ref[pl.ds(st, rb), ZD['g']]
        o_ref[pl.ds(st, rb), :] = y * (gate * jax.nn.sigmoid(gate))
    _rows(s, rb, out)


def _hgrn(zd, lbp, g, nb, s, layer):
    kw = HEADS * HGRN_DF
    return pl.pallas_call(
        functools.partial(_hgrn_kernel, s=s, layer=layer),
        grid=(nb,),
        in_specs=[pl.BlockSpec((None, s, ZD_W), lambda b: (b, 0, 0)),
                  _const_spec((DEPTH, 2 * kw)), _const_spec((1, BRANCH_W))],
        out_specs=pl.BlockSpec((None, s, BRANCH_W), lambda b: (b, 0, 0)),
        out_shape=jax.ShapeDtypeStruct((nb, s, BRANCH_W), f32),
        scratch_shapes=[pltpu.VMEM((s, kw), f32), pltpu.VMEM((s, 2 * kw), f32), pltpu.VMEM((s, 2 * kw), f32)]
        + [pltpu.VMEM((s, BRANCH_W), f32)] * 2 + [pltpu.VMEM((BRANCH_W, kw), f32)] * 2,
        compiler_params=_cparams("parallel"),
        name="hgrn",
    )(zd.reshape(nb, s, ZD_W), lbp, g)


def _mlstm_bidir(z_ref, q_ref, k_ref, acc_refs, st_refs, nv_refs, m_refs, *, s):
    c, w = MLSTM_CHUNK, BRANCH_W
    n, un = s // c, MLSTM_UNROLL
    ones = jnp.ones((c, c), bf16)
    lane, row = _iota((c, w), 1), _iota((c, w), 0)
    head, sidx = lane // HEAD_W, lane % HEAD_W
    diag = sidx == row
    same = (_iota((w, w), 0) // HEAD_W) == (_iota((w, w), 1) // HEAD_W)
    bones = jnp.where(same, 1.0, 0.0).astype(bf16)
    tris = (_tri(c, False), _tri(c, True))
    causals = (sidx <= row, sidx >= row)
    for d in range(2):
        st_refs[d][...] = jnp.zeros((w, w), f32)
        nv_refs[d][...] = jnp.zeros((1, w), f32)
        m_refs[d][...] = jnp.zeros((1, w), f32)

    streams = [(d, u) for u in range(un) for d in range(2)]

    def seg_max(x):
        out = jnp.full((c, w), NEG_BIG, f32)
        for j in range(HEADS):
            mj = jnp.max(jnp.where(head == j, x, NEG_BIG), axis=-1, keepdims=True)
            out = jnp.where(head == j, mj, out)
        return out

    def body(ci, carry):
        idx = [(ci * un + u) if d == 0 else (n - 1 - ci * un - u) for d, u in streams]
        sts = [pl.multiple_of(i * c, c) for i in idx]
        qs = [q_ref[pl.ds(st, c), :] for st in sts]
        ks = [k_ref[pl.ds(st, c), :] for st in sts]
        vs = [z_ref[pl.ds(st, c), ZB['v']] for st in sts]
        igs = [z_ref[pl.ds(st, c), _sub(ZB['i'], d, w)] for (d, _), st in zip(streams, sts)]
        lfs = [jax.nn.log_sigmoid(z_ref[pl.ds(st, c), _sub(ZB['f'], d, w)])
               for (d, _), st in zip(streams, sts)]
        bs = [_lmul3(tris[d], lf) for (d, _), lf in zip(streams, lfs)]
        rs = [_lmul3(ones, jnp.where(diag, ig - b, 0.0)) for ig, b in zip(igs, bs)]
        log_ds = [jnp.where(causals[d], b + r, NEG_BIG) for (d, _), b, r in zip(streams, bs, rs)]
        mrows = [seg_max(x) for x in log_ds]
        lasts = [b[0:1] if d == 1 else b[c - 1:c] for (d, _), b in zip(streams, bs)]
        log_ws = [l - b + ig for l, b, ig in zip(lasts, bs, igs)]
        mws = [jnp.max(x, axis=0, keepdims=True) for x in log_ws]
        qbs = [q.astype(bf16) for q in qs]
        kexps = [jnp.concatenate([jnp.where(head == j, k, 0.0).astype(bf16) for j in range(HEADS)], axis=0)
                 for k in ks]
        vexps = [jnp.concatenate([jnp.where(head == j, v, 0.0).astype(bf16) for j in range(HEADS)], axis=0)
                 for v in vs]
        qks = [_mm_nt(qb, kexp) for qb, kexp in zip(qbs, kexps)]
        ms = [m_ref[...] for m_ref in m_refs]
        m_ts, w_prevs, w_ss, decs = [], [], [], []
        for i, (d, _) in enumerate(streams):
            log_prev = bs[i] + ms[d]
            m_t = jnp.maximum(mrows[i], log_prev)
            m_new = jnp.maximum(lasts[i] + ms[d], mws[i])
            m_ts.append(m_t)
            w_prevs.append(jnp.exp(log_prev - m_t))
            w_ss.append(jnp.exp(log_ws[i] - m_new))
            decs.append(jnp.exp(lasts[i] + ms[d] - m_new))
            ms[d] = m_new
        scs = [qk * jnp.exp(log_d - m_t) for qk, log_d, m_t in zip(qks, log_ds, m_ts)]
        nums = [_mm(sc.astype(bf16), vexp) for sc, vexp in zip(scs, vexps)]
        dens = [_rmul3(sc, bones) for sc in scs]
        upds = [jnp.where(same, _mm_tn((v * w_s).astype(bf16), k.astype(bf16)), 0.0)
                for v, w_s, k in zip(vs, w_ss, ks)]
        nvus = [jnp.sum(w_s * k, axis=0, keepdims=True) for w_s, k in zip(w_ss, ks)]
        states = [st_ref[...] for st_ref in st_refs]
        nvs = [nv_ref[...] for nv_ref in nv_refs]
        for i, (d, _) in enumerate(streams):
            num = w_prevs[i] * _mm_nt(qbs[i], states[d].astype(bf16)) + nums[i]
            den = w_prevs[i] * _rmul3(qs[i] * nvs[d], bones) + dens[i]
            acc_refs[d][pl.ds(sts[i], c), :] = num / jnp.maximum(jnp.abs(den), jnp.exp(-m_ts[i]))
            states[d] = decs[i] * states[d] + upds[i]
            nvs[d] = decs[i] * nvs[d] + nvus[i]
        for d in range(2):
            st_refs[d][...] = states[d]
            nv_refs[d][...] = nvs[d]
            m_refs[d][...] = ms[d]
        return carry

    lax.fori_loop(0, n // un, body, 0)


def _mlstm_kernel(z_ref, cw_ref, cb_ref, g_ref, o_ref, q_ref, k_ref, accf_ref, accb_ref, stf_ref, stb_ref,
                  nvf_ref, nvb_ref, mf_ref, mb_ref, *, s):
    rb = ROW_BLOCK
    nblk = s // rb
    w = BRANCH_W

    qk = slice(ZB['q'].start, ZB['k'].stop)

    def conv(st, i):
        x = z_ref[pl.ds(st, rb), qk]
        pst = pl.multiple_of(jnp.maximum(st - 8, 0), 8)
        nst = pl.multiple_of(jnp.minimum(st + rb, s - 8), 8)
        prev = z_ref[pl.ds(pst, 8), qk][7:8]
        nxt = z_ref[pl.ds(nst, 8), qk][0:1]
        prev = jnp.where(i == 0, 0.0, prev)
        nxt = jnp.where(i == nblk - 1, 0.0, nxt)
        row = _iota((rb, 2 * w), 0)
        xm = jnp.where(row == 0, prev, pltpu.roll(x, 1, 0))
        xp = jnp.where(row == rb - 1, nxt, pltpu.roll(x, rb - 1, 0))
        y = cb_ref[...] + xm * cw_ref[0:1, :]
        y = y + x * cw_ref[1:2, :]
        y = y + xp * cw_ref[2:3, :]
        y = y * jax.nn.sigmoid(y)
        q_ref[pl.ds(st, rb), :] = y[:, 0:w]
        k_ref[pl.ds(st, rb), :] = y[:, w:2 * w] * (HEAD_W ** -0.5)
    _rows(s, rb, conv)

    _mlstm_bidir(z_ref, q_ref, k_ref, (accf_ref, accb_ref), (stf_ref, stb_ref), (nvf_ref, nvb_ref),
                 (mf_ref, mb_ref), s=s)

    mavg = _block_avg(w, HEAD_W)

    def out(st, i):
        y = _head_rmsnorm(accf_ref[pl.ds(st, rb), :] + accb_ref[pl.ds(st, rb), :], g_ref[...], mavg)
        o_ref[pl.ds(st, rb), :] = jax.nn.sigmoid(z_ref[pl.ds(st, rb), ZB['o']]) * y
    _rows(s, rb, out)


def _mlstm(zb, cw, cb, g, nb, s):
    w = BRANCH_W
    return pl.pallas_call(
        functools.partial(_mlstm_kernel, s=s),
        grid=(nb,),
        in_specs=[pl.BlockSpec((None, s, ZB_W), lambda b: (b, 0, 0)),
                  _const_spec((3, 2 * w)), _const_spec((1, 2 * w)), _const_spec((1, w))],
        out_specs=pl.BlockSpec((None, s, w), lambda b: (b, 0, 0)),
        out_shape=jax.ShapeDtypeStruct((nb, s, w), f32),
        scratch_shapes=[pltpu.VMEM((s, w), f32)] * 4 + [pltpu.VMEM((w, w), f32)] * 2 + [pltpu.VMEM((1, w), f32)] * 4,
        compiler_params=_cparams("parallel"),
        name="mlstm",
    )(zb.reshape(nb, s, ZB_W), cw, cb, g)


def _rope(x, cos, sa, sb):
    w = x.shape[-1]
    half = HEAD_W // 2
    return x * cos + pltpu.roll(x, w - half, 1) * sa + pltpu.roll(x, half, 1) * sb


def _dil_bias(s):
    rel = jnp.arange(s, dtype=i32)[None, :] - jnp.arange(s, dtype=i32)[:, None]
    dist = jnp.abs(rel)
    cnt = jnp.zeros((s, s), f32)
    for window, dil in DIL_PATTERNS:
        ok = dist <= (window // (2 * dil)) * dil
        if dil > 1:
            ok = jnp.logical_and(ok, (rel & (dil - 1)) == 0)
        cnt = cnt + jnp.where(ok, 1.0, 0.0)
    return jnp.where(cnt > 0.0, jnp.log(jnp.maximum(cnt, 1.0)), NEG_BIG)


def _dil_kernel(z_ref, cos_ref, sa_ref, sb_ref, bias_ref, o_ref, kr_ref, vm_ref, *, s, tq):
    w = BRANCH_W
    qi = pl.program_id(1)
    rb = ROW_BLOCK

    @pl.when(qi == 0)
    def _():
        head = _iota((rb, w), 1) // HEAD_W

        def prep(st, i):
            x = z_ref[pl.ds(st, rb), ZC['k']]
            kr_ref[pl.ds(st, rb), :] = _rope(x, cos_ref[pl.ds(st, rb), :], sa_ref[pl.ds(st, rb), :],
                                             sb_ref[pl.ds(st, rb), :]).astype(bf16)
            v = z_ref[pl.ds(st, rb), ZC['v']]
            for j in range(HEADS):
                vm_ref[j, pl.ds(st, rb), :] = jnp.where(head == j, v, 0.0).astype(bf16)
        _rows(s, rb, prep)

    q0 = pl.multiple_of(qi * tq, tq)
    qr = _rope(z_ref[pl.ds(q0, tq), ZC['q']], cos_ref[pl.ds(q0, tq), :], sa_ref[pl.ds(q0, tq), :],
               sb_ref[pl.ds(q0, tq), :]) * (HEAD_W ** -0.5)
    qhead = _iota((tq, w), 1) // HEAD_W
    acc = jnp.zeros((tq, w), f32)
    for j in range(HEADS):
        qh = jnp.where(qhead == j, qr, 0.0).astype(bf16)
        sc = _mm_nt(qh, kr_ref[...]) + bias_ref[...]
        m = jnp.max(sc, axis=-1, keepdims=True)
        p = jnp.exp(sc - m)
        l = jnp.sum(p, axis=-1, keepdims=True)
        acc = acc + _mm(p.astype(bf16), vm_ref[j]) / l
    o_ref[...] = acc


def _dil(zc, cos, sa, sb, bias, nb, s, tq=512):
    w = BRANCH_W
    for _, dil in DIL_PATTERNS:
        assert dil & (dil - 1) == 0 and s % dil == 0
    return pl.pallas_call(
        functools.partial(_dil_kernel, s=s, tq=tq),
        grid=(nb, s // tq),
        in_specs=[pl.BlockSpec((None, s, ZC_W), lambda b, i: (b, 0, 0)),
                  _const_spec((s, w)), _const_spec((s, w)), _const_spec((s, w)),
                  pl.BlockSpec((tq, s), lambda b, i: (i, 0))],
        out_specs=pl.BlockSpec((None, tq, w), lambda b, i: (b, i, 0)),
        out_shape=jax.ShapeDtypeStruct((nb, s, w), f32),
        scratch_shapes=[pltpu.VMEM((s, w), bf16), pltpu.VMEM((HEADS, s, w), bf16)],
        compiler_params=_cparams("parallel", "arbitrary"),
        name="dilattn",
    )(zc.reshape(nb, s, ZC_W), cos, sa, sb, bias)


def _merge_kernel(h_ref, ya, yb, yc, yd, g_ref, wg_ref, wbr_ref, wo_ref, o_ref):
    x = h_ref[...]
    u = _rms(x, g_ref[...]).astype(bf16)
    merged = None
    for n, y in enumerate((ya, yb, yc, yd)):
        gate = jax.nn.sigmoid(_mm(u, wg_ref[n]))
        term = gate * _mm(y[...].astype(bf16), wbr_ref[n])
        merged = term if merged is None else merged + term
    o_ref[...] = x + _mm(merged.astype(bf16), wo_ref[...])


def _merge(h, ys, g, wg, wbr, wo, tm=512):
    t = h.shape[0]
    tok = lambda w: pl.BlockSpec((tm, w), lambda i: (i, 0))
    return pl.pallas_call(
        _merge_kernel,
        grid=(t // tm,),
        in_specs=[tok(D_MODEL)] + [tok(BRANCH_W)] * 4
        + [_const_spec((1, D_MODEL)), _const_spec((4, D_MODEL, D_MODEL)),
           _const_spec((4, BRANCH_W, D_MODEL)), _const_spec((D_MODEL, D_MODEL))],
        out_specs=tok(D_MODEL),
        out_shape=jax.ShapeDtypeStruct((t, D_MODEL), f32),
        compiler_params=_cparams("parallel"),
        name="merge",
    )(h, *ys, g, wg, wbr, wo)


def _stair_cells():
    return [(a, b) for a in range(PEER_TOPK) for b in range(PEER_TOPK) if (a + 1) * (b + 1) <= PEER_TOPK]


_CELLS = _stair_cells()
_NCELL = len(_CELLS)
_NCELL_PAD = -(-_NCELL // 8) * 8


def _extract_top(works, rowf, nrows):
    works = list(works)
    tops = [[] for _ in works]
    ranks = [jnp.full(w.shape, float(PEER_TOPK), f32) for w in works]
    for r in range(PEER_TOPK):
        mxs = [jnp.max(w, axis=0, keepdims=True) for w in works]
        cands = [jnp.where(w == mx, rowf, float(nrows)) for w, mx in zip(works, mxs)]
        hits = [cand == jnp.min(cand, axis=0, keepdims=True) for cand in cands]
        works = [jnp.where(hit, -jnp.inf, w) for hit, w in zip(hits, works)]
        ranks = [jnp.where(hit, float(r), rank) for hit, rank in zip(hits, ranks)]
        for t, mx in zip(tops, mxs):
            t.append(mx)
    return tops, ranks


def _distinct_maxima(xs):
    tops = [[] for _ in xs]
    mxs = [None] * len(xs)
    for _ in range(PEER_TOPK):
        curs = [x if mx is None else jnp.where(x < mx, x, -jnp.inf) for x, mx in zip(xs, mxs)]
        mxs = [jnp.max(cur, axis=0, keepdims=True) for cur in curs]
        for t, mx in zip(tops, mxs):
            t.append(mx)
    return tops


def _count(mask):
    return jnp.sum(jnp.where(mask, 1.0, 0.0), axis=0, keepdims=True)


def _route_kernel(h_ref, g_ref, wpq_ref, keys_ref, sa_ref, ut_ref, a_ref, cnt_ref, rk_ref, bn_ref,
                  q_ref, *m_refs, tm):
    uf = _rms(h_ref[...], g_ref[...])
    ut_ref[...] = uf.T.astype(bf16)
    q_ref[...] = _mm(uf.astype(bf16), wpq_ref[...]).astype(bf16)
    nk = PEER_NKEYS
    topk = float(PEER_TOPK)

    def scores(hd, toks):
        return [_mm_nt(keys_ref[hd * 2 + p], q_ref[toks, pl.ds(pl.multiple_of((hd * 2 + p) * nk, nk), nk)])
                for p in range(2)]

    def write_tables(hd, amat, cnt, rank2, bnorm):
        rows = pl.ds(pl.multiple_of(hd * nk, nk), nk)
        prow = pl.ds(pl.multiple_of(hd * (nk // 2), nk // 2), nk // 2)
        rkp = pltpu.bitcast(rank2.astype(bf16), jnp.uint32)
        bnp = pltpu.bitcast(bnorm.astype(bf16), jnp.uint32)
        amat = 0.5 * amat
        for tb in range(tm // LANES):
            lanes = slice(tb * LANES, (tb + 1) * LANES)
            a_ref[tb, rows, :] = amat[:, lanes]
            cnt_ref[tb, rows, :] = cnt[:, lanes]
            rk_ref[tb, prow, :] = rkp[:, lanes]
            bn_ref[tb, prow, :] = bnp[:, lanes]

    def stair_sums(tops, m_ref):
        m_ref[...] = jnp.full((_NCELL_PAD, tm), -jnp.inf, f32)
        for ci, (a, b) in enumerate(_CELLS):
            m_ref[ci:ci + 1, :] = tops[0][a] + tops[1][b]
        return m_ref[...]

    def fast_tiles(hds):
        scs = [scores(hd, slice(0, tm)) for hd in hds]
        flat = _distinct_maxima([s for pair in scs for s in pair])
        tops = [flat[2 * i:2 * i + 2] for i in range(len(hds))]
        rank2s = [jnp.zeros((nk, tm), f32) for _ in hds]
        for r in range(PEER_TOPK):
            rank2s = [rank2 + jnp.where(sc[1] < top[1][r], 1.0, 0.0) for rank2, sc, top in zip(rank2s, scs, tops)]
        tied = jnp.zeros((1, tm), f32)
        for sc, top, rank2 in zip(scs, tops, rank2s):
            tied = jnp.where(_count(sc[0] >= top[0][-1]) == topk, tied, 1.0)
            tied = jnp.where(_count(rank2 < topk) == topk, tied, 1.0)
        cands = [stair_sums(top, m_ref) for top, m_ref in zip(tops, m_refs)]
        ctops = _distinct_maxima(cands)
        for hd, sc, top, rank2, cand, ctop in zip(hds, scs, tops, rank2s, cands, ctops):
            chosen = cand >= ctop[-1]
            tied = jnp.where(_count(chosen) == topk, tied, 1.0)
            zsum = jnp.sum(jnp.where(chosen, jnp.exp(cand - cand[0:1]), 0.0), axis=0, keepdims=True)
            rowcnt = _mm(sa_ref[...], jnp.where(chosen, 1.0, 0.0).astype(bf16))
            cnt = jnp.zeros((nk, tm), f32)
            for a in range(PEER_TOPK):
                cnt = jnp.where(sc[0] == top[0][a], rowcnt[a:a + 1], cnt)
            write_tables(hd, jnp.exp(sc[0] - top[0][0]), cnt, rank2, jnp.exp(sc[1] - top[1][0]) / zsum)
        return tied

    def exact_tile(hd, m_ref):
        rowf = _iota((nk, tm), 0).astype(f32)
        rowcf = _iota((_NCELL_PAD, tm), 0).astype(f32)
        scs = scores(hd, slice(0, tm))
        tops, ranks = _extract_top(scs, rowf, nk)
        cand = stair_sums(tops, m_ref)
        _, (crank,) = _extract_top([cand], rowcf, _NCELL_PAD)
        chosen = crank < topk
        zsum = jnp.sum(jnp.where(chosen, jnp.exp(cand - cand[0:1]), 0.0), axis=0, keepdims=True)
        rowcnt = _mm(sa_ref[...], jnp.where(chosen, 1.0, 0.0).astype(bf16))
        cnt = jnp.zeros((nk, tm), f32)
        for a in range(PEER_TOPK):
            cnt = jnp.where(ranks[0] == float(a), rowcnt[a:a + 1], cnt)
        write_tables(hd, jnp.exp(scs[0] - tops[0][0]), cnt, ranks[1], jnp.exp(scs[1] - tops[1][0]) / zsum)

    def per_group(hg, carry):
        hds = [hg * ROUTE_HEADS_TOGETHER + j for j in range(ROUTE_HEADS_TOGETHER)]
        tied = fast_tiles(hds)

        @pl.when(jnp.max(tied) > 0.0)
        def _():
            for hd, m_ref in zip(hds, m_refs):
                exact_tile(hd, m_ref)
        return carry

    lax.fori_loop(0, PEER_HEADS // ROUTE_HEADS_TOGETHER, per_group, 0)


def _route(h, g, wpq, keys, sa, tm=256):
    t = h.shape[0]
    nh, nk = PEER_HEADS, PEER_NKEYS
    tab = pl.BlockSpec((tm // LANES, nh * nk, LANES), lambda i: (i, 0, 0))
    ptab = pl.BlockSpec((tm // LANES, nh * nk // 2, LANES), lambda i: (i, 0, 0))
    return pl.pallas_call(
        functools.partial(_route_kernel, tm=tm),
        grid=(t // tm,),
        in_specs=[pl.BlockSpec((tm, D_MODEL), lambda i: (i, 0)), _const_spec((1, D_MODEL)),
                  _const_spec((D_MODEL, 2 * nh * nk)), _const_spec((2 * nh, nk, nk)),
                  _const_spec((PEER_TOPK, _NCELL_PAD))],
        out_specs=[pl.BlockSpec((D_MODEL, tm), lambda i: (0, i)), tab, tab, ptab, ptab],
        out_shape=[jax.ShapeDtypeStruct((D_MODEL, t), bf16)] + [jax.ShapeDtypeStruct((t // LANES, nh * nk, LANES), f32)] * 2
        + [jax.ShapeDtypeStruct((t // LANES, nh * nk // 2, LANES), jnp.uint32)] * 2,
        scratch_shapes=[pltpu.VMEM((tm, 2 * nh * nk), bf16)]
        + [pltpu.VMEM((_NCELL_PAD, tm), f32)] * ROUTE_HEADS_TOGETHER,
        compiler_params=_cparams("parallel"),
        name="route",
    )(h, g, wpq, keys, sa)


def _twice_gelu_bf16(x):
    return (x * (1.0 + lax.erf(x * math.sqrt(0.5)))).astype(bf16)


def _peer_gate_block(act_ref, p_ref, tabs, i0, ii, tb):
    a_ref, cnt_ref, rk_ref, bn_ref = tabs
    nk = PEER_NKEYS
    lanes = slice(tb * LANES, (tb + 1) * LANES)
    gate = None
    for hd in range(PEER_HEADS):
        row = pl.ds(hd * nk + i0 + ii, 1)
        a = jnp.broadcast_to(a_ref[tb, row, :], (nk, LANES)).astype(bf16)
        c = jnp.broadcast_to(cnt_ref[tb, row, :], (nk, LANES)).astype(bf16)
        prow = slice(hd * (nk // 2), (hd + 1) * (nk // 2))
        rk = pltpu.bitcast(rk_ref[tb, prow, :], bf16)
        bn = pltpu.bitcast(bn_ref[tb, prow, :], bf16)
        term = a * jnp.where(rk < c, bn, jnp.zeros((), bf16))
        gate = term if gate is None else gate + term
    p_ref[ii * nk:(ii + 1) * nk, lanes] = gate * act_ref[ii * nk:(ii + 1) * nk, lanes]


def _mxu_product(rhs_tile, lhs_tile, between):
    nt = PEER_SUB // MXU_TILE
    for q in range(2):
        pltpu.matmul_push_rhs(rhs_tile(0, q), staging_register=0, mxu_index=q)
    for kt in range(nt):
        reg = kt % 2
        for mc in range(nt):
            lhs = lhs_tile(mc, kt)
            for q in range(2):
                pltpu.matmul_acc_lhs(mc * (MXU_TILE // 4), lhs, mxu_index=q,
                                     load_staged_rhs=(reg if mc == 0 else None))
            if mc == 0 and kt + 1 < nt:
                for q in range(2):
                    pltpu.matmul_push_rhs(rhs_tile(kt + 1, q), staging_register=1 - reg, mxu_index=q)
            between()


def _peer_kernel(ut_ref, a_ref, cnt_ref, rk_ref, bn_ref, pu_ref, pvt_ref, o_ref, acc_ref, act0, act1, p0, p1,
                 *, ne, tp):
    g = pl.program_id(0)
    nblocks = pl.num_programs(0) - 2
    b1 = jnp.clip(g - 1, 0, nblocks - 1)
    b2 = jnp.clip(g - 2, 0, nblocks - 1)
    i1 = (b1 % ne) * (PEER_SUB // PEER_NKEYS)
    tabs = (a_ref, cnt_ref, rk_ref, bn_ref)

    @pl.when(g == 0)
    def _():
        for ref in (act0, act1, p0, p1):
            ref[...] = jnp.zeros(ref.shape, bf16)

    @pl.when(b2 % ne == 0)
    def _():
        acc_ref[...] = jnp.zeros(acc_ref.shape, f32)

    t256 = lambda i: slice(i * MXU_TILE, (i + 1) * MXU_TILE)

    def stage(act_w, act_r, p_w, p_r):
        blocks = iter([(ii, tb) for ii in range(PEER_SUB // PEER_NKEYS) for tb in range(tp // LANES)])

        def one_gate_block():
            ii, tb = next(blocks)
            _peer_gate_block(act_r, p_w, tabs, i1, ii, tb)

        nt = PEER_SUB // MXU_TILE
        _mxu_product(lambda kt, q: ut_ref[t256(kt), t256(q)], lambda mc, kt: pu_ref[t256(mc), t256(kt)],
                     one_gate_block)
        for mc in range(nt):
            for q in range(2):
                x = pltpu.matmul_pop(mc * (MXU_TILE // 4), (MXU_TILE, MXU_TILE), f32, q)
                act_w[t256(mc), t256(q)] = _twice_gelu_bf16(x)
        _mxu_product(lambda kt, q: p_r[t256(kt), t256(q)], lambda mc, kt: pvt_ref[t256(mc), t256(kt)],
                     one_gate_block)
        for mc in range(nt):
            for q in range(2):
                acc_ref[t256(mc), t256(q)] += pltpu.matmul_pop(mc * (MXU_TILE // 4), (MXU_TILE, MXU_TILE), f32, q)

    @pl.when(g % 2 == 0)
    def _():
        stage(act0, act1, p1, p0)

    @pl.when(g % 2 == 1)
    def _():
        stage(act1, act0, p0, p1)

    @pl.when(jnp.logical_and(g >= 2, b2 % ne == ne - 1))
    def _():
        o_ref[...] = acc_ref[...].T


def _peer(ut, tabs, pu, pvt):
    t = ut.shape[1]
    tp = 2 * MXU_TILE
    nh, nk = PEER_HEADS, PEER_NKEYS
    assert D_MODEL == PEER_SUB and (PEER_SUB // nk) * (tp // LANES) == 2 * (PEER_SUB // MXU_TILE) ** 2
    ne = pu.shape[0] // PEER_SUB
    nblocks = (t // tp) * ne
    blk = lambda lag: (lambda g: jnp.clip(g - lag, 0, nblocks - 1))
    b0, b1, b2 = blk(0), blk(1), blk(2)
    tab = pl.BlockSpec((tp // LANES, nh * nk, LANES), lambda g: (b1(g) // ne, 0, 0))
    ptab = pl.BlockSpec((tp // LANES, nh * nk // 2, LANES), lambda g: (b1(g) // ne, 0, 0))
    return pl.pallas_call(
        functools.partial(_peer_kernel, ne=ne, tp=tp),
        grid=(nblocks + 2,),
        in_specs=[pl.BlockSpec((D_MODEL, tp), lambda g: (0, b0(g) // ne)), tab, tab, ptab, ptab,
                  pl.BlockSpec((PEER_SUB, D_MODEL), lambda g: (b0(g) % ne, 0)),
                  pl.BlockSpec((D_MODEL, PEER_SUB), lambda g: (0, b2(g) % ne))],
        out_specs=pl.BlockSpec((tp, D_MODEL), lambda g: (b2(g) // ne, 0)),
        out_shape=jax.ShapeDtypeStruct((t, D_MODEL), f32),
        scratch_shapes=[pltpu.VMEM((D_MODEL, tp), f32)] + [pltpu.VMEM((PEER_SUB, tp), bf16)] * 4,
        compiler_params=_cparams("arbitrary"),
        name="peer",
    )(ut, *tabs, pu, pvt)


def _pe_kernel(h_ref, po_ref, p_ref, g_ref, wpg_ref, wpe_ref, gf_ref, o_ref, *, final):
    h = h_ref[...] + po_ref[...]
    gate = jax.nn.sigmoid(_mm(_rms(h, g_ref[...]).astype(bf16), wpg_ref[...]))
    h = h + gate * _mm(p_ref[...].astype(bf16), wpe_ref[...])
    if final:
        h = _rms(h, gf_ref[...])
    o_ref[...] = h


def _pe(h, po, p, g, wpg, wpe, gf, final, tm=512):
    t = h.shape[0]
    tok = lambda w: pl.BlockSpec((tm, w), lambda i: (i, 0))
    return pl.pallas_call(
        functools.partial(_pe_kernel, final=final),
        grid=(t // tm,),
        in_specs=[tok(D_MODEL), tok(D_MODEL), tok(PE_DIM), _const_spec((1, D_MODEL)),
                  _const_spec((D_MODEL, D_MODEL)), _const_spec((PE_DIM, D_MODEL)), _const_spec((1, D_MODEL))],
        out_specs=tok(D_MODEL),
        out_shape=jax.ShapeDtypeStruct((t, D_MODEL), f32),
        compiler_params=_cparams("parallel"),
        name="pe_embed",
    )(h, po, p, g, wpg, wpe, gf)


def _pe_proj_kernel(h_ref, po_ref, p_ref, g_ref, wpg_ref, wpe_ref, gm_ref, wa, wb, wc, wd, ba, bb, bc, bd,
                    o_ref, za, zb, zc, zd):
    h = h_ref[...] + po_ref[...]
    gate = jax.nn.sigmoid(_mm(_rms(h, g_ref[...]).astype(bf16), wpg_ref[...]))
    h = h + gate * _mm(p_ref[...].astype(bf16), wpe_ref[...])
    o_ref[...] = h
    u = _rms(h, gm_ref[...]).astype(bf16)
    for w, b, z in ((wa, ba, za), (wb, bb, zb), (wc, bc, zc), (wd, bd, zd)):
        z[...] = _mm(u, w[...]) + b[...]


def _pe_proj(h, po, p, g, wpg, wpe, gm, ws, bs, tm=512):
    t = h.shape[0]
    widths = (ZA_W, ZB_W, ZC_W, ZD_W)
    tok = lambda w: pl.BlockSpec((tm, w), lambda i: (i, 0))
    return pl.pallas_call(
        _pe_proj_kernel,
        grid=(t // tm,),
        in_specs=[tok(D_MODEL), tok(D_MODEL), tok(PE_DIM), _const_spec((1, D_MODEL)),
                  _const_spec((D_MODEL, D_MODEL)), _const_spec((PE_DIM, D_MODEL)), _const_spec((1, D_MODEL))]
        + [_const_spec((D_MODEL, w)) for w in widths] + [_const_spec((1, w)) for w in widths],
        out_specs=[tok(D_MODEL)] + [tok(w) for w in widths],
        out_shape=[jax.ShapeDtypeStruct((t, D_MODEL), f32)] + [jax.ShapeDtypeStruct((t, w), f32) for w in widths],
        compiler_params=_cparams("parallel"),
        name="pe_proj",
    )(h, po, p, g, wpg, wpe, gm, *ws, *bs)


def _cols(w, name):
    a, b = _PROJ_OFF[name]
    return w[..., a:b]


def _proj_groups(w):
    lr_real = _PROJ_OFF['a_lr'][1] - _PROJ_OFF['a_lr'][0]
    pad = jnp.zeros(w.shape[:-1] + (ZA['lr'].stop - ZA['lr'].start - lr_real,), w.dtype)
    ga = jnp.concatenate([_cols(w, n) for n in ('a_q', 'a_k', 'a_v', 'a_g', 'a_lr')] + [pad], axis=-1)
    gates = [jnp.repeat(_cols(w, n), HEAD_W, axis=-1) for n in ('b_i', 'b_f')]
    gb = jnp.concatenate([_cols(w, n) for n in ('b_q', 'b_k', 'b_v', 'b_o')] + gates, axis=-1)
    gc = jnp.concatenate([_cols(w, n) for n in ('c_q', 'c_k', 'c_v')], axis=-1)
    gd = jnp.concatenate([_cols(w, n) for n in ('d_q', 'd_f', 'd_i', 'd_g')], axis=-1)
    return ga, gb, gc, gd


def _rope_tables(s):
    half = HEAD_W // 2
    inv = ROPE_THETA ** (-jnp.arange(half, dtype=f32) / half)
    ang = jnp.arange(s, dtype=f32)[:, None] * inv[None, :]
    cos, sin = jnp.cos(ang), jnp.sin(ang)
    zero = jnp.zeros_like(sin)
    tile = lambda a, b: jnp.tile(jnp.concatenate([a, b], axis=-1), (1, HEADS))
    return tile(cos, cos), tile(-sin, zero), tile(zero, sin)


def _row_count_matrix():
    m = np.zeros((PEER_TOPK, _NCELL_PAD), np.float32)
    for ci, (a, _) in enumerate(_CELLS):
        m[a, ci] = 1.0
    return jnp.asarray(m, bf16)


def _prepare(g_mix, w_in, b_in, w_lr2, b_lr2, conv_w, conv_b, lb_param, g_a, g_b, g_d, w_gate, w_br, w_out,
             g_ffn, w_pq, sub_keys, peer_u, peer_v, g_pe, w_pe, w_pg, g_final):
    layers = []
    row = lambda a: a.reshape(1, -1)
    for l in range(DEPTH):
        wlr = jnp.zeros((LANES, 256), f32)
        for n in range(2):
            wlr = wlr.at[n * GLA_RANK:(n + 1) * GLA_RANK, n * 128:(n + 1) * 128].set(w_lr2[l, n])
        layers.append(dict(
            g_mix=row(g_mix[l]),
            w_proj=[w.astype(bf16) for w in _proj_groups(w_in[l])],
            b_proj=list(_proj_groups(row(b_in[l]))),
            wlr=wlr.astype(bf16), blr=row(b_lr2[l]),
            conv_w=conv_w[l], conv_b=row(conv_b[l]),
            g_a=row(g_a[l]), g_b=row(g_b[l]), g_d=row(g_d[l]),
            w_gate=w_gate[l].astype(bf16), w_br=w_br[l].astype(bf16), w_out=w_out[l].astype(bf16),
            g_ffn=row(g_ffn[l]), w_pq=w_pq[l].astype(bf16),
            keys=sub_keys[l].reshape(2 * PEER_HEADS, PEER_NKEYS, -1).astype(bf16),
            peer_u=peer_u[l].astype(bf16), peer_vt=peer_v[l].astype(bf16).T,
            g_pe=row(g_pe[l]), w_pe=w_pe[l].astype(bf16), w_pg=w_pg[l].astype(bf16),
        ))
    shared = dict(
        lbp=jnp.transpose(lb_param, (1, 0, 2)).reshape(DEPTH, -1),
        g_final=row(g_final), sa=_row_count_matrix(),
    )
    return layers, shared


def _trunk(x, p, layers, shared):
    nb, s, d = x.shape
    t = nb * s
    h = x.reshape(t, d)
    cos, sa, sb = _rope_tables(s)
    dil_bias = _dil_bias(s)
    zs = _proj(h, layers[0]['g_mix'], layers[0]['w_proj'], layers[0]['b_proj'])
    for l, w in enumerate(layers):
        za, zb, zc, zd = zs
        ya = _gla(za, w['wlr'], w['blr'], w['g_a'], nb, s)
        yb = _mlstm(zb, w['conv_w'], w['conv_b'], w['g_b'], nb, s)
        yc = _dil(zc, cos, sa, sb, dil_bias, nb, s)
        yd = _hgrn(zd, shared['lbp'], w['g_d'], nb, s, l)
        ys = [y.reshape(t, BRANCH_W) for y in (ya, yb, yc, yd)]
        h1 = _merge(h, ys, w['g_mix'], w['w_gate'], w['w_br'], w['w_out'])
        ut, *tabs = _route(h1, w['g_ffn'], w['w_pq'], w['keys'], shared['sa'])
        po = _peer(ut, tabs, w['peer_u'], w['peer_vt'])
        pe_args = (h1, po, p[l].reshape(t, PE_DIM), w['g_pe'], w['w_pg'], w['w_pe'])
        if l + 1 < DEPTH:
            nxt = layers[l + 1]
            h, *zs = _pe_proj(*pe_args, nxt['g_mix'], nxt['w_proj'], nxt['b_proj'])
        else:
            h = _pe(*pe_args, shared['g_final'], final=True)
    return h.reshape(nb, s, d)


def kernel(x_prompt, x_sample, p_prompt, p_sample, g_mix, w_in, b_in, w_lr2, b_lr2, conv_w, conv_b, lb_param, g_a, g_b, g_d, w_gate, w_br, w_out, g_ffn, w_pq, sub_keys, peer_u, peer_v, g_pe, w_pe, w_pg, g_final):
    layers, shared = _prepare(g_mix, w_in, b_in, w_lr2, b_lr2, conv_w, conv_b, lb_param, g_a, g_b, g_d, w_gate,
                              w_br, w_out, g_ffn, w_pq, sub_keys, peer_u, peer_v, g_pe, w_pe, w_pg, g_final)
    return (_trunk(x_prompt, p_prompt, layers, shared), _trunk(x_sample, p_sample, layers, shared))
```
